```python
import math
import jax, jax.numpy as jnp
from jax import lax
import numpy as np

D_MODEL = 1024
BATCH = 8
SEQ = 2048
DEPTH = 4
DEC_BATCH = 128
DEC_SEQ = 8
PAST_LEN = 8192
PAGE_SIZE = 128

N_MIXERS = 2
N_SSM_LAYERS = (DEPTH + 1) // 2
N_SWA_LAYERS = DEPTH // 2
SSM_GROUP = 16
N_GROUPS = D_MODEL // SSM_GROUP
STATE_DIM = 64
SCAN_CHUNK = 128
HEAD_DIM = 64
N_HEADS = D_MODEL // HEAD_DIM
N_KV_HEADS = 4
KV_REP = N_HEADS // N_KV_HEADS
WINDOW = 128
ATTN_BLOCK = WINDOW
ROT_DIM = HEAD_DIM // 4
ROPE_THETA = 500000.0
ATTN_SCALE = HEAD_DIM ** -0.5
D_FF = ((8 * D_MODEL + 3 * 256 - 1) // (3 * 256)) * 256
NORM_EPS = 1e-6
NEG_INF = -1e30

kernel_name = "hybrid_s5_swa_sink_decoder_step"


def rms_norm(x, g):
    xf = x.astype(jnp.float32)
    y = xf * lax.rsqrt(jnp.mean(xf * xf, axis=-1, keepdims=True) + NORM_EPS)
    return (y * g.astype(jnp.float32)).astype(x.dtype)


def rotary(x, pos):
    half = ROT_DIM // 2
    inv_freq = ROPE_THETA ** (-jnp.arange(half, dtype=jnp.float32) * 2.0 / ROT_DIM)
    ang = pos[:, None] * inv_freq[None, :]
    cos = jnp.cos(ang)[:, None, :]
    sin = jnp.sin(ang)[:, None, :]
    xf = x.astype(jnp.float32)
    x1 = xf[..., :half]
    x2 = xf[..., half:ROT_DIM]
    out = jnp.concatenate([x1 * cos - x2 * sin, x2 * cos + x1 * sin, xf[..., ROT_DIM:]], axis=-1)
    return out.astype(x.dtype)


def cmul(ar, ai, br, bi):
    return ar * br - ai * bi, ar * bi + ai * br


def ssm_discretize(a_re, a_im, log_dt, b_re, b_im):
    f32 = jnp.float32
    a_re = a_re.astype(f32)
    a_im = a_im.astype(f32)
    dt = jnp.exp(log_dt.astype(f32))[:, None]
    mag = jnp.exp(a_re * dt)
    lam_re = mag * jnp.cos(a_im * dt)
    lam_im = mag * jnp.sin(a_im * dt)
    den = a_re * a_re + a_im * a_im
    nr = lam_re - 1.0
    ni = lam_im
    f_re = (nr * a_re + ni * a_im) / den
    f_im = (ni * a_re - nr * a_im) / den
    b_re = b_re.astype(f32)
    b_im = b_im.astype(f32)
    bb_re = f_re[..., None] * b_re - f_im[..., None] * b_im
    bb_im = f_re[..., None] * b_im + f_im[..., None] * b_re
    return lam_re, lam_im, bb_re, bb_im


def ssm_combine(e1, e2):
    a1r, a1i, b1r, b1i = e1
    a2r, a2i, b2r, b2i = e2
    ar, ai = cmul(a1r, a1i, a2r, a2i)
    br, bi = cmul(a2r, a2i, b1r, b1i)
    return ar, ai, br + b2r, bi + b2i


def s5_mixer(x_n, h0_re, h0_im, a_re, a_im, log_dt, b_re, b_im, c_re, c_im, d, w_glu):
    f32 = jnp.float32
    bsz, seq_len, _ = x_n.shape
    lam_re, lam_im, bb_re, bb_im = ssm_discretize(a_re, a_im, log_dt, b_re, b_im)
    c_re = c_re.astype(f32)
    c_im = c_im.astype(f32)
    d_g = d.astype(f32).reshape(N_GROUPS, SSM_GROUP)
    t_blk = SCAN_CHUNK if seq_len % SCAN_CHUNK == 0 else seq_len
    n_blk = seq_len // t_blk
    u = x_n.astype(f32).reshape(bsz, n_blk, t_blk, N_GROUPS, SSM_GROUP).swapaxes(0, 1)

    def block_step(h, u_c):
        hr, hi = h
        br = jnp.einsum('btgc,gpc->btgp', u_c, bb_re)
        bi = jnp.einsum('btgc,gpc->btgp', u_c, bb_im)
        ir, ii = cmul(lam_re, lam_im, hr, hi)
        br = br.at[:, 0].add(ir)
        bi = bi.at[:, 0].add(ii)
        ar = jnp.broadcast_to(lam_re, br.shape)
        ai = jnp.broadcast_to(lam_im, bi.shape)
        _, _, hs_re, hs_im = lax.associative_scan(ssm_combine, (ar, ai, br, bi), axis=1)
        y = (jnp.einsum('btgp,gcp->btgc', hs_re, c_re)
             - jnp.einsum('btgp,gcp->btgc', hs_im, c_im)) + d_g * u_c
        return (hs_re[:, -1], hs_im[:, -1]), y

    (hr, hi), ys = lax.scan(block_step, (h0_re.astype(f32), h0_im.astype(f32)), u)
    y = ys.swapaxes(0, 1).reshape(bsz, seq_len, D_MODEL)
    z = jax.nn.gelu(y).astype(x_n.dtype)
    g = z @ w_glu
    out = g[..., :D_MODEL] * jax.nn.sigmoid(g[..., D_MODEL:])
    return out.astype(x_n.dtype), hr, hi


def qkv_project(x_n, w_qkv, q_gain, k_gain, pos):
    bsz, seq_len, _ = x_n.shape
    qkv = x_n @ w_qkv
    nq = N_HEADS * HEAD_DIM
    nk = N_KV_HEADS * HEAD_DIM
    q = qkv[..., :nq].reshape(bsz, seq_len, N_HEADS, HEAD_DIM)
    k = qkv[..., nq:nq + nk].reshape(bsz, seq_len, N_KV_HEADS, HEAD_DIM)
    v = qkv[..., nq + nk:].reshape(bsz, seq_len, N_KV_HEADS, HEAD_DIM)
    q = rotary(rms_norm(q, q_gain), pos)
    k = rotary(rms_norm(k, k_gain), pos)
    return q, k, v


def sink_softmax(scores, mask, sinks):
    s = jnp.where(mask, scores, NEG_INF)
    sk = sinks.astype(jnp.float32).reshape(N_KV_HEADS, KV_REP, 1, 1)
    m = jnp.maximum(jnp.max(s, axis=-1, keepdims=True), sk)
    p = jnp.exp(s - m)
    denom = jnp.sum(p, axis=-1, keepdims=True) + jnp.exp(sk - m)
    return p / denom


def swa_prompt(x_n, w_qkv, q_gain, k_gain, sinks, w_o):
    f32 = jnp.float32
    bsz, seq_len, _ = x_n.shape
    pos = jnp.arange(seq_len, dtype=f32)
    q, k, v = qkv_project(x_n, w_qkv, q_gain, k_gain, pos)
    nb = seq_len // ATTN_BLOCK
    qb = q.reshape(bsz, nb, ATTN_BLOCK, N_KV_HEADS, KV_REP, HEAD_DIM).astype(f32)
    kb = k.reshape(bsz, nb, ATTN_BLOCK, N_KV_HEADS, HEAD_DIM).astype(f32)
    vb = v.reshape(bsz, nb, ATTN_BLOCK, N_KV_HEADS, HEAD_DIM).astype(f32)
    kc = jnp.concatenate([jnp.concatenate([jnp.zeros_like(kb[:, :1]), kb[:, :-1]], axis=1), kb], axis=2)
    vc = jnp.concatenate([jnp.concatenate([jnp.zeros_like(vb[:, :1]), vb[:, :-1]], axis=1), vb], axis=2)
    scores = jnp.einsum('bnqkrd,bnskd->bnkrqs', qb, kc) * ATTN_SCALE
    iq = jnp.arange(ATTN_BLOCK)[:, None]
    js = jnp.arange(2 * ATTN_BLOCK)[None, :]
    diff = ATTN_BLOCK + iq - js
    blk = jnp.arange(nb)[:, None, None]
    mask = (diff >= 0) & (diff < WINDOW) & ((blk - 1) * ATTN_BLOCK + js >= 0)
    p = sink_softmax(scores, mask[None, :, None, None], sinks)
    o = jnp.einsum('bnkrqs,bnskd->bnqkrd', p, vc).reshape(bsz, seq_len, N_HEADS * HEAD_DIM)
    out = o.astype(x_n.dtype) @ w_o
    buf = min(WINDOW, seq_len)
    return out, k[:, seq_len - buf:], v[:, seq_len - buf:]


def swa_sample(x_n, cache_k, cache_v, w_qkv, q_gain, k_gain, sinks, w_o):
    f32 = jnp.float32
    bsz, s_len, _ = x_n.shape
    w_buf = cache_k.shape[1]
    pos = PAST_LEN + jnp.arange(s_len, dtype=f32)
    q, k, v = qkv_project(x_n, w_qkv, q_gain, k_gain, pos)
    k_all = jnp.concatenate([cache_k.astype(k.dtype), k], axis=1)
    v_all = jnp.concatenate([cache_v.astype(v.dtype), v], axis=1)
    kpos = PAST_LEN - w_buf + jnp.arange(w_buf + s_len)
    qpos = PAST_LEN + jnp.arange(s_len)
    diff = qpos[:, None] - kpos[None, :]
    mask = (diff >= 0) & (diff < WINDOW)
    qg = q.reshape(bsz, s_len, N_KV_HEADS, KV_REP, HEAD_DIM).astype(f32)
    scores = jnp.einsum('bqkrd,bskd->bkrqs', qg, k_all.astype(f32)) * ATTN_SCALE
    p = sink_softmax(scores, mask, sinks)
    o = jnp.einsum('bkrqs,bskd->bqkrd', p, v_all.astype(f32)).reshape(bsz, s_len, N_HEADS * HEAD_DIM)
    out = o.astype(x_n.dtype) @ w_o
    return out, k_all[:, -w_buf:], v_all[:, -w_buf:]


def swiglu(x_n, w_gate_up, w_down):
    h = x_n @ w_gate_up
    return (jax.nn.silu(h[..., :D_FF]) * h[..., D_FF:]) @ w_down


def setup_inputs(seed: int = 0) -> dict:
    key = jax.random.key(seed)
    ks = jax.random.split(key, 24)
    f32 = jnp.float32
    nrm = lambda k, shape, s=1.0: (jax.random.normal(k, shape, f32) * s)
    w_buf = min(WINDOW, PAST_LEN)
    a_im_base = jnp.pi * jnp.arange(STATE_DIM, dtype=f32)
    return {
        "x_prompt": nrm(ks[0], (BATCH, SEQ, D_MODEL)),
        "x_sample": nrm(ks[1], (DEC_BATCH, DEC_SEQ, D_MODEL)),
        "state_ssm_re": nrm(ks[2], (N_SSM_LAYERS, DEC_BATCH, N_GROUPS, STATE_DIM), 0.3),
        "state_ssm_im": nrm(ks[3], (N_SSM_LAYERS, DEC_BATCH, N_GROUPS, STATE_DIM), 0.3),
        "cache_swa_k": nrm(ks[4], (N_SWA_LAYERS, DEC_BATCH, w_buf, N_KV_HEADS, HEAD_DIM)),
        "cache_swa_v": nrm(ks[5], (N_SWA_LAYERS, DEC_BATCH, w_buf, N_KV_HEADS, HEAD_DIM)),
        "norm_mix": 1.0 + nrm(ks[6], (DEPTH, D_MODEL), 0.02),
        "norm_ffn": 1.0 + nrm(ks[7], (DEPTH, D_MODEL), 0.02),
        "ssm_a_re": -0.5 + nrm(ks[8], (N_SSM_LAYERS, N_GROUPS, STATE_DIM), 0.01),
        "ssm_a_im": a_im_base + nrm(ks[9], (N_SSM_LAYERS, N_GROUPS, STATE_DIM), 0.01),
        "ssm_log_dt": jax.random.uniform(ks[10], (N_SSM_LAYERS, N_GROUPS), f32, math.log(1e-3), math.log(1e-1)),
        "ssm_b_re": nrm(ks[11], (N_SSM_LAYERS, N_GROUPS, STATE_DIM, SSM_GROUP), (2 * SSM_GROUP) ** -0.5),
        "ssm_b_im": nrm(ks[12], (N_SSM_LAYERS, N_GROUPS, STATE_DIM, SSM_GROUP), (2 * SSM_GROUP) ** -0.5),
        "ssm_c_re": nrm(ks[13], (N_SSM_LAYERS, N_GROUPS, SSM_GROUP, STATE_DIM), STATE_DIM ** -0.5),
        "ssm_c_im": nrm(ks[14], (N_SSM_LAYERS, N_GROUPS, SSM_GROUP, STATE_DIM), STATE_DIM ** -0.5),
        "ssm_d": nrm(ks[15], (N_SSM_LAYERS, D_MODEL)),
        "ssm_w_glu": nrm(ks[16], (N_SSM_LAYERS, D_MODEL, 2 * D_MODEL), D_MODEL ** -0.5),
        "attn_w_qkv": nrm(ks[17], (N_SWA_LAYERS, D_MODEL, (N_HEADS + 2 * N_KV_HEADS) * HEAD_DIM), D_MODEL ** -0.5),
        "attn_q_norm": 1.0 + nrm(ks[18], (N_SWA_LAYERS, HEAD_DIM), 0.02),
        "attn_k_norm": 1.0 + nrm(ks[19], (N_SWA_LAYERS, HEAD_DIM), 0.02),
        "attn_sinks": nrm(ks[20], (N_SWA_LAYERS, N_HEADS)),
        "attn_w_o": nrm(ks[21], (N_SWA_LAYERS, N_HEADS * HEAD_DIM, D_MODEL), (N_HEADS * HEAD_DIM) ** -0.5),
        "ffn_w_gate_up": nrm(ks[22], (DEPTH, D_MODEL, 2 * D_FF), D_MODEL ** -0.5),
        "ffn_w_down": nrm(ks[23], (DEPTH, D_FF, D_MODEL), D_FF ** -0.5),
    }


def reference(x_prompt, x_sample, state_ssm_re, state_ssm_im, cache_swa_k, cache_swa_v,
              norm_mix, norm_ffn, ssm_a_re, ssm_a_im, ssm_log_dt, ssm_b_re, ssm_b_im,
              ssm_c_re, ssm_c_im, ssm_d, ssm_w_glu, attn_w_qkv, attn_q_norm, attn_k_norm,
              attn_sinks, attn_w_o, ffn_w_gate_up, ffn_w_down):
    yp = x_prompt
    ys = x_sample
    p_re, p_im, p_k, p_v = [], [], [], []
    s_re, s_im, s_k, s_v = [], [], [], []
    h0 = jnp.zeros((x_prompt.shape[0], N_GROUPS, STATE_DIM), jnp.float32)
    for i in range(DEPTH):
        j = i // N_MIXERS
        xp_n = rms_norm(yp, norm_mix[i])
        xs_n = rms_norm(ys, norm_mix[i])
        if i % N_MIXERS == 0:
            ssm_w = (ssm_a_re[j], ssm_a_im[j], ssm_log_dt[j], ssm_b_re[j], ssm_b_im[j],
                     ssm_c_re[j], ssm_c_im[j], ssm_d[j], ssm_w_glu[j])
            op, hr, hi = s5_mixer(xp_n, h0, h0, *ssm_w)
            osm, sr, si = s5_mixer(xs_n, state_ssm_re[j], state_ssm_im[j], *ssm_w)
            p_re.append(hr)
            p_im.append(hi)
            s_re.append(sr)
            s_im.append(si)
        else:
            attn_w = (attn_w_qkv[j], attn_q_norm[j], attn_k_norm[j], attn_sinks[j], attn_w_o[j])
            op, kp, vp = swa_prompt(xp_n, *attn_w)
            osm, ks_, vs_ = swa_sample(xs_n, cache_swa_k[j], cache_swa_v[j], *attn_w)
            p_k.append(kp)
            p_v.append(vp)
            s_k.append(ks_)
            s_v.append(vs_)
        yp = yp + op
        ys = ys + osm
        yp = yp + swiglu(rms_norm(yp, norm_ffn[i]), ffn_w_gate_up[i], ffn_w_down[i])
        ys = ys + swiglu(rms_norm(ys, norm_ffn[i]), ffn_w_gate_up[i], ffn_w_down[i])
    p_state_re = jnp.stack(p_re)
    p_state_im = jnp.stack(p_im)
    p_cache_k = jnp.stack(p_k)
    p_cache_v = jnp.stack(p_v)
    s_state_re = jnp.stack(s_re)
    s_state_im = jnp.stack(s_im)
    s_cache_k = jnp.stack(s_k)
    s_cache_v = jnp.stack(s_v)
    return (yp, ys, p_state_re, p_state_im, p_cache_k, p_cache_v, s_state_re, s_state_im, s_cache_k, s_cache_v)
```

```python
import functools

import jax
import jax.numpy as jnp
import numpy as np
from jax import lax
from jax.experimental import pallas as pl
from jax.experimental.pallas import tpu as pltpu

F32 = jnp.float32
BF16 = jnp.bfloat16

D_MODEL = 1024
DEPTH = 4
PAST_LEN = 8192
SSM_GROUP = 16
N_GROUPS = D_MODEL // SSM_GROUP
STATE_DIM = 64
N_STATE = N_GROUPS * STATE_DIM
HEAD_DIM = 64
N_HEADS = D_MODEL // HEAD_DIM
N_KV_HEADS = 4
KV_REP = N_HEADS // N_KV_HEADS
D_KV = N_KV_HEADS * HEAD_DIM
D_QK = D_MODEL + D_KV
D_QKV = D_MODEL + 2 * D_KV
WINDOW = 128
ROT_DIM = HEAD_DIM // 4
ROPE_THETA = 500000.0
ATTN_SCALE = HEAD_DIM ** -0.5
D_FF = 2816
NORM_EPS = 1e-6
NEG_INF = -1e30

LANES = 128
SUBLANES = 8
MXU_DIM = 256
N_SLABS = N_STATE // LANES
VMEM_LIMIT = 56 * 1024 * 1024


def _rms(x, g):
    ms = jnp.mean(x * x, axis=-1, keepdims=True)
    return x * lax.rsqrt(ms + NORM_EPS) * g


def _const_spec(shape):
    nd = len(shape)
    return pl.BlockSpec(shape, lambda *_: (0,) * nd, pipeline_mode=pl.Buffered(1))


FF_CHUNK = 256


def _ffn_body(x_ref, g_ref, wgu_ref, wd_ref, o_ref, a_scr):
    x = x_ref[...]
    xn = _rms(x, g_ref[...]).astype(BF16)
    for c in range(D_FF // FF_CHUNK):
        lo = c * FF_CHUNK
        gate = jnp.dot(xn, wgu_ref[:, lo:lo + FF_CHUNK], preferred_element_type=F32)
        up = jnp.dot(xn, wgu_ref[:, D_FF + lo:D_FF + lo + FF_CHUNK], preferred_element_type=F32)
        a_scr[:, lo:lo + FF_CHUNK] = (gate * jax.nn.sigmoid(gate) * up).astype(BF16)
    o_ref[...] = x + jnp.dot(a_scr[...], wd_ref[...], preferred_element_type=F32)


def _ffn_call(x, g, wgu, wd, *, tm):
    m = x.shape[0]
    return pl.pallas_call(
        _ffn_body,
        grid=(m // tm,),
        in_specs=[
            pl.BlockSpec((tm, D_MODEL), lambda i: (i, 0)),
            _const_spec((1, D_MODEL)),
            _const_spec((D_MODEL, 2 * D_FF)),
            _const_spec((D_FF, D_MODEL)),
        ],
        out_specs=pl.BlockSpec((tm, D_MODEL), lambda i: (i, 0)),
        out_shape=jax.ShapeDtypeStruct((m, D_MODEL), F32),
        scratch_shapes=[pltpu.VMEM((tm, D_FF), BF16)],
        compiler_params=pltpu.CompilerParams(
            dimension_semantics=("arbitrary",), vmem_limit_bytes=VMEM_LIMIT),
        name="ffn",
    )(x, g, wgu, wd)


SCAN_SLABS = 8


def _ssm_body(x_ref, g_ref, wb_ref, wc_ref, lam_ref, d_ref, wglu_ref, h0r_ref, h0i_ref,
              o_ref, hr_ref, hi_ref, xp_scr, bu_scr, *, bb, tc):
    pitch = tc + 1

    @pl.when(pl.program_id(1) == 0)
    def _():
        hr_ref[...] = h0r_ref[...]
        hi_ref[...] = h0i_ref[...]

    g = g_ref[...]
    for b in range(bb):
        xp_scr[b * pitch:b * pitch + tc, :] = _rms(x_ref[b], g)
        xp_scr[b * pitch + tc:b * pitch + tc + 1, :] = jnp.zeros((1, D_MODEL), F32)

    u = xp_scr[...]
    ub = u.astype(BF16)
    half = N_SLABS // 4
    for kt in range(4):
        res = jnp.dot(ub[:, kt * MXU_DIM:(kt + 1) * MXU_DIM], wb_ref[kt], preferred_element_type=F32)
        for j in range(half):
            bu_scr[kt * half + j] = res[:, j * LANES:(j + 1) * LANES]
            bu_scr[N_SLABS + kt * half + j] = res[:, (half + j) * LANES:(half + j + 1) * LANES]

    for rg in range(bb // SUBLANES):
        rows = slice(rg * SUBLANES, (rg + 1) * SUBLANES)
        for sg in range(N_SLABS // SCAN_SLABS):
            slabs = [sg * SCAN_SLABS + i for i in range(SCAN_SLABS)]
            lr = [lam_ref[s] for s in slabs]
            li = [lam_ref[N_SLABS + s] for s in slabs]
            hr0 = tuple(hr_ref[rows, s * LANES:(s + 1) * LANES] for s in slabs)
            hi0 = tuple(hi_ref[rows, s * LANES:(s + 1) * LANES] for s in slabs)

            def step(t, carry, slabs=slabs, lr=lr, li=li, rg=rg):
                hr, hi = carry
                idx = pl.ds(rg * SUBLANES * pitch + t, SUBLANES, stride=pitch)
                nhr, nhi = [], []
                for i, s in enumerate(slabs):
                    br = bu_scr[s, idx, :]
                    bi = bu_scr[N_SLABS + s, idx, :]
                    nr = (lr[i] * hr[i] - li[i] * hi[i]) + br
                    ni = (lr[i] * hi[i] + li[i] * hr[i]) + bi
                    bu_scr[s, idx, :] = nr
                    bu_scr[N_SLABS + s, idx, :] = ni
                    nhr.append(nr)
                    nhi.append(ni)
                return tuple(nhr), tuple(nhi)

            hrn, hin = lax.fori_loop(0, tc, step, (hr0, hi0), unroll=2)
            for i, s in enumerate(slabs):
                hr_ref[rows, s * LANES:(s + 1) * LANES] = hrn[i]
                hi_ref[rows, s * LANES:(s + 1) * LANES] = hin[i]

    ys = []
    for nt in range(4):
        hcat = jnp.concatenate(
            [bu_scr[nt * half + j] for j in range(half)]
            + [bu_scr[N_SLABS + nt * half + j] for j in range(half)], axis=-1).astype(BF16)
        ys.append(jnp.dot(hcat, wc_ref[nt], preferred_element_type=F32))
    y = jnp.concatenate(ys, axis=-1) + d_ref[...] * u
    z = jax.nn.gelu(y).astype(BF16)
    gl = jnp.dot(z, wglu_ref[...], preferred_element_type=F32)
    xp_scr[...] = gl[:, :D_MODEL] * jax.nn.sigmoid(gl[:, D_MODEL:])
    for b in range(bb):
        o_ref[b] = x_ref[b] + xp_scr[b * pitch:b * pitch + tc, :]


def _ssm_call(x, g, wb, wc, lam, d, wglu, h0r, h0i, *, bb, tc):
    bsz, seq, _ = x.shape
    mp = bb * (tc + 1)
    body = functools.partial(_ssm_body, bb=bb, tc=tc)
    return pl.pallas_call(
        body,
        grid=(bsz // bb, seq // tc),
        in_specs=[
            pl.BlockSpec((bb, tc, D_MODEL), lambda i, j: (i, j, 0)),
            _const_spec((1, D_MODEL)),
            _const_spec(wb.shape),
            _const_spec(wc.shape),
            _const_spec(lam.shape),
            _const_spec((1, D_MODEL)),
            _const_spec(wglu.shape),
            pl.BlockSpec((bb, N_STATE), lambda i, j: (i, 0)),
            pl.BlockSpec((bb, N_STATE), lambda i, j: (i, 0)),
        ],
        out_specs=[
            pl.BlockSpec((bb, tc, D_MODEL), lambda i, j: (i, j, 0)),
            pl.BlockSpec((bb, N_STATE), lambda i, j: (i, 0)),
            pl.BlockSpec((bb, N_STATE), lambda i, j: (i, 0)),
        ],
        out_shape=[
            jax.ShapeDtypeStruct(x.shape, F32),
            jax.ShapeDtypeStruct((bsz, N_STATE), F32),
            jax.ShapeDtypeStruct((bsz, N_STATE), F32),
        ],
        scratch_shapes=[
            pltpu.VMEM((mp, D_MODEL), F32),
            pltpu.VMEM((2 * N_SLABS, mp, LANES), F32),
        ],
        compiler_params=pltpu.CompilerParams(
            dimension_semantics=("arbitrary", "arbitrary"), vmem_limit_bytes=VMEM_LIMIT),
        name="ssm",
    )(x, g, wb, wc, lam, d, wglu, h0r, h0i)


def _qk_norm_rope(qk, ones_bd, gain, cos, sina, sinb):
    n_cols = qk.shape[1] // LANES
    sq = qk * qk
    hi = sq.astype(BF16)
    lo = (sq - hi.astype(F32)).astype(BF16)
    outs = []
    for c in range(n_cols // 2):
        sl = slice(c * MXU_DIM, (c + 1) * MXU_DIM)
        ss = (jnp.dot(hi[:, sl], ones_bd, preferred_element_type=F32)
              + jnp.dot(lo[:, sl], ones_bd, preferred_element_type=F32))
        outs.append(qk[:, sl] * lax.rsqrt(ss * (1.0 / HEAD_DIM) + NORM_EPS) * gain[:, sl])
    qn = jnp.concatenate(outs, axis=-1)
    cols = []
    for c in range(n_cols):
        xc = qn[:, c * LANES:(c + 1) * LANES]
        cols.append(xc * cos + pltpu.roll(xc, ROT_DIM // 2, 1) * sina
                    + pltpu.roll(xc, LANES - ROT_DIM // 2, 1) * sinb)
    return jnp.concatenate(cols, axis=-1)


def _lane_halves(x):
    lane = lax.broadcasted_iota(jnp.int32, x.shape, 1)
    low = lane < HEAD_DIM
    return jnp.where(low, x, 0.0), jnp.where(low, 0.0, x)


TQ = 256


def _attn_prompt_body(x_ref, g_ref, wqkv_ref, ones_ref, gain_ref, cos_ref, sina_ref, sinb_ref,
                      bias_ref, sink_ref, wo_ref, o_ref, kc_ref, vc_ref,
                      q_scr, ka_scr, kb_scr, va_scr, vb_scr, o_scr):
    i = pl.program_id(1)
    n_keys = WINDOW + TQ

    @pl.when(i == 0)
    def _():
        for scr in (ka_scr, kb_scr, va_scr, vb_scr):
            scr[:, 0:WINDOW, :] = jnp.zeros((N_KV_HEADS, WINDOW, LANES), BF16)

    x = x_ref[0]
    xn = _rms(x, g_ref[...]).astype(BF16)
    qkv = jnp.dot(xn, wqkv_ref[...], preferred_element_type=F32)
    qk = _qk_norm_rope(qkv[:, :D_QK], ones_ref[...], gain_ref[...],
                       cos_ref[...], sina_ref[...], sinb_ref[...])
    q_scr[...] = qk[:, :D_MODEL].astype(BF16)
    k = qk[:, D_MODEL:]
    v = qkv[:, D_QK:]
    kc_ref[0] = k[TQ - WINDOW:, :]
    vc_ref[0] = v[TQ - WINDOW:, :]

    for h in range(N_KV_HEADS):
        col = slice((h // 2) * LANES, (h // 2 + 1) * LANES)
        for src, dst_a, dst_b in ((k, ka_scr, kb_scr), (v, va_scr, vb_scr)):
            lo_part, hi_part = _lane_halves(src[:, col])
            if h % 2 == 0:
                a, b = lo_part, pltpu.roll(lo_part, HEAD_DIM, 1)
            else:
                a, b = pltpu.roll(hi_part, HEAD_DIM, 1), hi_part
            dst_a[h, WINDOW:n_keys, :] = a.astype(BF16)
            dst_b[h, WINDOW:n_keys, :] = b.astype(BF16)

    bias_mid = jnp.concatenate([bias_ref[...], bias_ref[...]], axis=0)
    col = lax.broadcasted_iota(jnp.int32, bias_mid.shape, 1)
    bias_first = jnp.where(jnp.logical_and(i == 0, col < WINDOW), NEG_INF, bias_mid)
    row = lax.broadcasted_iota(jnp.int32, (2 * WINDOW, 1), 0)
    nt_dims = (((1,), (1,)), ((), ()))
    for j in range(TQ // WINDOW):
        bias = bias_first if j == 0 else bias_mid
        qrows = slice(j * WINDOW, (j + 1) * WINDOW)
        krows = slice(j * WINDOW, j * WINDOW + 2 * WINDOW)
        for h in range(N_KV_HEADS):
            qs = jnp.concatenate([q_scr[qrows, (2 * h) * LANES:(2 * h + 1) * LANES],
                                  q_scr[qrows, (2 * h + 1) * LANES:(2 * h + 2) * LANES]], axis=0)
            o_pair = None
            for par, k_scr, v_scr in ((0, ka_scr, va_scr), (1, kb_scr, vb_scr)):
                s = lax.dot_general(qs, k_scr[h, krows, :], nt_dims, preferred_element_type=F32) + bias
                sink = jnp.where(row < WINDOW, sink_ref[4 * h + par], sink_ref[4 * h + 2 + par])
                m = jnp.maximum(jnp.max(s, axis=-1, keepdims=True), sink)
                p = jnp.exp(s - m)
                denom = jnp.sum(p, axis=-1, keepdims=True) + jnp.exp(sink - m)
                o = jnp.dot(p.astype(BF16), v_scr[h, krows, :], preferred_element_type=F32)
                o = o * (1.0 / denom)
                o_pair = o if o_pair is None else o_pair + o
            o_scr[qrows, (2 * h) * LANES:(2 * h + 1) * LANES] = o_pair[:WINDOW].astype(BF16)
            o_scr[qrows, (2 * h + 1) * LANES:(2 * h + 2) * LANES] = o_pair[WINDOW:].astype(BF16)

    o_ref[0] = x + jnp.dot(o_scr[...], wo_ref[...], preferred_element_type=F32)
    for scr in (ka_scr, kb_scr, va_scr, vb_scr):
        scr[:, 0:WINDOW, :] = scr[:, TQ:n_keys, :]


def _attn_prompt_call(x, g, wqkv, ones_bd, gain, cos, sina, sinb, bias, sinks, wo):
    bsz, seq, _ = x.shape
    n_keys = WINDOW + TQ
    kv_scr = pltpu.VMEM((N_KV_HEADS, n_keys, LANES), BF16)
    return pl.pallas_call(
        _attn_prompt_body,
        grid=(bsz, seq // TQ),
        in_specs=[
            pl.BlockSpec((1, TQ, D_MODEL), lambda b, i: (b, i, 0)),
            _const_spec((1, D_MODEL)),
            _const_spec((D_MODEL, D_QKV)),
            _const_spec((MXU_DIM, MXU_DIM)),
            _const_spec((1, D_QK)),
            pl.BlockSpec((TQ, LANES), lambda b, i: (i, 0)),
            pl.BlockSpec((TQ, LANES), lambda b, i: (i, 0)),
            pl.BlockSpec((TQ, LANES), lambda b, i: (i, 0)),
            _const_spec((WINDOW, 2 * WINDOW)),
            pl.BlockSpec(memory_space=pltpu.SMEM),
            _const_spec((D_MODEL, D_MODEL)),
        ],
        out_specs=[
            pl.BlockSpec((1, TQ, D_MODEL), lambda b, i: (b, i, 0)),
            pl.BlockSpec((1, WINDOW, D_KV), lambda b, i: (b, 0, 0)),
            pl.BlockSpec((1, WINDOW, D_KV), lambda b, i: (b, 0, 0)),
        ],
        out_shape=[
            jax.ShapeDtypeStruct(x.shape, F32),
            jax.ShapeDtypeStruct((bsz, WINDOW, D_KV), F32),
            jax.ShapeDtypeStruct((bsz, WINDOW, D_KV), F32),
        ],
        scratch_shapes=[pltpu.VMEM((TQ, D_MODEL), BF16), kv_scr, kv_scr, kv_scr, kv_scr,
                        pltpu.VMEM((TQ, D_MODEL), BF16)],
        compiler_params=pltpu.CompilerParams(
            dimension_semantics=("arbitrary", "arbitrary"), vmem_limit_bytes=VMEM_LIMIT),
        name="attn_prompt",
    )(x, g, wqkv, ones_bd, gain, cos, sina, sinb, bias, sinks, wo)


SEQ_BLOCK = 16
N_KEYS_PAD = 2 * WINDOW


def _attn_sample_body(x_ref, ck_ref, cv_ref, g_ref, wqkv_ref, ones_ref, gain_ref, cos_ref, sina_ref,
                      sinb_ref, bias_ref, sink_ref, wo_ref, o_ref, kc_ref, vc_ref,
                      kf_scr, vf_scr, oall_scr, *, s_len):
    rows = SEQ_BLOCK * s_len
    w_keep = WINDOW - s_len

    @pl.when(pl.program_id(0) == 0)
    def _():
        pad = (SEQ_BLOCK, N_KEYS_PAD - WINDOW, D_KV)
        kf_scr[:, WINDOW:, :] = jnp.zeros(pad, BF16)
        vf_scr[:, WINDOW:, :] = jnp.zeros(pad, BF16)

    x = x_ref[...].reshape(rows, D_MODEL)
    xn = _rms(x, g_ref[...]).astype(BF16)
    qkv = jnp.dot(xn, wqkv_ref[...], preferred_element_type=F32)
    qk = _qk_norm_rope(qkv[:, :D_QK], ones_ref[...], gain_ref[...],
                       cos_ref[...], sina_ref[...], sinb_ref[...])
    q = qk[:, :D_MODEL]
    k_new = qk[:, D_MODEL:].reshape(SEQ_BLOCK, s_len, D_KV)
    v_new = qkv[:, D_QK:].reshape(SEQ_BLOCK, s_len, D_KV)

    ck = ck_ref[...]
    cv = cv_ref[...]
    kc_ref[:, 0:w_keep, :] = ck[:, s_len:, :]
    vc_ref[:, 0:w_keep, :] = cv[:, s_len:, :]
    kc_ref[:, w_keep:, :] = k_new
    vc_ref[:, w_keep:, :] = v_new
    kf_scr[:, 0:WINDOW, :] = ck.astype(BF16)
    vf_scr[:, 0:WINDOW, :] = cv.astype(BF16)
    tile = 2 * SUBLANES
    zpad = jnp.zeros((SEQ_BLOCK, tile - s_len, D_KV), F32)
    kf_scr[:, WINDOW:WINDOW + tile, :] = jnp.concatenate([k_new, zpad], axis=1).astype(BF16)
    vf_scr[:, WINDOW:WINDOW + tile, :] = jnp.concatenate([v_new, zpad], axis=1).astype(BF16)

    bias = bias_ref[...]
    sink = sink_ref[...]
    nt_dims = (((1,), (1,)), ((), ()))
    zero_col = jnp.zeros((s_len, LANES), F32)
    for s in range(SEQ_BLOCK):
        qs = q[s * s_len:(s + 1) * s_len, :]
        tiles = []
        for h in range(N_KV_HEADS):
            for r in range(KV_REP):
                head = KV_REP * h + r
                src = qs[:, (head // 2) * LANES:(head // 2 + 1) * LANES]
                lo_part, hi_part = _lane_halves(src)
                part = lo_part if head % 2 == 0 else hi_part
                if head % 2 != h % 2:
                    part = pltpu.roll(part, HEAD_DIM, 1)
                tiles.append(jnp.concatenate([part, zero_col] if h < 2 else [zero_col, part], axis=1))
        wt = jnp.concatenate(tiles, axis=0).astype(BF16)
        sc = lax.dot_general(wt, kf_scr[s], nt_dims, preferred_element_type=F32) + bias
        m = jnp.maximum(jnp.max(sc, axis=-1, keepdims=True), sink)
        p = jnp.exp(sc - m)
        denom = jnp.sum(p, axis=-1, keepdims=True) + jnp.exp(sink - m)
        o = jnp.dot(p.astype(BF16), vf_scr[s], preferred_element_type=F32) * (1.0 / denom)
        cols = []
        for c in range(N_HEADS // 2):
            pieces = []
            for head in (2 * c, 2 * c + 1):
                h, r = divmod(head, KV_REP)
                blk = o[(KV_REP * h + r) * s_len:(KV_REP * h + r + 1) * s_len,
                        (h // 2) * LANES:(h // 2 + 1) * LANES]
                lo_part, hi_part = _lane_halves(blk)
                part = lo_part if h % 2 == 0 else hi_part
                if h % 2 != head % 2:
                    part = pltpu.roll(part, HEAD_DIM, 1)
                pieces.append(part)
            cols.append(pieces[0] + pieces[1])
        oall_scr[s * s_len:(s + 1) * s_len, :] = jnp.concatenate(cols, axis=-1)

    out = jnp.dot(oall_scr[...].astype(BF16), wo_ref[...], preferred_element_type=F32)
    o_ref[...] = (x + out).reshape(SEQ_BLOCK, s_len, D_MODEL)


def _attn_sample_call(x, ck, cv, g, wqkv, ones_bd, gain, cos, sina, sinb, bias, sink_rows, wo):
    bsz, s_len, _ = x.shape
    rows = SEQ_BLOCK * s_len
    body = functools.partial(_attn_sample_body, s_len=s_len)
    seq_spec = lambda shape: pl.BlockSpec(shape, lambda i: (i, 0, 0))
    return pl.pallas_call(
        body,
        grid=(bsz // SEQ_BLOCK,),
        in_specs=[
            seq_spec((SEQ_BLOCK, s_len, D_MODEL)),
            seq_spec((SEQ_BLOCK, WINDOW, D_KV)),
            seq_spec((SEQ_BLOCK, WINDOW, D_KV)),
            _const_spec((1, D_MODEL)),
            _const_spec((D_MODEL, D_QKV)),
            _const_spec((MXU_DIM, MXU_DIM)),
            _const_spec((1, D_QK)),
            _const_spec((rows, LANES)),
            _const_spec((rows, LANES)),
            _const_spec((rows, LANES)),
            _const_spec((N_HEADS * s_len, N_KEYS_PAD)),
            _const_spec((N_HEADS * s_len, 1)),
            _const_spec((D_MODEL, D_MODEL)),
        ],
        out_specs=[
            seq_spec((SEQ_BLOCK, s_len, D_MODEL)),
            seq_spec((SEQ_BLOCK, WINDOW, D_KV)),
            seq_spec((SEQ_BLOCK, WINDOW, D_KV)),
        ],
        out_shape=[
            jax.ShapeDtypeStruct(x.shape, F32),
            jax.ShapeDtypeStruct(ck.shape, F32),
            jax.ShapeDtypeStruct(cv.shape, F32),
        ],
        scratch_shapes=[
            pltpu.VMEM((SEQ_BLOCK, N_KEYS_PAD, D_KV), BF16),
            pltpu.VMEM((SEQ_BLOCK, N_KEYS_PAD, D_KV), BF16),
            pltpu.VMEM((rows, D_MODEL), F32),
        ],
        compiler_params=pltpu.CompilerParams(
            dimension_semantics=("arbitrary",), vmem_limit_bytes=VMEM_LIMIT),
        name="attn_sample",
    )(x, ck, cv, g, wqkv, ones_bd, gain, cos, sina, sinb, bias, sink_rows, wo)


def _ssm_weights(a_re, a_im, log_dt, b_re, b_im, c_re, c_im):
    dt = jnp.exp(log_dt)[:, None]
    mag = jnp.exp(a_re * dt)
    lam_re = mag * jnp.cos(a_im * dt)
    lam_im = mag * jnp.sin(a_im * dt)
    den = a_re * a_re + a_im * a_im
    nr = lam_re - 1.0
    ni = lam_im
    f_re = (nr * a_re + ni * a_im) / den
    f_im = (ni * a_re - nr * a_im) / den
    bb_re = f_re[..., None] * b_re - f_im[..., None] * b_im
    bb_im = f_re[..., None] * b_im + f_im[..., None] * b_re
    gpt = MXU_DIM // SSM_GROUP
    eye = jnp.eye(gpt, dtype=F32)

    def b_tiles(bb):
        t = bb.reshape(4, gpt, STATE_DIM, SSM_GROUP)
        return jnp.einsum('kgpc,gh->kgchp', t, eye).reshape(4, MXU_DIM, gpt * STATE_DIM)

    def c_tiles(cc):
        t = cc.reshape(4, gpt, SSM_GROUP, STATE_DIM)
        return jnp.einsum('ngcp,hg->nhpgc', t, eye).reshape(4, gpt * STATE_DIM, MXU_DIM)

    wb = jnp.concatenate([b_tiles(bb_re), b_tiles(bb_im)], axis=-1).astype(BF16)
    wc = jnp.concatenate([c_tiles(c_re), -c_tiles(c_im)], axis=1).astype(BF16)
    lam = jnp.concatenate([lam_re.reshape(N_SLABS, 1, LANES), lam_im.reshape(N_SLABS, 1, LANES)], axis=0)
    lam = jnp.broadcast_to(lam, (2 * N_SLABS, SUBLANES, LANES))
    return wb, wc, lam


def _rope_tables(pos):
    half = ROT_DIM // 2
    inv_freq = ROPE_THETA ** (-jnp.arange(half, dtype=F32) * 2.0 / ROT_DIM)
    ang = pos[:, None] * inv_freq[None, :]
    cos, sin = jnp.cos(ang), jnp.sin(ang)
    n = pos.shape[0]
    ones = jnp.ones((n, HEAD_DIM - ROT_DIM), F32)
    zeros = jnp.zeros((n, HEAD_DIM - ROT_DIM), F32)
    zh = jnp.zeros((n, half), F32)
    cos_h = jnp.concatenate([cos, cos, ones], axis=1)
    sina_h = jnp.concatenate([zh, sin, zeros], axis=1)
    sinb_h = jnp.concatenate([-sin, zh, zeros], axis=1)
    return tuple(jnp.concatenate([t, t], axis=1) for t in (cos_h, sina_h, sinb_h))


def _ones_block_diag():
    idx = np.arange(MXU_DIM) // HEAD_DIM
    return jnp.asarray(idx[:, None] == idx[None, :], dtype=BF16)


def _prompt_bias():
    iq = np.arange(WINDOW)[:, None]
    js = np.arange(2 * WINDOW)[None, :]
    diff = WINDOW + iq - js
    return jnp.asarray(np.where((diff >= 0) & (diff < WINDOW), 0.0, NEG_INF), dtype=F32)


def _sample_bias(s_len):
    t = (np.arange(N_HEADS * s_len) % s_len)[:, None]
    key = np.arange(N_KEYS_PAD)[None, :]
    diff = np.where(key < WINDOW, WINDOW + t - key, t - (key - WINDOW))
    ok = (diff >= 0) & (diff < WINDOW) & (key < WINDOW + s_len)
    return jnp.asarray(np.where(ok, 0.0, NEG_INF), dtype=F32)


SSM_PROMPT_TC = 32
SSM_SAMPLE_BB = 32
FFN_TM = 512


def kernel(x_prompt, x_sample, state_ssm_re, state_ssm_im, cache_swa_k, cache_swa_v, norm_mix, norm_ffn, ssm_a_re, ssm_a_im, ssm_log_dt, ssm_b_re, ssm_b_im, ssm_c_re, ssm_c_im, ssm_d, ssm_w_glu, attn_w_qkv, attn_q_norm, attn_k_norm, attn_sinks, attn_w_o, ffn_w_gate_up, ffn_w_down):
    bsz, seq, _ = x_prompt.shape
    dbsz, dseq, _ = x_sample.shape
    yp, ys = x_prompt, x_sample
    p_re, p_im, p_k, p_v, s_re, s_im, s_k, s_v = ([] for _ in range(8))

    ones_bd = _ones_block_diag()
    p_tabs = _rope_tables(jnp.arange(seq, dtype=F32))
    s_pos = PAST_LEN + jnp.arange(dseq, dtype=F32)
    s_tabs = tuple(jnp.tile(t, (SEQ_BLOCK, 1)) for t in _rope_tables(s_pos))
    p_bias = _prompt_bias()
    s_bias = _sample_bias(dseq)
    h0 = jnp.zeros((bsz, N_STATE), F32)

    for i in range(DEPTH):
        j = i // 2
        g_mix = norm_mix[i].reshape(1, D_MODEL)
        if i % 2 == 0:
            wb, wc, lam = _ssm_weights(ssm_a_re[j], ssm_a_im[j], ssm_log_dt[j], ssm_b_re[j], ssm_b_im[j],
                                       ssm_c_re[j], ssm_c_im[j])
            d = ssm_d[j].reshape(1, D_MODEL)
            wglu = ssm_w_glu[j].astype(BF16)
            yp, hr, hi = _ssm_call(yp, g_mix, wb, wc, lam, d, wglu, h0, h0, bb=bsz, tc=SSM_PROMPT_TC)
            ys, sr, si = _ssm_call(ys, g_mix, wb, wc, lam, d, wglu,
                                   state_ssm_re[j].reshape(dbsz, N_STATE), state_ssm_im[j].reshape(dbsz, N_STATE),
                                   bb=SSM_SAMPLE_BB, tc=dseq)
            shp = (N_GROUPS, STATE_DIM)
            p_re.append(hr.reshape(bsz, *shp))
            p_im.append(hi.reshape(bsz, *shp))
            s_re.append(sr.reshape(dbsz, *shp))
            s_im.append(si.reshape(dbsz, *shp))
        else:
            wqkv = attn_w_qkv[j].astype(BF16)
            wo = attn_w_o[j].astype(BF16)
            gain = jnp.concatenate([jnp.tile(attn_q_norm[j] * ATTN_SCALE, N_HEADS),
                                    jnp.tile(attn_k_norm[j], N_KV_HEADS)]).reshape(1, D_QK)
            sinks = attn_sinks[j]
            yp, kp, vp = _attn_prompt_call(yp, g_mix, wqkv, ones_bd, gain, *p_tabs, p_bias, sinks, wo)
            sink_rows = jnp.repeat(sinks, dseq).reshape(N_HEADS * dseq, 1)
            ys, ks_, vs_ = _attn_sample_call(
                ys, cache_swa_k[j].reshape(dbsz, WINDOW, D_KV), cache_swa_v[j].reshape(dbsz, WINDOW, D_KV),
                g_mix, wqkv, ones_bd, gain, *s_tabs, s_bias, sink_rows, wo)
            kv_shape = (WINDOW, N_KV_HEADS, HEAD_DIM)
            p_k.append(kp.reshape(bsz, *kv_shape))
            p_v.append(vp.reshape(bsz, *kv_shape))
            s_k.append(ks_.reshape(dbsz, *kv_shape))
            s_v.append(vs_.reshape(dbsz, *kv_shape))
        g_ffn = norm_ffn[i].reshape(1, D_MODEL)
        wgu = ffn_w_gate_up[i].astype(BF16)
        wd = ffn_w_down[i].astype(BF16)
        yp = _ffn_call(yp.reshape(bsz * seq, D_MODEL), g_ffn, wgu, wd, tm=FFN_TM).reshape(bsz, seq, D_MODEL)
        ys = _ffn_call(ys.reshape(dbsz * dseq, D_MODEL), g_ffn, wgu, wd, tm=FFN_TM).reshape(dbsz, dseq, D_MODEL)

    return (yp, ys, jnp.stack(p_re), jnp.stack(p_im), jnp.stack(p_k), jnp.stack(p_v),
            jnp.stack(s_re), jnp.stack(s_im), jnp.stack(s_k), jnp.stack(s_v))
```

```python
import functools

import jax
import jax.numpy as jnp
import numpy as np
from jax import lax
from jax.experimental import pallas as pl
from jax.experimental.pallas import tpu as pltpu

F32 = jnp.float32
BF16 = jnp.bfloat16

D_MODEL = 1024
DEPTH = 4
PAST_LEN = 8192
SSM_GROUP = 16
N_GROUPS = D_MODEL // SSM_GROUP
STATE_DIM = 64
N_STATE = N_GROUPS * STATE_DIM
HEAD_DIM = 64
N_HEADS = D_MODEL // HEAD_DIM
N_KV_HEADS = 4
KV_REP = N_HEADS // N_KV_HEADS
D_KV = N_KV_HEADS * HEAD_DIM
D_QK = D_MODEL + D_KV
D_QKV = D_MODEL + 2 * D_KV
WINDOW = 128
ROT_DIM = HEAD_DIM // 4
ROPE_THETA = 500000.0
ATTN_SCALE = HEAD_DIM ** -0.5
D_FF = 2816
NORM_EPS = 1e-6
NEG_INF = -1e30

LANES = 128
SUBLANES = 8
MXU_DIM = 256
N_SLABS = N_STATE // LANES
VMEM_LIMIT = 56 * 1024 * 1024


def _rms(x, g):
    ms = jnp.mean(x * x, axis=-1, keepdims=True)
    return x * lax.rsqrt(ms + NORM_EPS) * g


def _const_spec(shape):
    nd = len(shape)
    return pl.BlockSpec(shape, lambda *_: (0,) * nd, pipeline_mode=pl.Buffered(1))


def _layer_spec(stack, layer):
    nd = stack.ndim - 1
    return pl.BlockSpec((None,) + stack.shape[1:], lambda *_: (layer,) + (0,) * nd,
                        pipeline_mode=pl.Buffered(1))


FF_CHUNK = 256


def _ffn_body(x_ref, g_ref, wgu_ref, wd_ref, o_ref, a_scr):
    x = x_ref[...]
    xn = _rms(x, g_ref[...]).astype(BF16)
    for c in range(D_FF // FF_CHUNK):
        lo = c * FF_CHUNK
        gate = jnp.dot(xn, wgu_ref[:, lo:lo + FF_CHUNK], preferred_element_type=F32)
        up = jnp.dot(xn, wgu_ref[:, D_FF + lo:D_FF + lo + FF_CHUNK], preferred_element_type=F32)
        a_scr[:, lo:lo + FF_CHUNK] = (gate * jax.nn.sigmoid(gate) * up).astype(BF16)
    o_ref[...] = x + jnp.dot(a_scr[...], wd_ref[...], preferred_element_type=F32)


def _ffn_call(x, g, wgu, wd, *, layer, tm):
    m = x.shape[0]
    return pl.pallas_call(
        _ffn_body,
        grid=(m // tm,),
        in_specs=[
            pl.BlockSpec((tm, D_MODEL), lambda i: (i, 0)),
            _const_spec((1, D_MODEL)),
            _layer_spec(wgu, layer),
            _layer_spec(wd, layer),
        ],
        out_specs=pl.BlockSpec((tm, D_MODEL), lambda i: (i, 0)),
        out_shape=jax.ShapeDtypeStruct((m, D_MODEL), F32),
        scratch_shapes=[pltpu.VMEM((tm, D_FF), BF16)],
        compiler_params=pltpu.CompilerParams(
            dimension_semantics=("arbitrary",), vmem_limit_bytes=VMEM_LIMIT),
        name="ffn",
    )(x, g, wgu, wd)


N_GTILES = D_MODEL // MXU_DIM
SLABS_PER_TILE = N_SLABS // N_GTILES


def _ssm_body(x_ref, g_ref, wb_ref, wc_ref, lam_ref, d_ref, wglu_ref, h0r_ref, h0i_ref,
              o_ref, hr_ref, hi_ref, xp_scr, *bu_scrs, bb, tc):
    pitch = tc + 1
    spt = SLABS_PER_TILE

    @pl.when(pl.program_id(1) == 0)
    def _():
        hr_ref[...] = h0r_ref[...]
        hi_ref[...] = h0i_ref[...]

    g = g_ref[...]
    for b in range(bb):
        xp_scr[b * pitch:b * pitch + tc, :] = _rms(x_ref[b], g)
        xp_scr[b * pitch + tc:b * pitch + tc + 1, :] = jnp.zeros((1, D_MODEL), F32)

    u = xp_scr[...]
    ub = u.astype(BF16)
    ys = []
    for kt in range(N_GTILES):
        bu = bu_scrs[kt]
        res = jnp.dot(ub[:, kt * MXU_DIM:(kt + 1) * MXU_DIM], wb_ref[kt], preferred_element_type=F32)
        for j in range(2 * spt):
            bu[j] = res[:, j * LANES:(j + 1) * LANES]
        for rg in range(bb // SUBLANES):
            rows = slice(rg * SUBLANES, (rg + 1) * SUBLANES)
            for j in range(spt):
                s = kt * spt + j
                lanes = slice(s * LANES, (s + 1) * LANES)
                lr = lam_ref[s]
                li = lam_ref[N_SLABS + s]
                hr = hr_ref[rows, lanes]
                hi = hi_ref[rows, lanes]
                for t in range(tc):
                    idx = pl.ds(rg * SUBLANES * pitch + t, SUBLANES, stride=pitch)
                    nr = (lr * hr - li * hi) + bu[j, idx, :]
                    ni = (lr * hi + li * hr) + bu[spt + j, idx, :]
                    bu[j, idx, :] = nr
                    bu[spt + j, idx, :] = ni
                    hr, hi = nr, ni
                hr_ref[rows, lanes] = hr
                hi_ref[rows, lanes] = hi
        hcat = jnp.concatenate([bu[j] for j in range(2 * spt)], axis=-1).astype(BF16)
        ys.append(jnp.dot(hcat, wc_ref[kt], preferred_element_type=F32))
    y = jnp.concatenate(ys, axis=-1) + d_ref[...] * u
    z = jax.nn.gelu(y).astype(BF16)
    gl = jnp.dot(z, wglu_ref[...], preferred_element_type=F32)
    xp_scr[...] = gl[:, :D_MODEL] * jax.nn.sigmoid(gl[:, D_MODEL:])
    for b in range(bb):
        o_ref[b] = x_ref[b] + xp_scr[b * pitch:b * pitch + tc, :]


def _ssm_call(x, g, wb, wc, lam, d, wglu, h0r, h0i, *, layer, bb, tc):
    bsz, seq, _ = x.shape
    mp = bb * (tc + 1)
    body = functools.partial(_ssm_body, bb=bb, tc=tc)
    return pl.pallas_call(
        body,
        grid=(bsz // bb, seq // tc),
        in_specs=[
            pl.BlockSpec((bb, tc, D_MODEL), lambda i, j: (i, j, 0)),
            _const_spec((1, D_MODEL)),
            _layer_spec(wb, layer),
            _layer_spec(wc, layer),
            _layer_spec(lam, layer),
            _const_spec((1, D_MODEL)),
            _layer_spec(wglu, layer),
            pl.BlockSpec((bb, N_STATE), lambda i, j: (i, 0)),
            pl.BlockSpec((bb, N_STATE), lambda i, j: (i, 0)),
        ],
        out_specs=[
            pl.BlockSpec((bb, tc, D_MODEL), lambda i, j: (i, j, 0)),
            pl.BlockSpec((bb, N_STATE), lambda i, j: (i, 0)),
            pl.BlockSpec((bb, N_STATE), lambda i, j: (i, 0)),
        ],
        out_shape=[
            jax.ShapeDtypeStruct(x.shape, F32),
            jax.ShapeDtypeStruct((bsz, N_STATE), F32),
            jax.ShapeDtypeStruct((bsz, N_STATE), F32),
        ],
        scratch_shapes=[pltpu.VMEM((mp, D_MODEL), F32)]
        + [pltpu.VMEM((2 * SLABS_PER_TILE, mp, LANES), F32) for _ in range(N_GTILES)],
        compiler_params=pltpu.CompilerParams(
            dimension_semantics=("arbitrary", "arbitrary"), vmem_limit_bytes=VMEM_LIMIT),
        name="ssm",
    )(x, g, wb, wc, lam, d, wglu, h0r, h0i)


def _qk_norm_rope(qk, ones_bd, gain, cos, sina, sinb):
    n_cols = qk.shape[1] // LANES
    sq = qk * qk
    hi = sq.astype(BF16)
    lo = (sq - hi.astype(F32)).astype(BF16)
    outs = []
    for c in range(n_cols // 2):
        sl = slice(c * MXU_DIM, (c + 1) * MXU_DIM)
        ss = (jnp.dot(hi[:, sl], ones_bd, preferred_element_type=F32)
              + jnp.dot(lo[:, sl], ones_bd, preferred_element_type=F32))
        outs.append(qk[:, sl] * lax.rsqrt(ss * (1.0 / HEAD_DIM) + NORM_EPS) * gain[:, sl])
    qn = jnp.concatenate(outs, axis=-1)
    cols = []
    for c in range(n_cols):
        xc = qn[:, c * LANES:(c + 1) * LANES]
        cols.append(xc * cos + pltpu.roll(xc, ROT_DIM // 2, 1) * sina
                    + pltpu.roll(xc, LANES - ROT_DIM // 2, 1) * sinb)
    return jnp.concatenate(cols, axis=-1)


def _lane_halves(x):
    lane = lax.broadcasted_iota(jnp.int32, x.shape, 1)
    low = lane < HEAD_DIM
    return jnp.where(low, x, 0.0), jnp.where(low, 0.0, x)


TQ = 256


def _attn_prompt_body(x_ref, g_ref, wqkv_ref, ones_ref, gain_ref, cos_ref, sina_ref, sinb_ref,
                      bias_ref, sink_ref, wo_ref, o_ref, kc_ref, vc_ref,
                      q_scr, ka_scr, kb_scr, va_scr, vb_scr, o_scr):
    i = pl.program_id(1)
    n_keys = WINDOW + TQ

    @pl.when(i == 0)
    def _():
        for scr in (ka_scr, kb_scr, va_scr, vb_scr):
            scr[:, 0:WINDOW, :] = jnp.zeros((N_KV_HEADS, WINDOW, LANES), BF16)

    x = x_ref[0]
    xn = _rms(x, g_ref[...]).astype(BF16)
    qkv = jnp.dot(xn, wqkv_ref[...], preferred_element_type=F32)
    qk = _qk_norm_rope(qkv[:, :D_QK], ones_ref[...], gain_ref[...],
                       cos_ref[...], sina_ref[...], sinb_ref[...])
    q_scr[...] = qk[:, :D_MODEL].astype(BF16)
    k = qk[:, D_MODEL:]
    v = qkv[:, D_QK:]
    kc_ref[0] = k[TQ - WINDOW:, :]
    vc_ref[0] = v[TQ - WINDOW:, :]

    for h in range(N_KV_HEADS):
        col = slice((h // 2) * LANES, (h // 2 + 1) * LANES)
        for src, dst_a, dst_b in ((k, ka_scr, kb_scr), (v, va_scr, vb_scr)):
            lo_part, hi_part = _lane_halves(src[:, col])
            if h % 2 == 0:
                a, b = lo_part, pltpu.roll(lo_part, HEAD_DIM, 1)
            else:
                a, b = pltpu.roll(hi_part, HEAD_DIM, 1), hi_part
            dst_a[h, WINDOW:n_keys, :] = a.astype(BF16)
            dst_b[h, WINDOW:n_keys, :] = b.astype(BF16)

    bias_mid = jnp.concatenate([bias_ref[...], bias_ref[...]], axis=0)
    col = lax.broadcasted_iota(jnp.int32, bias_mid.shape, 1)
    bias_first = jnp.where(jnp.logical_and(i == 0, col < WINDOW), NEG_INF, bias_mid)
    row = lax.broadcasted_iota(jnp.int32, (2 * WINDOW, 1), 0)
    nt_dims = (((1,), (1,)), ((), ()))
    for j in range(TQ // WINDOW):
        bias = bias_first if j == 0 else bias_mid
        qrows = slice(j * WINDOW, (j + 1) * WINDOW)
        krows = slice(j * WINDOW, j * WINDOW + 2 * WINDOW)
        for h in range(N_KV_HEADS):
            qs = jnp.concatenate([q_scr[qrows, (2 * h) * LANES:(2 * h + 1) * LANES],
                                  q_scr[qrows, (2 * h + 1) * LANES:(2 * h + 2) * LANES]], axis=0)
            o_pair = None
            for par, k_scr, v_scr in ((0, ka_scr, va_scr), (1, kb_scr, vb_scr)):
                s = lax.dot_general(qs, k_scr[h, krows, :], nt_dims, preferred_element_type=F32) + bias
                sink = jnp.where(row < WINDOW, sink_ref[4 * h + par], sink_ref[4 * h + 2 + par])
                m = jnp.maximum(jnp.max(s, axis=-1, keepdims=True), sink)
                p = jnp.exp(s - m)
                denom = jnp.sum(p, axis=-1, keepdims=True) + jnp.exp(sink - m)
                o = jnp.dot(p.astype(BF16), v_scr[h, krows, :], preferred_element_type=F32)
                o = o * (1.0 / denom)
                o_pair = o if o_pair is None else o_pair + o
            o_scr[qrows, (2 * h) * LANES:(2 * h + 1) * LANES] = o_pair[:WINDOW].astype(BF16)
            o_scr[qrows, (2 * h + 1) * LANES:(2 * h + 2) * LANES] = o_pair[WINDOW:].astype(BF16)

    o_ref[0] = x + jnp.dot(o_scr[...], wo_ref[...], preferred_element_type=F32)
    for scr in (ka_scr, kb_scr, va_scr, vb_scr):
        scr[:, 0:WINDOW, :] = scr[:, TQ:n_keys, :]


def _attn_prompt_call(x, g, wqkv, ones_bd, gain, cos, sina, sinb, bias, sinks, wo, *, layer):
    bsz, seq, _ = x.shape
    n_keys = WINDOW + TQ
    kv_scr = pltpu.VMEM((N_KV_HEADS, n_keys, LANES), BF16)
    return pl.pallas_call(
        _attn_prompt_body,
        grid=(bsz, seq // TQ),
        in_specs=[
            pl.BlockSpec((1, TQ, D_MODEL), lambda b, i: (b, i, 0)),
            _const_spec((1, D_MODEL)),
            _layer_spec(wqkv, layer),
            _const_spec((MXU_DIM, MXU_DIM)),
            _const_spec((1, D_QK)),
            pl.BlockSpec((TQ, LANES), lambda b, i: (i, 0)),
            pl.BlockSpec((TQ, LANES), lambda b, i: (i, 0)),
            pl.BlockSpec((TQ, LANES), lambda b, i: (i, 0)),
            _const_spec((WINDOW, 2 * WINDOW)),
            pl.BlockSpec(memory_space=pltpu.SMEM),
            _layer_spec(wo, layer),
        ],
        out_specs=[
            pl.BlockSpec((1, TQ, D_MODEL), lambda b, i: (b, i, 0)),
            pl.BlockSpec((1, WINDOW, D_KV), lambda b, i: (b, 0, 0)),
            pl.BlockSpec((1, WINDOW, D_KV), lambda b, i: (b, 0, 0)),
        ],
        out_shape=[
            jax.ShapeDtypeStruct(x.shape, F32),
            jax.ShapeDtypeStruct((bsz, WINDOW, D_KV), F32),
            jax.ShapeDtypeStruct((bsz, WINDOW, D_KV), F32),
        ],
        scratch_shapes=[pltpu.VMEM((TQ, D_MODEL), BF16), kv_scr, kv_scr, kv_scr, kv_scr,
                        pltpu.VMEM((TQ, D_MODEL), BF16)],
        compiler_params=pltpu.CompilerParams(
            dimension_semantics=("arbitrary", "arbitrary"), vmem_limit_bytes=VMEM_LIMIT),
        name="attn_prompt",
    )(x, g, wqkv, ones_bd, gain, cos, sina, sinb, bias, sinks, wo)


SEQ_BLOCK = 16
N_KEYS_PAD = 2 * WINDOW


def _attn_sample_body(x_ref, ck_ref, cv_ref, g_ref, wqkv_ref, ones_ref, gain_ref, cos_ref, sina_ref,
                      sinb_ref, bias_ref, sink_ref, wo_ref, o_ref, kc_ref, vc_ref,
                      kf_scr, vf_scr, oall_scr, *, s_len):
    rows = SEQ_BLOCK * s_len
    w_keep = WINDOW - s_len

    @pl.when(pl.program_id(0) == 0)
    def _():
        pad = (SEQ_BLOCK, N_KEYS_PAD - WINDOW, D_KV)
        kf_scr[:, WINDOW:, :] = jnp.zeros(pad, BF16)
        vf_scr[:, WINDOW:, :] = jnp.zeros(pad, BF16)

    x = x_ref[...].reshape(rows, D_MODEL)
    xn = _rms(x, g_ref[...]).astype(BF16)
    qkv = jnp.dot(xn, wqkv_ref[...], preferred_element_type=F32)
    qk = _qk_norm_rope(qkv[:, :D_QK], ones_ref[...], gain_ref[...],
                       cos_ref[...], sina_ref[...], sinb_ref[...])
    q = qk[:, :D_MODEL]
    k_new = qk[:, D_MODEL:].reshape(SEQ_BLOCK, s_len, D_KV)
    v_new = qkv[:, D_QK:].reshape(SEQ_BLOCK, s_len, D_KV)

    ck = ck_ref[...]
    cv = cv_ref[...]
    kc_ref[:, 0:w_keep, :] = ck[:, s_len:, :]
    vc_ref[:, 0:w_keep, :] = cv[:, s_len:, :]
    kc_ref[:, w_keep:, :] = k_new
    vc_ref[:, w_keep:, :] = v_new
    kf_scr[:, 0:WINDOW, :] = ck.astype(BF16)
    vf_scr[:, 0:WINDOW, :] = cv.astype(BF16)
    tile = 2 * SUBLANES
    zpad = jnp.zeros((SEQ_BLOCK, tile - s_len, D_KV), F32)
    kf_scr[:, WINDOW:WINDOW + tile, :] = jnp.concatenate([k_new, zpad], axis=1).astype(BF16)
    vf_scr[:, WINDOW:WINDOW + tile, :] = jnp.concatenate([v_new, zpad], axis=1).astype(BF16)

    bias = bias_ref[...]
    sink = sink_ref[...]
    nt_dims = (((1,), (1,)), ((), ()))
    zero_col = jnp.zeros((s_len, LANES), F32)
    for s in range(SEQ_BLOCK):
        qs = q[s * s_len:(s + 1) * s_len, :]
        tiles = []
        for h in range(N_KV_HEADS):
            for r in range(KV_REP):
                head = KV_REP * h + r
                src = qs[:, (head // 2) * LANES:(head // 2 + 1) * LANES]
                lo_part, hi_part = _lane_halves(src)
                part = lo_part if head % 2 == 0 else hi_part
                if head % 2 != h % 2:
                    part = pltpu.roll(part, HEAD_DIM, 1)
                tiles.append(jnp.concatenate([part, zero_col] if h < 2 else [zero_col, part], axis=1))
        wt = jnp.concatenate(tiles, axis=0).astype(BF16)
        sc = lax.dot_general(wt, kf_scr[s], nt_dims, preferred_element_type=F32) + bias
        m = jnp.maximum(jnp.max(sc, axis=-1, keepdims=True), sink)
        p = jnp.exp(sc - m)
        denom = jnp.sum(p, axis=-1, keepdims=True) + jnp.exp(sink - m)
        o = jnp.dot(p.astype(BF16), vf_scr[s], preferred_element_type=F32) * (1.0 / denom)
        cols = []
        for c in range(N_HEADS // 2):
            pieces = []
            for head in (2 * c, 2 * c + 1):
                h, r = divmod(head, KV_REP)
                blk = o[(KV_REP * h + r) * s_len:(KV_REP * h + r + 1) * s_len,
                        (h // 2) * LANES:(h // 2 + 1) * LANES]
                lo_part, hi_part = _lane_halves(blk)
                part = lo_part if h % 2 == 0 else hi_part
                if h % 2 != head % 2:
                    part = pltpu.roll(part, HEAD_DIM, 1)
                pieces.append(part)
            cols.append(pieces[0] + pieces[1])
        oall_scr[s * s_len:(s + 1) * s_len, :] = jnp.concatenate(cols, axis=-1)

    out = jnp.dot(oall_scr[...].astype(BF16), wo_ref[...], preferred_element_type=F32)
    o_ref[...] = (x + out).reshape(SEQ_BLOCK, s_len, D_MODEL)


def _attn_sample_call(x, ck, cv, g, wqkv, ones_bd, gain, cos, sina, sinb, bias, sink_rows, wo, *, layer):
    bsz, s_len, _ = x.shape
    rows = SEQ_BLOCK * s_len
    body = functools.partial(_attn_sample_body, s_len=s_len)
    seq_spec = lambda shape: pl.BlockSpec(shape, lambda i: (i, 0, 0))
    cache_spec = pl.BlockSpec((None, SEQ_BLOCK, WINDOW, D_KV), lambda i: (layer, i, 0, 0))
    return pl.pallas_call(
        body,
        grid=(bsz // SEQ_BLOCK,),
        in_specs=[
            seq_spec((SEQ_BLOCK, s_len, D_MODEL)),
            cache_spec,
            cache_spec,
            _const_spec((1, D_MODEL)),
            _layer_spec(wqkv, layer),
            _const_spec((MXU_DIM, MXU_DIM)),
            _const_spec((1, D_QK)),
            _const_spec((rows, LANES)),
            _const_spec((rows, LANES)),
            _const_spec((rows, LANES)),
            _const_spec((N_HEADS * s_len, N_KEYS_PAD)),
            _const_spec((N_HEADS * s_len, 1)),
            _layer_spec(wo, layer),
        ],
        out_specs=[
            seq_spec((SEQ_BLOCK, s_len, D_MODEL)),
            seq_spec((SEQ_BLOCK, WINDOW, D_KV)),
            seq_spec((SEQ_BLOCK, WINDOW, D_KV)),
        ],
        out_shape=[
            jax.ShapeDtypeStruct(x.shape, F32),
            jax.ShapeDtypeStruct(ck.shape[1:], F32),
            jax.ShapeDtypeStruct(cv.shape[1:], F32),
        ],
        scratch_shapes=[
            pltpu.VMEM((SEQ_BLOCK, N_KEYS_PAD, D_KV), BF16),
            pltpu.VMEM((SEQ_BLOCK, N_KEYS_PAD, D_KV), BF16),
            pltpu.VMEM((rows, D_MODEL), F32),
        ],
        compiler_params=pltpu.CompilerParams(
            dimension_semantics=("arbitrary",), vmem_limit_bytes=VMEM_LIMIT),
        name="attn_sample",
    )(x, ck, cv, g, wqkv, ones_bd, gain, cos, sina, sinb, bias, sink_rows, wo)


def _ssm_weights(a_re, a_im, log_dt, b_re, b_im, c_re, c_im):
    nl = a_re.shape[0]
    dt = jnp.exp(log_dt)[..., None]
    mag = jnp.exp(a_re * dt)
    lam_re = mag * jnp.cos(a_im * dt)
    lam_im = mag * jnp.sin(a_im * dt)
    den = a_re * a_re + a_im * a_im
    nr = lam_re - 1.0
    ni = lam_im
    f_re = (nr * a_re + ni * a_im) / den
    f_im = (ni * a_re - nr * a_im) / den
    bb_re = f_re[..., None] * b_re - f_im[..., None] * b_im
    bb_im = f_re[..., None] * b_im + f_im[..., None] * b_re
    gpt = MXU_DIM // SSM_GROUP
    n_st = gpt * STATE_DIM
    same_group = jnp.asarray(
        (np.arange(MXU_DIM)[:, None] // SSM_GROUP) == (np.arange(n_st)[None, :] // STATE_DIM))

    def b_tiles(bb):
        t = jnp.swapaxes(bb, 2, 3).reshape(nl, N_GTILES, MXU_DIM, STATE_DIM)
        return jnp.where(same_group, jnp.tile(t, (1, 1, 1, gpt)), 0.0)

    def c_tiles(cc):
        t = jnp.swapaxes(cc, 2, 3).reshape(nl, N_GTILES, n_st, SSM_GROUP)
        return jnp.where(same_group.T, jnp.tile(t, (1, 1, 1, gpt)), 0.0)

    wb = jnp.concatenate([b_tiles(bb_re), b_tiles(bb_im)], axis=-1).astype(BF16)
    wc = jnp.concatenate([c_tiles(c_re), -c_tiles(c_im)], axis=2).astype(BF16)
    lam = jnp.concatenate([lam_re.reshape(nl, N_SLABS, 1, LANES), lam_im.reshape(nl, N_SLABS, 1, LANES)], axis=1)
    lam = jnp.broadcast_to(lam, (nl, 2 * N_SLABS, SUBLANES, LANES))
    return wb, wc, lam


def _rope_tables(pos):
    half = ROT_DIM // 2
    inv_freq = ROPE_THETA ** (-jnp.arange(half, dtype=F32) * 2.0 / ROT_DIM)
    ang = pos[:, None] * inv_freq[None, :]
    cos, sin = jnp.cos(ang), jnp.sin(ang)
    n = pos.shape[0]
    ones = jnp.ones((n, HEAD_DIM - ROT_DIM), F32)
    zeros = jnp.zeros((n, HEAD_DIM - ROT_DIM), F32)
    zh = jnp.zeros((n, half), F32)
    cos_h = jnp.concatenate([cos, cos, ones], axis=1)
    sina_h = jnp.concatenate([zh, sin, zeros], axis=1)
    sinb_h = jnp.concatenate([-sin, zh, zeros], axis=1)
    return tuple(jnp.concatenate([t, t], axis=1) for t in (cos_h, sina_h, sinb_h))


def _ones_block_diag():
    idx = np.arange(MXU_DIM) // HEAD_DIM
    return jnp.asarray(idx[:, None] == idx[None, :], dtype=BF16)


def _prompt_bias():
    iq = np.arange(WINDOW)[:, None]
    js = np.arange(2 * WINDOW)[None, :]
    diff = WINDOW + iq - js
    return jnp.asarray(np.where((diff >= 0) & (diff < WINDOW), 0.0, NEG_INF), dtype=F32)


def _sample_bias(s_len):
    t = (np.arange(N_HEADS * s_len) % s_len)[:, None]
    key = np.arange(N_KEYS_PAD)[None, :]
    diff = np.where(key < WINDOW, WINDOW + t - key, t - (key - WINDOW))
    ok = (diff >= 0) & (diff < WINDOW) & (key < WINDOW + s_len)
    return jnp.asarray(np.where(ok, 0.0, NEG_INF), dtype=F32)


SSM_PROMPT_TC = 32
SSM_SAMPLE_BB = 32
FFN_TM = 512


def kernel(x_prompt, x_sample, state_ssm_re, state_ssm_im, cache_swa_k, cache_swa_v, norm_mix, norm_ffn, ssm_a_re, ssm_a_im, ssm_log_dt, ssm_b_re, ssm_b_im, ssm_c_re, ssm_c_im, ssm_d, ssm_w_glu, attn_w_qkv, attn_q_norm, attn_k_norm, attn_sinks, attn_w_o, ffn_w_gate_up, ffn_w_down):
    bsz, seq, _ = x_prompt.shape
    dbsz, dseq, _ = x_sample.shape
    yp, ys = x_prompt, x_sample
    p_re, p_im, p_k, p_v, s_re, s_im, s_k, s_v = ([] for _ in range(8))

    ones_bd = _ones_block_diag()
    p_tabs = _rope_tables(jnp.arange(seq, dtype=F32))
    s_pos = PAST_LEN + jnp.arange(dseq, dtype=F32)
    s_tabs = tuple(jnp.tile(t, (SEQ_BLOCK, 1)) for t in _rope_tables(s_pos))
    p_bias = _prompt_bias()
    s_bias = _sample_bias(dseq)
    h0 = jnp.zeros((bsz, N_STATE), F32)

    wb, wc, lam = _ssm_weights(ssm_a_re, ssm_a_im, ssm_log_dt, ssm_b_re, ssm_b_im, ssm_c_re, ssm_c_im)
    wglu = ssm_w_glu.astype(BF16)
    wqkv = attn_w_qkv.astype(BF16)
    wo = attn_w_o.astype(BF16)
    wgu = ffn_w_gate_up.astype(BF16)
    wd = ffn_w_down.astype(BF16)
    n_swa = cache_swa_k.shape[0]
    ck_all = cache_swa_k.reshape(n_swa, dbsz, WINDOW, D_KV)
    cv_all = cache_swa_v.reshape(n_swa, dbsz, WINDOW, D_KV)

    for i in range(DEPTH):
        j = i // 2
        g_mix = norm_mix[i].reshape(1, D_MODEL)
        if i % 2 == 0:
            d = ssm_d[j].reshape(1, D_MODEL)
            yp, hr, hi = _ssm_call(yp, g_mix, wb, wc, lam, d, wglu, h0, h0,
                                   layer=j, bb=bsz, tc=SSM_PROMPT_TC)
            ys, sr, si = _ssm_call(ys, g_mix, wb, wc, lam, d, wglu,
                                   state_ssm_re[j].reshape(dbsz, N_STATE), state_ssm_im[j].reshape(dbsz, N_STATE),
                                   layer=j, bb=SSM_SAMPLE_BB, tc=dseq)
            shp = (N_GROUPS, STATE_DIM)
            p_re.append(hr.reshape(bsz, *shp))
            p_im.append(hi.reshape(bsz, *shp))
            s_re.append(sr.reshape(dbsz, *shp))
            s_im.append(si.reshape(dbsz, *shp))
        else:
            gain = jnp.concatenate([jnp.tile(attn_q_norm[j] * ATTN_SCALE, N_HEADS),
                                    jnp.tile(attn_k_norm[j], N_KV_HEADS)]).reshape(1, D_QK)
            sinks = attn_sinks[j]
            yp, kp, vp = _attn_prompt_call(yp, g_mix, wqkv, ones_bd, gain, *p_tabs, p_bias, sinks, wo, layer=j)
            sink_rows = jnp.repeat(sinks, dseq).reshape(N_HEADS * dseq, 1)
            ys, ks_, vs_ = _attn_sample_call(ys, ck_all, cv_all, g_mix, wqkv, ones_bd, gain, *s_tabs, s_bias,
                                             sink_rows, wo, layer=j)
            kv_shape = (WINDOW, N_KV_HEADS, HEAD_DIM)
            p_k.append(kp.reshape(bsz, *kv_shape))
            p_v.append(vp.reshape(bsz, *kv_shape))
            s_k.append(ks_.reshape(dbsz, *kv_shape))
            s_v.append(vs_.reshape(dbsz, *kv_shape))
        g_ffn = norm_ffn[i].reshape(1, D_MODEL)
        yp = _ffn_call(yp.reshape(bsz * seq, D_MODEL), g_ffn, wgu, wd, layer=i, tm=FFN_TM).reshape(bsz, seq, D_MODEL)
        ys = _ffn_call(ys.reshape(dbsz * dseq, D_MODEL), g_ffn, wgu, wd, layer=i, tm=FFN_TM).reshape(dbsz, dseq, D_MODEL)

    return (yp, ys, jnp.stack(p_re), jnp.stack(p_im), jnp.stack(p_k), jnp.stack(p_v),
            jnp.stack(s_re), jnp.stack(s_im), jnp.stack(s_k), jnp.stack(s_v))
```

```python
import functools

import jax
import jax.numpy as jnp
import numpy as np
from jax import lax
from jax.experimental import pallas as pl
from jax.experimental.pallas import tpu as pltpu

F32 = jnp.float32
BF16 = jnp.bfloat16

D_MODEL = 1024
DEPTH = 4
PAST_LEN = 8192
SSM_GROUP = 16
N_GROUPS = D_MODEL // SSM_GROUP
STATE_DIM = 64
N_STATE = N_GROUPS * STATE_DIM
HEAD_DIM = 64
N_HEADS = D_MODEL // HEAD_DIM
N_KV_HEADS = 4
KV_REP = N_HEADS // N_KV_HEADS
D_KV = N_KV_HEADS * HEAD_DIM
D_QK = D_MODEL + D_KV
D_QKV = D_MODEL + 2 * D_KV
WINDOW = 128
ROT_DIM = HEAD_DIM // 4
ROPE_THETA = 500000.0
ATTN_SCALE = HEAD_DIM ** -0.5
D_FF = 2816
NORM_EPS = 1e-6
NEG_INF = -1e30

LANES = 128
SUBLANES = 8
MXU_DIM = 256
N_SLABS = N_STATE // LANES
VMEM_LIMIT = 56 * 1024 * 1024


def _rms(x, g):
    ms = jnp.mean(x * x, axis=-1, keepdims=True)
    return x * lax.rsqrt(ms + NORM_EPS) * g


def _const_spec(shape):
    nd = len(shape)
    return pl.BlockSpec(shape, lambda *_: (0,) * nd, pipeline_mode=pl.Buffered(1))


def _layer_spec(stack, layer):
    nd = stack.ndim - 1
    return pl.BlockSpec((None,) + stack.shape[1:], lambda *_: (layer,) + (0,) * nd,
                        pipeline_mode=pl.Buffered(1))


FF_CHUNK = 256


def _ffn_body(x_ref, g_ref, wgu_ref, wd_ref, o_ref, a_scr):
    x = x_ref[...]
    xn = _rms(x, g_ref[...]).astype(BF16)
    for c in range(D_FF // FF_CHUNK):
        lo = c * FF_CHUNK
        gate = jnp.dot(xn, wgu_ref[:, lo:lo + FF_CHUNK], preferred_element_type=F32)
        up = jnp.dot(xn, wgu_ref[:, D_FF + lo:D_FF + lo + FF_CHUNK], preferred_element_type=F32)
        a_scr[:, lo:lo + FF_CHUNK] = (gate * jax.nn.sigmoid(gate) * up).astype(BF16)
    o_ref[...] = x + jnp.dot(a_scr[...], wd_ref[...], preferred_element_type=F32)


def _ffn_call(x, g, wgu, wd, *, layer, tm):
    m = x.shape[0]
    return pl.pallas_call(
        _ffn_body,
        grid=(m // tm,),
        in_specs=[
            pl.BlockSpec((tm, D_MODEL), lambda i: (i, 0)),
            _const_spec((1, D_MODEL)),
            _layer_spec(wgu, layer),
            _layer_spec(wd, layer),
        ],
        out_specs=pl.BlockSpec((tm, D_MODEL), lambda i: (i, 0)),
        out_shape=jax.ShapeDtypeStruct((m, D_MODEL), F32),
        scratch_shapes=[pltpu.VMEM((tm, D_FF), BF16)],
        compiler_params=pltpu.CompilerParams(
            dimension_semantics=("arbitrary",), vmem_limit_bytes=VMEM_LIMIT),
        name="ffn",
    )(x, g, wgu, wd)


N_GTILES = D_MODEL // MXU_DIM
SLABS_PER_TILE = N_SLABS // N_GTILES


SSM_AHEAD = 2
SSM_ROW_PAD = 2


def _ssm_body(x_ref, g_ref, wb_ref, wc_ref, lam_ref, d_ref, wglu_ref, h0r_ref, h0i_ref,
              o_ref, hr_ref, hi_ref, xp_scr, hb_scr, y_scr, op_scr, xprev_scr, *bu_scrs, bb, tc, n_chunks,
              pipelined):
    pitch = tc + SSM_ROW_PAD
    spt = SLABS_PER_TILE
    step = pl.program_id(1)
    commit = step < n_chunks

    @pl.when(step == 0)
    def _():
        hr_ref[...] = h0r_ref[...]
        hi_ref[...] = h0i_ref[...]

    if pipelined:
        @pl.when(jnp.logical_and(step == 0, pl.program_id(0) == 0))
        def _():
            y_scr[...] = jnp.zeros(y_scr.shape, F32)
            xprev_scr[...] = jnp.zeros(xprev_scr.shape, F32)

    def mix_out(y):
        z = jax.nn.gelu(y).astype(BF16)
        gl = jnp.dot(z, wglu_ref[...], preferred_element_type=F32)
        op_scr[...] = gl[:, :D_MODEL] * jax.nn.sigmoid(gl[:, D_MODEL:])

    def residual_out(x_of):
        for b in range(bb):
            o_ref[b] = x_of(b) + op_scr[b * pitch:b * pitch + tc, :]

    g = g_ref[...]
    for b in range(bb):
        xp_scr[b * pitch:b * pitch + tc, :] = _rms(x_ref[b], g)
        xp_scr[b * pitch + tc:(b + 1) * pitch, :] = jnp.zeros((SSM_ROW_PAD, D_MODEL), F32)

    u = xp_scr[...]
    ub = u.astype(BF16)

    def project_in(kt):
        bu = bu_scrs[kt]
        res = jnp.dot(ub[:, kt * MXU_DIM:(kt + 1) * MXU_DIM], wb_ref[kt], preferred_element_type=F32)
        for j in range(2 * spt):
            bu[j] = res[:, j * LANES:(j + 1) * LANES]

    def scan(kt):
        bu = bu_scrs[kt]
        for rg in range(bb // SUBLANES):
            rows = slice(rg * SUBLANES, (rg + 1) * SUBLANES)
            for j in range(spt):
                s = kt * spt + j
                lanes = slice(s * LANES, (s + 1) * LANES)
                lr = lam_ref[s]
                li = lam_ref[N_SLABS + s]
                hr = hr_ref[rows, lanes]
                hi = hi_ref[rows, lanes]
                for t in range(tc):
                    idx = pl.ds(rg * SUBLANES * pitch + t, SUBLANES, stride=pitch)
                    nr = (lr * hr - li * hi) + bu[j, idx, :]
                    ni = (lr * hi + li * hr) + bu[spt + j, idx, :]
                    bu[j, idx, :] = nr
                    bu[spt + j, idx, :] = ni
                    hr, hi = nr, ni
                if pipelined:
                    hr = jnp.where(commit, hr, hr_ref[rows, lanes])
                    hi = jnp.where(commit, hi, hi_ref[rows, lanes])
                hr_ref[rows, lanes] = hr
                hi_ref[rows, lanes] = hi

    def narrow(kt):
        bu = bu_scrs[kt]
        for j in range(2 * spt):
            hb_scr[kt, :, j * LANES:(j + 1) * LANES] = bu[j].astype(BF16)

    def project_out(kt):
        return jnp.dot(hb_scr[kt], wc_ref[kt], preferred_element_type=F32)

    ys = []
    for kt in range(min(SSM_AHEAD, N_GTILES)):
        project_in(kt)
    if pipelined:
        z_prev = jax.nn.gelu(y_scr[...]).astype(BF16)
        half = D_MODEL // 2

        def mix_half(c):
            val = jnp.dot(z_prev, wglu_ref[:, c * half:(c + 1) * half], preferred_element_type=F32)
            gate = jnp.dot(z_prev, wglu_ref[:, D_MODEL + c * half:D_MODEL + (c + 1) * half],
                           preferred_element_type=F32)
            op_scr[:, c * half:(c + 1) * half] = val * jax.nn.sigmoid(gate)
    for kt in range(N_GTILES):
        scan(kt)
        narrow(kt)
        if pipelined and kt == 0:
            mix_half(0)
        if pipelined and kt == N_GTILES - 1:
            mix_half(1)
        ys.append(project_out(kt))
        if kt + SSM_AHEAD < N_GTILES:
            project_in(kt + SSM_AHEAD)
    y = jnp.concatenate(ys, axis=-1) + d_ref[...] * u
    if pipelined:
        residual_out(lambda b: xprev_scr[b])
        y_scr[...] = y
        xprev_scr[...] = x_ref[...]
    else:
        mix_out(y)
        residual_out(lambda b: x_ref[b])


def _ssm_call(x, g, wb, wc, lam, d, wglu, h0r, h0i, *, layer, bb, tc):
    bsz, seq, _ = x.shape
    mp = bb * (tc + SSM_ROW_PAD)
    n_chunks = seq // tc
    pipelined = n_chunks > 1
    body = functools.partial(_ssm_body, bb=bb, tc=tc, n_chunks=n_chunks, pipelined=pipelined)
    lag = 1 if pipelined else 0
    return pl.pallas_call(
        body,
        grid=(bsz // bb, n_chunks + lag),
        in_specs=[
            pl.BlockSpec((bb, tc, D_MODEL), lambda i, j: (i, jnp.minimum(j, n_chunks - 1), 0)),
            _const_spec((1, D_MODEL)),
            _layer_spec(wb, layer),
            _layer_spec(wc, layer),
            _layer_spec(lam, layer),
            _const_spec((1, D_MODEL)),
            _layer_spec(wglu, layer),
            pl.BlockSpec((bb, N_STATE), lambda i, j: (i, 0)),
            pl.BlockSpec((bb, N_STATE), lambda i, j: (i, 0)),
        ],
        out_specs=[
            pl.BlockSpec((bb, tc, D_MODEL), lambda i, j: (i, jnp.maximum(j - lag, 0), 0)),
            pl.BlockSpec((bb, N_STATE), lambda i, j: (i, 0)),
            pl.BlockSpec((bb, N_STATE), lambda i, j: (i, 0)),
        ],
        out_shape=[
            jax.ShapeDtypeStruct(x.shape, F32),
            jax.ShapeDtypeStruct((bsz, N_STATE), F32),
            jax.ShapeDtypeStruct((bsz, N_STATE), F32),
        ],
        scratch_shapes=[pltpu.VMEM((mp, D_MODEL), F32),
                        pltpu.VMEM((N_GTILES, mp, 2 * SLABS_PER_TILE * LANES), BF16),
                        pltpu.VMEM((mp, D_MODEL), F32),
                        pltpu.VMEM((mp, D_MODEL), F32),
                        pltpu.VMEM((bb, tc, D_MODEL), F32)]
        + [pltpu.VMEM((2 * SLABS_PER_TILE, mp, LANES), F32) for _ in range(N_GTILES)],
        compiler_params=pltpu.CompilerParams(
            dimension_semantics=("arbitrary", "arbitrary"), vmem_limit_bytes=VMEM_LIMIT),
        name="ssm",
    )(x, g, wb, wc, lam, d, wglu, h0r, h0i)


def _qk_norm_rope(qk, ones_bd, gain, cos, sina, sinb):
    n_cols = qk.shape[1] // LANES
    sq = qk * qk
    hi = sq.astype(BF16)
    lo = (sq - hi.astype(F32)).astype(BF16)
    outs = []
    for c in range(n_cols // 2):
        sl = slice(c * MXU_DIM, (c + 1) * MXU_DIM)
        ss = (jnp.dot(hi[:, sl], ones_bd, preferred_element_type=F32)
              + jnp.dot(lo[:, sl], ones_bd, preferred_element_type=F32))
        outs.append(qk[:, sl] * lax.rsqrt(ss * (1.0 / HEAD_DIM) + NORM_EPS) * gain[:, sl])
    qn = jnp.concatenate(outs, axis=-1)
    cols = []
    for c in range(n_cols):
        xc = qn[:, c * LANES:(c + 1) * LANES]
        cols.append(xc * cos + pltpu.roll(xc, ROT_DIM // 2, 1) * sina
                    + pltpu.roll(xc, LANES - ROT_DIM // 2, 1) * sinb)
    return jnp.concatenate(cols, axis=-1)


def _lane_halves(x):
    lane = lax.broadcasted_iota(jnp.int32, x.shape, 1)
    low = lane < HEAD_DIM
    return jnp.where(low, x, 0.0), jnp.where(low, 0.0, x)


TQ = 256
ATTN_AHEAD = 2


def _attn_prompt_body(x_ref, g_ref, wqkv_ref, ones_ref, gain_ref, cos_ref, sina_ref, sinb_ref,
                      bias_ref, sink_ref, wo_ref, o_ref, kc_ref, vc_ref,
                      qt_cur, ka_cur, kb_cur, vt_cur, ot_scr, xprev_scr, qt_nxt, ka_nxt, kb_nxt, vt_nxt,
                      ka_prev, kb_prev, vt_prev, *, blocks_per_seq):
    g = pl.program_id(0)

    @pl.when(g == 0)
    def _():
        for scr in (ka_cur, kb_cur, ka_prev, kb_prev):
            scr[...] = jnp.zeros(scr.shape, BF16)
        for scr in (vt_cur, vt_prev):
            scr[:, 0:HEAD_DIM, :] = jnp.zeros((N_KV_HEADS, HEAD_DIM, scr.shape[2]), BF16)
            scr[:, HEAD_DIM:, :] = jnp.ones((N_KV_HEADS, HEAD_DIM, scr.shape[2]), BF16)
        qt_cur[...] = jnp.zeros((D_MODEL, TQ), BF16)
        xprev_scr[...] = jnp.zeros((TQ, D_MODEL), F32)

    bias_mid = jnp.concatenate([bias_ref[...], bias_ref[...]], axis=1)
    first = (g - 1) % blocks_per_seq == 0
    key_row = lax.broadcasted_iota(jnp.int32, bias_mid.shape, 0)
    bias_first = jnp.where(jnp.logical_and(first, key_row < WINDOW), NEG_INF, bias_mid)
    lane = lax.broadcasted_iota(jnp.int32, (1, 2 * WINDOW), 1)
    pair = 2 * HEAD_DIM

    def key_window(j, h, prev, cur):
        if j == 0:
            return [prev[h], cur[h, 0:WINDOW, :]]
        return [cur[h, (j - 1) * WINDOW:(j + 1) * WINDOW, :]]

    def value_window(j, h):
        if j == 0:
            return jnp.concatenate([vt_prev[h], vt_cur[h, :, 0:WINDOW]], axis=1)
        return vt_cur[h, :, (j - 1) * WINDOW:(j + 1) * WINDOW]

    def scores(j, h):
        qpos = slice(j * WINDOW, (j + 1) * WINDOW)
        qs = jnp.concatenate([qt_cur[(2 * h) * pair:(2 * h + 1) * pair, qpos],
                              qt_cur[(2 * h + 1) * pair:(2 * h + 2) * pair, qpos]], axis=1)
        kk = jnp.concatenate(key_window(j, h, ka_prev, ka_cur) + key_window(j, h, kb_prev, kb_cur),
                             axis=0)
        return jnp.dot(kk, qs, preferred_element_type=F32)

    def softmax(j, h, st):
        bias = bias_first if j == 0 else bias_mid
        out = []
        for par in range(2):
            s = st[par * 2 * WINDOW:(par + 1) * 2 * WINDOW, :] + bias
            sink = jnp.where(lane < WINDOW, sink_ref[4 * h + par], sink_ref[4 * h + 2 + par])
            m = jnp.maximum(jnp.max(s, axis=0, keepdims=True), sink)
            out.append((jnp.exp(s - m).astype(BF16), jnp.exp(sink - m)))
        return out

    def weighted_values(j, h, probs):
        qpos = slice(j * WINDOW, (j + 1) * WINDOW)
        vth = value_window(j, h)
        for par, (p, sink_term) in enumerate(probs):
            nd = jnp.dot(vth, p, preferred_element_type=F32)
            denom = nd[HEAD_DIM:, :] + sink_term
            on = nd[:HEAD_DIM, :] * (1.0 / denom)
            for rp in range(2):
                head = 4 * h + 2 * rp + par
                ot_scr[head * HEAD_DIM:(head + 1) * HEAD_DIM, qpos] = on[:, rp * WINDOW:(rp + 1) * WINDOW]

    x = x_ref[0]
    xn = _rms(x, g_ref[...]).astype(BF16)
    tabs = (cos_ref[...], sina_ref[...], sinb_ref[...])

    def project(c):
        cols = slice(c * MXU_DIM, (c + 1) * MXU_DIM)
        t = jnp.dot(xn, wqkv_ref[:, cols], preferred_element_type=F32)
        if c * MXU_DIM < D_QK:
            t = _qk_norm_rope(t, ones_ref[...], gain_ref[:, cols], *tabs)
        if c * MXU_DIM < D_MODEL:
            qt_nxt[cols, :] = t.T.astype(BF16)
        elif c * MXU_DIM < D_QK:
            kc_ref[0] = t[TQ - WINDOW:, :]
            for h in range(N_KV_HEADS):
                lo_part, hi_part = _lane_halves(t[:, (h // 2) * LANES:(h // 2 + 1) * LANES])
                if h % 2 == 0:
                    a, b = lo_part, pltpu.roll(lo_part, HEAD_DIM, 1)
                else:
                    a, b = pltpu.roll(hi_part, HEAD_DIM, 1), hi_part
                ka_nxt[h] = a.astype(BF16)
                kb_nxt[h] = b.astype(BF16)
        else:
            vc_ref[0] = t[TQ - WINDOW:, :]
            vt = t.T
            for h in range(N_KV_HEADS):
                vt_nxt[h] = vt[h * HEAD_DIM:(h + 1) * HEAD_DIM, :].astype(BF16)

    groups = [(j, h) for j in range(TQ // WINDOW) for h in range(N_KV_HEADS)]
    n_groups = len(groups)
    st, probs = {}, {}
    for n in range(n_groups + ATTN_AHEAD):
        if n < n_groups:
            st[n] = scores(*groups[n])
        if n < D_QKV // MXU_DIM:
            project(n)
        d = n - ATTN_AHEAD + 1
        if 0 <= d < n_groups:
            probs[d] = softmax(*groups[d], st.pop(d))
        d = n - ATTN_AHEAD
        if 0 <= d < n_groups:
            weighted_values(*groups[d], probs.pop(d))
    attn = ot_scr[...].T.astype(BF16)
    o_ref[0] = xprev_scr[...] + jnp.dot(attn, wo_ref[...], preferred_element_type=F32)

    xprev_scr[...] = x
    qt_cur[...] = qt_nxt[...]
    for prev, cur, nxt in ((ka_prev, ka_cur, ka_nxt), (kb_prev, kb_cur, kb_nxt)):
        prev[...] = cur[:, TQ - WINDOW:, :]
        cur[...] = nxt[...]
    vt_prev[:, 0:HEAD_DIM, :] = vt_cur[:, 0:HEAD_DIM, TQ - WINDOW:]
    vt_cur[:, 0:HEAD_DIM, :] = vt_nxt[...]


def _attn_prompt_call(x, g, wqkv, ones_bd, gain, cos, sina, sinb, bias, sinks, wo, *, layer):
    bsz, seq, _ = x.shape
    bps = seq // TQ
    n_blocks = bsz * bps
    k_blk = pltpu.VMEM((N_KV_HEADS, TQ, LANES), BF16)
    k_tail = pltpu.VMEM((N_KV_HEADS, WINDOW, LANES), BF16)
    vt_blk = pltpu.VMEM((N_KV_HEADS, 2 * HEAD_DIM, TQ), BF16)
    vt_tail = pltpu.VMEM((N_KV_HEADS, 2 * HEAD_DIM, WINDOW), BF16)

    def proj_block(gi):
        blk = jnp.minimum(gi, n_blocks - 1)
        return blk // bps, blk % bps

    def attn_block(gi):
        blk = jnp.maximum(gi - 1, 0)
        return blk // bps, blk % bps

    tab_spec = pl.BlockSpec((TQ, LANES), lambda gi: (proj_block(gi)[1], 0))
    cache_spec = pl.BlockSpec((1, WINDOW, D_KV), lambda gi: (proj_block(gi)[0], 0, 0))
    body = functools.partial(_attn_prompt_body, blocks_per_seq=bps)
    return pl.pallas_call(
        body,
        grid=(n_blocks + 1,),
        in_specs=[
            pl.BlockSpec((1, TQ, D_MODEL), lambda gi: (*proj_block(gi), 0)),
            _const_spec((1, D_MODEL)),
            _layer_spec(wqkv, layer),
            _const_spec((MXU_DIM, MXU_DIM)),
            _const_spec((1, D_QK)),
            tab_spec, tab_spec, tab_spec,
            _const_spec((2 * WINDOW, WINDOW)),
            pl.BlockSpec(memory_space=pltpu.SMEM),
            _layer_spec(wo, layer),
        ],
        out_specs=[
            pl.BlockSpec((1, TQ, D_MODEL), lambda gi: (*attn_block(gi), 0)),
            cache_spec, cache_spec,
        ],
        out_shape=[
            jax.ShapeDtypeStruct(x.shape, F32),
            jax.ShapeDtypeStruct((bsz, WINDOW, D_KV), F32),
            jax.ShapeDtypeStruct((bsz, WINDOW, D_KV), F32),
        ],
        scratch_shapes=[pltpu.VMEM((D_MODEL, TQ), BF16), k_blk, k_blk, vt_blk,
                        pltpu.VMEM((D_MODEL, TQ), F32),
                        pltpu.VMEM((TQ, D_MODEL), F32),
                        pltpu.VMEM((D_MODEL, TQ), BF16), k_blk, k_blk,
                        pltpu.VMEM((N_KV_HEADS, HEAD_DIM, TQ), BF16),
                        k_tail, k_tail, vt_tail],
        compiler_params=pltpu.CompilerParams(
            dimension_semantics=("arbitrary",), vmem_limit_bytes=VMEM_LIMIT),
        name="attn_prompt",
    )(x, g, wqkv, ones_bd, gain, cos, sina, sinb, bias, sinks, wo)


SEQ_BLOCK = 16
N_KEYS_PAD = 2 * WINDOW


def _attn_sample_body(x_ref, ck_ref, cv_ref, g_ref, wqkv_ref, ones_ref, gain_ref, cos_ref, sina_ref,
                      sinb_ref, bias_ref, sink_ref, wo_ref, o_ref, kc_ref, vc_ref,
                      kf_scr, vf_scr, oall_scr, *, s_len):
    rows = SEQ_BLOCK * s_len
    w_keep = WINDOW - s_len

    @pl.when(pl.program_id(0) == 0)
    def _():
        pad = (SEQ_BLOCK, N_KEYS_PAD - WINDOW, D_KV)
        kf_scr[:, WINDOW:, :] = jnp.zeros(pad, BF16)
        vf_scr[:, WINDOW:, :] = jnp.zeros(pad, BF16)

    x = x_ref[...].reshape(rows, D_MODEL)
    xn = _rms(x, g_ref[...]).astype(BF16)
    qkv = jnp.dot(xn, wqkv_ref[...], preferred_element_type=F32)
    qk = _qk_norm_rope(qkv[:, :D_QK], ones_ref[...], gain_ref[...],
                       cos_ref[...], sina_ref[...], sinb_ref[...])
    q = qk[:, :D_MODEL]
    k_new = qk[:, D_MODEL:].reshape(SEQ_BLOCK, s_len, D_KV)
    v_new = qkv[:, D_QK:].reshape(SEQ_BLOCK, s_len, D_KV)

    ck = ck_ref[...]
    cv = cv_ref[...]
    kc_ref[:, 0:w_keep, :] = ck[:, s_len:, :]
    vc_ref[:, 0:w_keep, :] = cv[:, s_len:, :]
    kc_ref[:, w_keep:, :] = k_new
    vc_ref[:, w_keep:, :] = v_new
    kf_scr[:, 0:WINDOW, :] = ck.astype(BF16)
    vf_scr[:, 0:WINDOW, :] = cv.astype(BF16)
    tile = 2 * SUBLANES
    zpad = jnp.zeros((SEQ_BLOCK, tile - s_len, D_KV), F32)
    kf_scr[:, WINDOW:WINDOW + tile, :] = jnp.concatenate([k_new, zpad], axis=1).astype(BF16)
    vf_scr[:, WINDOW:WINDOW + tile, :] = jnp.concatenate([v_new, zpad], axis=1).astype(BF16)

    bias = bias_ref[...]
    sink = sink_ref[...]
    nt_dims = (((1,), (1,)), ((), ()))
    zero_col = jnp.zeros((s_len, LANES), F32)
    for s in range(SEQ_BLOCK):
        qs = q[s * s_len:(s + 1) * s_len, :]
        tiles = []
        for h in range(N_KV_HEADS):
            for r in range(KV_REP):
                head = KV_REP * h + r
                src = qs[:, (head // 2) * LANES:(head // 2 + 1) * LANES]
                lo_part, hi_part = _lane_halves(src)
                part = lo_part if head % 2 == 0 else hi_part
                if head % 2 != h % 2:
                    part = pltpu.roll(part, HEAD_DIM, 1)
                tiles.append(jnp.concatenate([part, zero_col] if h < 2 else [zero_col, part], axis=1))
        wt = jnp.concatenate(tiles, axis=0).astype(BF16)
        sc = lax.dot_general(wt, kf_scr[s], nt_dims, preferred_element_type=F32) + bias
        m = jnp.maximum(jnp.max(sc, axis=-1, keepdims=True), sink)
        p = jnp.exp(sc - m)
        denom = jnp.sum(p, axis=-1, keepdims=True) + jnp.exp(sink - m)
        o = jnp.dot(p.astype(BF16), vf_scr[s], preferred_element_type=F32) * (1.0 / denom)
        cols = []
        for c in range(N_HEADS // 2):
            pieces = []
            for head in (2 * c, 2 * c + 1):
                h, r = divmod(head, KV_REP)
                blk = o[(KV_REP * h + r) * s_len:(KV_REP * h + r + 1) * s_len,
                        (h // 2) * LANES:(h // 2 + 1) * LANES]
                lo_part, hi_part = _lane_halves(blk)
                part = lo_part if h % 2 == 0 else hi_part
                if h % 2 != head % 2:
                    part = pltpu.roll(part, HEAD_DIM, 1)
                pieces.append(part)
            cols.append(pieces[0] + pieces[1])
        oall_scr[s * s_len:(s + 1) * s_len, :] = jnp.concatenate(cols, axis=-1)

    out = jnp.dot(oall_scr[...].astype(BF16), wo_ref[...], preferred_element_type=F32)
    o_ref[...] = (x + out).reshape(SEQ_BLOCK, s_len, D_MODEL)


def _attn_sample_call(x, ck, cv, g, wqkv, ones_bd, gain, cos, sina, sinb, bias, sink_rows, wo, *, layer):
    bsz, s_len, _ = x.shape
    rows = SEQ_BLOCK * s_len
    body = functools.partial(_attn_sample_body, s_len=s_len)
    seq_spec = lambda shape: pl.BlockSpec(shape, lambda i: (i, 0, 0))
    cache_spec = pl.BlockSpec((None, SEQ_BLOCK, WINDOW, D_KV), lambda i: (layer, i, 0, 0))
    return pl.pallas_call(
        body,
        grid=(bsz // SEQ_BLOCK,),
        in_specs=[
            seq_spec((SEQ_BLOCK, s_len, D_MODEL)),
            cache_spec,
            cache_spec,
            _const_spec((1, D_MODEL)),
            _layer_spec(wqkv, layer),
            _const_spec((MXU_DIM, MXU_DIM)),
            _const_spec((1, D_QK)),
            _const_spec((rows, LANES)),
            _const_spec((rows, LANES)),
            _const_spec((rows, LANES)),
            _const_spec((N_HEADS * s_len, N_KEYS_PAD)),
            _const_spec((N_HEADS * s_len, 1)),
            _layer_spec(wo, layer),
        ],
        out_specs=[
            seq_spec((SEQ_BLOCK, s_len, D_MODEL)),
            seq_spec((SEQ_BLOCK, WINDOW, D_KV)),
            seq_spec((SEQ_BLOCK, WINDOW, D_KV)),
        ],
        out_shape=[
            jax.ShapeDtypeStruct(x.shape, F32),
            jax.ShapeDtypeStruct(ck.shape[1:], F32),
            jax.ShapeDtypeStruct(cv.shape[1:], F32),
        ],
        scratch_shapes=[
            pltpu.VMEM((SEQ_BLOCK, N_KEYS_PAD, D_KV), BF16),
            pltpu.VMEM((SEQ_BLOCK, N_KEYS_PAD, D_KV), BF16),
            pltpu.VMEM((rows, D_MODEL), F32),
        ],
        compiler_params=pltpu.CompilerParams(
            dimension_semantics=("arbitrary",), vmem_limit_bytes=VMEM_LIMIT),
        name="attn_sample",
    )(x, ck, cv, g, wqkv, ones_bd, gain, cos, sina, sinb, bias, sink_rows, wo)


def _ssm_weights(a_re, a_im, log_dt, b_re, b_im, c_re, c_im):
    nl = a_re.shape[0]
    dt = jnp.exp(log_dt)[..., None]
    mag = jnp.exp(a_re * dt)
    lam_re = mag * jnp.cos(a_im * dt)
    lam_im = mag * jnp.sin(a_im * dt)
    den = a_re * a_re + a_im * a_im
    nr = lam_re - 1.0
    ni = lam_im
    f_re = (nr * a_re + ni * a_im) / den
    f_im = (ni * a_re - nr * a_im) / den
    bb_re = f_re[..., None] * b_re - f_im[..., None] * b_im
    bb_im = f_re[..., None] * b_im + f_im[..., None] * b_re
    gpt = MXU_DIM // SSM_GROUP
    n_st = gpt * STATE_DIM
    same_group = jnp.asarray(
        (np.arange(MXU_DIM)[:, None] // SSM_GROUP) == (np.arange(n_st)[None, :] // STATE_DIM))

    def b_tiles(bb):
        t = jnp.swapaxes(bb, 2, 3).reshape(nl, N_GTILES, MXU_DIM, STATE_DIM)
        return jnp.where(same_group, jnp.tile(t, (1, 1, 1, gpt)), 0.0)

    def c_tiles(cc):
        t = jnp.swapaxes(cc, 2, 3).reshape(nl, N_GTILES, n_st, SSM_GROUP)
        return jnp.where(same_group.T, jnp.tile(t, (1, 1, 1, gpt)), 0.0)

    wb = jnp.concatenate([b_tiles(bb_re), b_tiles(bb_im)], axis=-1).astype(BF16)
    wc = jnp.concatenate([c_tiles(c_re), -c_tiles(c_im)], axis=2).astype(BF16)
    lam = jnp.concatenate([lam_re.reshape(nl, N_SLABS, 1, LANES), lam_im.reshape(nl, N_SLABS, 1, LANES)], axis=1)
    lam = jnp.broadcast_to(lam, (nl, 2 * N_SLABS, SUBLANES, LANES))
    return wb, wc, lam


def _rope_tables(pos):
    half = ROT_DIM // 2
    inv_freq = ROPE_THETA ** (-jnp.arange(half, dtype=F32) * 2.0 / ROT_DIM)
    ang = pos[:, None] * inv_freq[None, :]
    cos, sin = jnp.cos(ang), jnp.sin(ang)
    n = pos.shape[0]
    ones = jnp.ones((n, HEAD_DIM - ROT_DIM), F32)
    zeros = jnp.zeros((n, HEAD_DIM - ROT_DIM), F32)
    zh = jnp.zeros((n, half), F32)
    cos_h = jnp.concatenate([cos, cos, ones], axis=1)
    sina_h = jnp.concatenate([zh, sin, zeros], axis=1)
    sinb_h = jnp.concatenate([-sin, zh, zeros], axis=1)
    return tuple(jnp.concatenate([t, t], axis=1) for t in (cos_h, sina_h, sinb_h))


def _ones_block_diag():
    idx = np.arange(MXU_DIM) // HEAD_DIM
    return jnp.asarray(idx[:, None] == idx[None, :], dtype=BF16)


def _prompt_bias():
    js = np.arange(2 * WINDOW)[:, None]
    iq = np.arange(WINDOW)[None, :]
    diff = WINDOW + iq - js
    return jnp.asarray(np.where((diff >= 0) & (diff < WINDOW), 0.0, NEG_INF), dtype=F32)


def _sample_bias(s_len):
    t = (np.arange(N_HEADS * s_len) % s_len)[:, None]
    key = np.arange(N_KEYS_PAD)[None, :]
    diff = np.where(key < WINDOW, WINDOW + t - key, t - (key - WINDOW))
    ok = (diff >= 0) & (diff < WINDOW) & (key < WINDOW + s_len)
    return jnp.asarray(np.where(ok, 0.0, NEG_INF), dtype=F32)


SSM_PROMPT_TC = 32
SSM_SAMPLE_BB = 32
FFN_TM = 512


def kernel(x_prompt, x_sample, state_ssm_re, state_ssm_im, cache_swa_k, cache_swa_v, norm_mix, norm_ffn, ssm_a_re, ssm_a_im, ssm_log_dt, ssm_b_re, ssm_b_im, ssm_c_re, ssm_c_im, ssm_d, ssm_w_glu, attn_w_qkv, attn_q_norm, attn_k_norm, attn_sinks, attn_w_o, ffn_w_gate_up, ffn_w_down):
    bsz, seq, _ = x_prompt.shape
    dbsz, dseq, _ = x_sample.shape
    yp, ys = x_prompt, x_sample
    p_re, p_im, p_k, p_v, s_re, s_im, s_k, s_v = ([] for _ in range(8))

    ones_bd = _ones_block_diag()
    p_tabs = _rope_tables(jnp.arange(seq, dtype=F32))
    s_pos = PAST_LEN + jnp.arange(dseq, dtype=F32)
    s_tabs = tuple(jnp.tile(t, (SEQ_BLOCK, 1)) for t in _rope_tables(s_pos))
    p_bias = _prompt_bias()
    s_bias = _sample_bias(dseq)
    h0 = jnp.zeros((bsz, N_STATE), F32)

    wb, wc, lam = _ssm_weights(ssm_a_re, ssm_a_im, ssm_log_dt, ssm_b_re, ssm_b_im, ssm_c_re, ssm_c_im)
    wglu = ssm_w_glu.astype(BF16)
    wqkv = attn_w_qkv.astype(BF16)
    wo = attn_w_o.astype(BF16)
    wgu = ffn_w_gate_up.astype(BF16)
    wd = ffn_w_down.astype(BF16)
    n_swa = cache_swa_k.shape[0]
    ck_all = cache_swa_k.reshape(n_swa, dbsz, WINDOW, D_KV)
    cv_all = cache_swa_v.reshape(n_swa, dbsz, WINDOW, D_KV)

    for i in range(DEPTH):
        j = i // 2
        g_mix = norm_mix[i].reshape(1, D_MODEL)
        if i % 2 == 0:
            d = ssm_d[j].reshape(1, D_MODEL)
            yp, hr, hi = _ssm_call(yp, g_mix, wb, wc, lam, d, wglu, h0, h0,
                                   layer=j, bb=bsz, tc=SSM_PROMPT_TC)
            ys, sr, si = _ssm_call(ys, g_mix, wb, wc, lam, d, wglu,
                                   state_ssm_re[j].reshape(dbsz, N_STATE), state_ssm_im[j].reshape(dbsz, N_STATE),
                                   layer=j, bb=SSM_SAMPLE_BB, tc=dseq)
            shp = (N_GROUPS, STATE_DIM)
            p_re.append(hr.reshape(bsz, *shp))
            p_im.append(hi.reshape(bsz, *shp))
            s_re.append(sr.reshape(dbsz, *shp))
            s_im.append(si.reshape(dbsz, *shp))
        else:
            gain = jnp.concatenate([jnp.tile(attn_q_norm[j] * ATTN_SCALE, N_HEADS),
                                    jnp.tile(attn_k_norm[j], N_KV_HEADS)]).reshape(1, D_QK)
            sinks = attn_sinks[j]
            yp, kp, vp = _attn_prompt_call(yp, g_mix, wqkv, ones_bd, gain, *p_tabs, p_bias, sinks, wo, layer=j)
            sink_rows = jnp.repeat(sinks, dseq).reshape(N_HEADS * dseq, 1)
            ys, ks_, vs_ = _attn_sample_call(ys, ck_all, cv_all, g_mix, wqkv, ones_bd, gain, *s_tabs, s_bias,
                                             sink_rows, wo, layer=j)
            kv_shape = (WINDOW, N_KV_HEADS, HEAD_DIM)
            p_k.append(kp.reshape(bsz, *kv_shape))
            p_v.append(vp.reshape(bsz, *kv_shape))
            s_k.append(ks_.reshape(dbsz, *kv_shape))
            s_v.append(vs_.reshape(dbsz, *kv_shape))
        g_ffn = norm_ffn[i].reshape(1, D_MODEL)
        yp = _ffn_call(yp.reshape(bsz * seq, D_MODEL), g_ffn, wgu, wd, layer=i, tm=FFN_TM).reshape(bsz, seq, D_MODEL)
        ys = _ffn_call(ys.reshape(dbsz * dseq, D_MODEL), g_ffn, wgu, wd, layer=i, tm=FFN_TM).reshape(dbsz, dseq, D_MODEL)

    return (yp, ys, jnp.stack(p_re), jnp.stack(p_im), jnp.stack(p_k), jnp.stack(p_v),
            jnp.stack(s_re), jnp.stack(s_im), jnp.stack(s_k), jnp.stack(s_v))
```

```python
import functools

import jax
import jax.numpy as jnp
import numpy as np
from jax import lax
from jax.experimental import pallas as pl
from jax.experimental.pallas import tpu as pltpu

F32 = jnp.float32
BF16 = jnp.bfloat16

D_MODEL = 1024
DEPTH = 4
PAST_LEN = 8192
SSM_GROUP = 16
N_GROUPS = D_MODEL // SSM_GROUP
STATE_DIM = 64
N_STATE = N_GROUPS * STATE_DIM
HEAD_DIM = 64
N_HEADS = D_MODEL // HEAD_DIM
N_KV_HEADS = 4
KV_REP = N_HEADS // N_KV_HEADS
D_KV = N_KV_HEADS * HEAD_DIM
D_QK = D_MODEL + D_KV
D_QKV = D_MODEL + 2 * D_KV
WINDOW = 128
ROT_DIM = HEAD_DIM // 4
ROPE_THETA = 500000.0
ATTN_SCALE = HEAD_DIM ** -0.5
D_FF = 2816
NORM_EPS = 1e-6
NEG_INF = -1e30

LANES = 128
SUBLANES = 8
MXU_DIM = 256
N_SLABS = N_STATE // LANES
VMEM_LIMIT = 56 * 1024 * 1024


def _rms(x, g):
    ms = jnp.mean(x * x, axis=-1, keepdims=True)
    return x * lax.rsqrt(ms + NORM_EPS) * g


def _const_spec(shape):
    nd = len(shape)
    return pl.BlockSpec(shape, lambda *_: (0,) * nd, pipeline_mode=pl.Buffered(1))


def _layer_spec(stack, layer):
    nd = stack.ndim - 1
    return pl.BlockSpec((None,) + stack.shape[1:], lambda *_: (layer,) + (0,) * nd,
                        pipeline_mode=pl.Buffered(1))


FF_CHUNK = 256


def _ffn_body(x_ref, g_ref, wgu_ref, wd_ref, o_ref, a_scr):
    x = x_ref[...]
    xn = _rms(x, g_ref[...]).astype(BF16)
    for c in range(D_FF // FF_CHUNK):
        lo = c * FF_CHUNK
        gate = jnp.dot(xn, wgu_ref[:, lo:lo + FF_CHUNK], preferred_element_type=F32)
        up = jnp.dot(xn, wgu_ref[:, D_FF + lo:D_FF + lo + FF_CHUNK], preferred_element_type=F32)
        a_scr[:, lo:lo + FF_CHUNK] = (gate * jax.nn.sigmoid(gate) * up).astype(BF16)
    o_ref[...] = x + jnp.dot(a_scr[...], wd_ref[...], preferred_element_type=F32)


def _ffn_call(x, g, wgu, wd, *, layer, tm):
    m = x.shape[0]
    return pl.pallas_call(
        _ffn_body,
        grid=(m // tm,),
        in_specs=[
            pl.BlockSpec((tm, D_MODEL), lambda i: (i, 0)),
            _const_spec((1, D_MODEL)),
            _layer_spec(wgu, layer),
            _layer_spec(wd, layer),
        ],
        out_specs=pl.BlockSpec((tm, D_MODEL), lambda i: (i, 0)),
        out_shape=jax.ShapeDtypeStruct((m, D_MODEL), F32),
        scratch_shapes=[pltpu.VMEM((tm, D_FF), BF16)],
        compiler_params=pltpu.CompilerParams(
            dimension_semantics=("arbitrary",), vmem_limit_bytes=VMEM_LIMIT),
        name="ffn",
    )(x, g, wgu, wd)


N_GTILES = D_MODEL // MXU_DIM
SLABS_PER_TILE = N_SLABS // N_GTILES


SSM_AHEAD = 1
SSM_ROW_PAD = 1


def _ssm_body(x_ref, g_ref, wb_ref, wc_ref, lam_ref, d_ref, wglu_ref, h0r_ref, h0i_ref,
              o_ref, hr_ref, hi_ref, xp_scr, *bu_scrs, bb, tc):
    pitch = tc + SSM_ROW_PAD
    spt = SLABS_PER_TILE

    @pl.when(pl.program_id(1) == 0)
    def _():
        hr_ref[...] = h0r_ref[...]
        hi_ref[...] = h0i_ref[...]

    g = g_ref[...]
    for b in range(bb):
        xp_scr[b * pitch:b * pitch + tc, :] = _rms(x_ref[b], g)
        xp_scr[b * pitch + tc:(b + 1) * pitch, :] = jnp.zeros((SSM_ROW_PAD, D_MODEL), F32)

    u = xp_scr[...]
    ub = u.astype(BF16)

    def project_in(kt):
        bu = bu_scrs[kt]
        res = jnp.dot(ub[:, kt * MXU_DIM:(kt + 1) * MXU_DIM], wb_ref[kt], preferred_element_type=F32)
        for j in range(2 * spt):
            bu[j] = res[:, j * LANES:(j + 1) * LANES]

    def scan(kt):
        bu = bu_scrs[kt]
        for rg in range(bb // SUBLANES):
            rows = slice(rg * SUBLANES, (rg + 1) * SUBLANES)
            for j in range(spt):
                s = kt * spt + j
                lanes = slice(s * LANES, (s + 1) * LANES)
                lr = lam_ref[s]
                li = lam_ref[N_SLABS + s]
                hr = hr_ref[rows, lanes]
                hi = hi_ref[rows, lanes]
                for t in range(tc):
                    idx = pl.ds(rg * SUBLANES * pitch + t, SUBLANES, stride=pitch)
                    nr = (lr * hr - li * hi) + bu[j, idx, :]
                    ni = (lr * hi + li * hr) + bu[spt + j, idx, :]
                    bu[j, idx, :] = nr
                    bu[spt + j, idx, :] = ni
                    hr, hi = nr, ni
                hr_ref[rows, lanes] = hr
                hi_ref[rows, lanes] = hi

    def project_out(kt):
        bu = bu_scrs[kt]
        hcat = jnp.concatenate([bu[j] for j in range(2 * spt)], axis=-1).astype(BF16)
        return jnp.dot(hcat, wc_ref[kt], preferred_element_type=F32)

    ys = []
    for kt in range(min(SSM_AHEAD, N_GTILES)):
        project_in(kt)
    for kt in range(N_GTILES):
        if kt + SSM_AHEAD < N_GTILES:
            project_in(kt + SSM_AHEAD)
        scan(kt)
        ys.append(project_out(kt))
    y = jnp.concatenate(ys, axis=-1) + d_ref[...] * u
    z = jax.nn.gelu(y).astype(BF16)
    gl = jnp.dot(z, wglu_ref[...], preferred_element_type=F32)
    xp_scr[...] = gl[:, :D_MODEL] * jax.nn.sigmoid(gl[:, D_MODEL:])
    for b in range(bb):
        o_ref[b] = x_ref[b] + xp_scr[b * pitch:b * pitch + tc, :]


def _ssm_call(x, g, wb, wc, lam, d, wglu, h0r, h0i, *, layer, bb, tc):
    bsz, seq, _ = x.shape
    mp = bb * (tc + SSM_ROW_PAD)
    body = functools.partial(_ssm_body, bb=bb, tc=tc)
    return pl.pallas_call(
        body,
        grid=(bsz // bb, seq // tc),
        in_specs=[
            pl.BlockSpec((bb, tc, D_MODEL), lambda i, j: (i, j, 0)),
            _const_spec((1, D_MODEL)),
            _layer_spec(wb, layer),
            _layer_spec(wc, layer),
            _layer_spec(lam, layer),
            _const_spec((1, D_MODEL)),
            _layer_spec(wglu, layer),
            pl.BlockSpec((bb, N_STATE), lambda i, j: (i, 0)),
            pl.BlockSpec((bb, N_STATE), lambda i, j: (i, 0)),
        ],
        out_specs=[
            pl.BlockSpec((bb, tc, D_MODEL), lambda i, j: (i, j, 0)),
            pl.BlockSpec((bb, N_STATE), lambda i, j: (i, 0)),
            pl.BlockSpec((bb, N_STATE), lambda i, j: (i, 0)),
        ],
        out_shape=[
            jax.ShapeDtypeStruct(x.shape, F32),
            jax.ShapeDtypeStruct((bsz, N_STATE), F32),
            jax.ShapeDtypeStruct((bsz, N_STATE), F32),
        ],
        scratch_shapes=[pltpu.VMEM((mp, D_MODEL), F32)]
        + [pltpu.VMEM((2 * SLABS_PER_TILE, mp, LANES), F32) for _ in range(N_GTILES)],
        compiler_params=pltpu.CompilerParams(
            dimension_semantics=("arbitrary", "arbitrary"), vmem_limit_bytes=VMEM_LIMIT),
        name="ssm",
    )(x, g, wb, wc, lam, d, wglu, h0r, h0i)


def _qk_norm_rope(qk, ones_bd, gain, cos, sina, sinb):
    n_cols = qk.shape[1] // LANES
    sq = qk * qk
    hi = sq.astype(BF16)
    lo = (sq - hi.astype(F32)).astype(BF16)
    outs = []
    for c in range(n_cols // 2):
        sl = slice(c * MXU_DIM, (c + 1) * MXU_DIM)
        ss = (jnp.dot(hi[:, sl], ones_bd, preferred_element_type=F32)
              + jnp.dot(lo[:, sl], ones_bd, preferred_element_type=F32))
        outs.append(qk[:, sl] * lax.rsqrt(ss * (1.0 / HEAD_DIM) + NORM_EPS) * gain[:, sl])
    qn = jnp.concatenate(outs, axis=-1)
    cols = []
    for c in range(n_cols):
        xc = qn[:, c * LANES:(c + 1) * LANES]
        cols.append(xc * cos + pltpu.roll(xc, ROT_DIM // 2, 1) * sina
                    + pltpu.roll(xc, LANES - ROT_DIM // 2, 1) * sinb)
    return jnp.concatenate(cols, axis=-1)


def _lane_halves(x):
    lane = lax.broadcasted_iota(jnp.int32, x.shape, 1)
    low = lane < HEAD_DIM
    return jnp.where(low, x, 0.0), jnp.where(low, 0.0, x)


TQ = 256
ATTN_AHEAD = 2


def _attn_prompt_body(x_ref, g_ref, wqkv_ref, ones_ref, gain_ref, cos_ref, sina_ref, sinb_ref,
                      bias_ref, sink_ref, wo_ref, o_ref, kc_ref, vc_ref,
                      qt_cur, ka_cur, kb_cur, vt_cur, ot_scr, xprev_scr, qt_nxt, ka_nxt, kb_nxt, vt_nxt,
                      ka_prev, kb_prev, vt_prev, *, blocks_per_seq):
    g = pl.program_id(0)

    @pl.when(g == 0)
    def _():
        for scr in (ka_cur, kb_cur, ka_prev, kb_prev):
            scr[...] = jnp.zeros(scr.shape, BF16)
        for scr in (vt_cur, vt_prev):
            scr[:, 0:HEAD_DIM, :] = jnp.zeros((N_KV_HEADS, HEAD_DIM, scr.shape[2]), BF16)
            scr[:, HEAD_DIM:, :] = jnp.ones((N_KV_HEADS, HEAD_DIM, scr.shape[2]), BF16)
        qt_cur[...] = jnp.zeros((D_MODEL, TQ), BF16)
        xprev_scr[...] = jnp.zeros((TQ, D_MODEL), F32)

    bias_mid = jnp.concatenate([bias_ref[...], bias_ref[...]], axis=1)
    first = (g - 1) % blocks_per_seq == 0
    key_row = lax.broadcasted_iota(jnp.int32, bias_mid.shape, 0)
    bias_first = jnp.where(jnp.logical_and(first, key_row < WINDOW), NEG_INF, bias_mid)
    lane = lax.broadcasted_iota(jnp.int32, (1, 2 * WINDOW), 1)
    pair = 2 * HEAD_DIM

    def key_window(j, h, prev, cur):
        if j == 0:
            return [prev[h], cur[h, 0:WINDOW, :]]
        return [cur[h, (j - 1) * WINDOW:(j + 1) * WINDOW, :]]

    def value_window(j, h):
        if j == 0:
            return jnp.concatenate([vt_prev[h], vt_cur[h, :, 0:WINDOW]], axis=1)
        return vt_cur[h, :, (j - 1) * WINDOW:(j + 1) * WINDOW]

    def scores(j, h):
        qpos = slice(j * WINDOW, (j + 1) * WINDOW)
        qs = jnp.concatenate([qt_cur[(2 * h) * pair:(2 * h + 1) * pair, qpos],
                              qt_cur[(2 * h + 1) * pair:(2 * h + 2) * pair, qpos]], axis=1)
        kk = jnp.concatenate(key_window(j, h, ka_prev, ka_cur) + key_window(j, h, kb_prev, kb_cur),
                             axis=0)
        return jnp.dot(kk, qs, preferred_element_type=F32)

    def softmax(j, h, st):
        bias = bias_first if j == 0 else bias_mid
        out = []
        for par in range(2):
            s = st[par * 2 * WINDOW:(par + 1) * 2 * WINDOW, :] + bias
            sink = jnp.where(lane < WINDOW, sink_ref[4 * h + par], sink_ref[4 * h + 2 + par])
            m = jnp.maximum(jnp.max(s, axis=0, keepdims=True), sink)
            out.append((jnp.exp(s - m).astype(BF16), jnp.exp(sink - m)))
        return out

    def weighted_values(j, h, probs):
        qpos = slice(j * WINDOW, (j + 1) * WINDOW)
        vth = value_window(j, h)
        for par, (p, sink_term) in enumerate(probs):
            nd = jnp.dot(vth, p, preferred_element_type=F32)
            denom = nd[HEAD_DIM:, :] + sink_term
            on = nd[:HEAD_DIM, :] * (1.0 / denom)
            for rp in range(2):
                head = 4 * h + 2 * rp + par
                ot_scr[head * HEAD_DIM:(head + 1) * HEAD_DIM, qpos] = on[:, rp * WINDOW:(rp + 1) * WINDOW]

    x = x_ref[0]
    xn = _rms(x, g_ref[...]).astype(BF16)
    tabs = (cos_ref[...], sina_ref[...], sinb_ref[...])

    def project(c):
        cols = slice(c * MXU_DIM, (c + 1) * MXU_DIM)
        t = jnp.dot(xn, wqkv_ref[:, cols], preferred_element_type=F32)
        if c * MXU_DIM < D_QK:
            t = _qk_norm_rope(t, ones_ref[...], gain_ref[:, cols], *tabs)
        if c * MXU_DIM < D_MODEL:
            qt_nxt[cols, :] = t.T.astype(BF16)
        elif c * MXU_DIM < D_QK:
            kc_ref[0] = t[TQ - WINDOW:, :]
            for h in range(N_KV_HEADS):
                lo_part, hi_part = _lane_halves(t[:, (h // 2) * LANES:(h // 2 + 1) * LANES])
                if h % 2 == 0:
                    a, b = lo_part, pltpu.roll(lo_part, HEAD_DIM, 1)
                else:
                    a, b = pltpu.roll(hi_part, HEAD_DIM, 1), hi_part
                ka_nxt[h] = a.astype(BF16)
                kb_nxt[h] = b.astype(BF16)
        else:
            vc_ref[0] = t[TQ - WINDOW:, :]
            vt = t.T
            for h in range(N_KV_HEADS):
                vt_nxt[h] = vt[h * HEAD_DIM:(h + 1) * HEAD_DIM, :].astype(BF16)

    groups = [(j, h) for j in range(TQ // WINDOW) for h in range(N_KV_HEADS)]
    n_groups = len(groups)
    st, probs = {}, {}
    for n in range(n_groups + ATTN_AHEAD):
        if n < n_groups:
            st[n] = scores(*groups[n])
        if n < D_QKV // MXU_DIM:
            project(n)
        d = n - ATTN_AHEAD + 1
        if 0 <= d < n_groups:
            probs[d] = softmax(*groups[d], st.pop(d))
        d = n - ATTN_AHEAD
        if 0 <= d < n_groups:
            weighted_values(*groups[d], probs.pop(d))
    attn = ot_scr[...].T.astype(BF16)
    o_ref[0] = xprev_scr[...] + jnp.dot(attn, wo_ref[...], preferred_element_type=F32)

    xprev_scr[...] = x
    qt_cur[...] = qt_nxt[...]
    for prev, cur, nxt in ((ka_prev, ka_cur, ka_nxt), (kb_prev, kb_cur, kb_nxt)):
        prev[...] = cur[:, TQ - WINDOW:, :]
        cur[...] = nxt[...]
    vt_prev[:, 0:HEAD_DIM, :] = vt_cur[:, 0:HEAD_DIM, TQ - WINDOW:]
    vt_cur[:, 0:HEAD_DIM, :] = vt_nxt[...]


def _attn_prompt_call(x, g, wqkv, ones_bd, gain, cos, sina, sinb, bias, sinks, wo, *, layer):
    bsz, seq, _ = x.shape
    bps = seq // TQ
    n_blocks = bsz * bps
    k_blk = pltpu.VMEM((N_KV_HEADS, TQ, LANES), BF16)
    k_tail = pltpu.VMEM((N_KV_HEADS, WINDOW, LANES), BF16)
    vt_blk = pltpu.VMEM((N_KV_HEADS, 2 * HEAD_DIM, TQ), BF16)
    vt_tail = pltpu.VMEM((N_KV_HEADS, 2 * HEAD_DIM, WINDOW), BF16)

    def proj_block(gi):
        blk = jnp.minimum(gi, n_blocks - 1)
        return blk // bps, blk % bps

    def attn_block(gi):
        blk = jnp.maximum(gi - 1, 0)
        return blk // bps, blk % bps

    tab_spec = pl.BlockSpec((TQ, LANES), lambda gi: (proj_block(gi)[1], 0))
    cache_spec = pl.BlockSpec((1, WINDOW, D_KV), lambda gi: (proj_block(gi)[0], 0, 0))
    body = functools.partial(_attn_prompt_body, blocks_per_seq=bps)
    return pl.pallas_call(
        body,
        grid=(n_blocks + 1,),
        in_specs=[
            pl.BlockSpec((1, TQ, D_MODEL), lambda gi: (*proj_block(gi), 0)),
            _const_spec((1, D_MODEL)),
            _layer_spec(wqkv, layer),
            _const_spec((MXU_DIM, MXU_DIM)),
            _const_spec((1, D_QK)),
            tab_spec, tab_spec, tab_spec,
            _const_spec((2 * WINDOW, WINDOW)),
            pl.BlockSpec(memory_space=pltpu.SMEM),
            _layer_spec(wo, layer),
        ],
        out_specs=[
            pl.BlockSpec((1, TQ, D_MODEL), lambda gi: (*attn_block(gi), 0)),
            cache_spec, cache_spec,
        ],
        out_shape=[
            jax.ShapeDtypeStruct(x.shape, F32),
            jax.ShapeDtypeStruct((bsz, WINDOW, D_KV), F32),
            jax.ShapeDtypeStruct((bsz, WINDOW, D_KV), F32),
        ],
        scratch_shapes=[pltpu.VMEM((D_MODEL, TQ), BF16), k_blk, k_blk, vt_blk,
                        pltpu.VMEM((D_MODEL, TQ), F32),
                        pltpu.VMEM((TQ, D_MODEL), F32),
                        pltpu.VMEM((D_MODEL, TQ), BF16), k_blk, k_blk,
                        pltpu.VMEM((N_KV_HEADS, HEAD_DIM, TQ), BF16),
                        k_tail, k_tail, vt_tail],
        compiler_params=pltpu.CompilerParams(
            dimension_semantics=("arbitrary",), vmem_limit_bytes=VMEM_LIMIT),
        name="attn_prompt",
    )(x, g, wqkv, ones_bd, gain, cos, sina, sinb, bias, sinks, wo)


SEQ_BLOCK = 16
SAMPLE_AHEAD = 4
N_KEYS_PAD = 2 * WINDOW


def _attn_sample_body(x_ref, ck_ref, cv_ref, g_ref, wqkv_ref, ones_ref, gain_ref, cos_ref, sina_ref,
                      sinb_ref, bias_ref, sink_ref, wo_ref, o_ref, kc_ref, vc_ref,
                      kf_scr, vf_scr, oall_scr, *, s_len):
    rows = SEQ_BLOCK * s_len
    w_keep = WINDOW - s_len

    @pl.when(pl.program_id(0) == 0)
    def _():
        pad = (SEQ_BLOCK, N_KEYS_PAD - WINDOW, D_KV)
        kf_scr[:, WINDOW:, :] = jnp.zeros(pad, BF16)
        vf_scr[:, WINDOW:, :] = jnp.zeros(pad, BF16)

    x = x_ref[...].reshape(rows, D_MODEL)
    xn = _rms(x, g_ref[...]).astype(BF16)
    qkv = jnp.dot(xn, wqkv_ref[...], preferred_element_type=F32)
    qk = _qk_norm_rope(qkv[:, :D_QK], ones_ref[...], gain_ref[...],
                       cos_ref[...], sina_ref[...], sinb_ref[...])
    q = qk[:, :D_MODEL]
    k_new = qk[:, D_MODEL:].reshape(SEQ_BLOCK, s_len, D_KV)
    v_new = qkv[:, D_QK:].reshape(SEQ_BLOCK, s_len, D_KV)

    ck = ck_ref[...]
    cv = cv_ref[...]
    kc_ref[:, 0:w_keep, :] = ck[:, s_len:, :]
    vc_ref[:, 0:w_keep, :] = cv[:, s_len:, :]
    kc_ref[:, w_keep:, :] = k_new
    vc_ref[:, w_keep:, :] = v_new
    kf_scr[:, 0:WINDOW, :] = ck.astype(BF16)
    vf_scr[:, 0:WINDOW, :] = cv.astype(BF16)
    tile = 2 * SUBLANES
    zpad = jnp.zeros((SEQ_BLOCK, tile - s_len, D_KV), F32)
    kf_scr[:, WINDOW:WINDOW + tile, :] = jnp.concatenate([k_new, zpad], axis=1).astype(BF16)
    vf_scr[:, WINDOW:WINDOW + tile, :] = jnp.concatenate([v_new, zpad], axis=1).astype(BF16)

    bias = bias_ref[...]
    sink = sink_ref[...]
    nt_dims = (((1,), (1,)), ((), ()))
    zero_col = jnp.zeros((s_len, LANES), F32)
    def scores(s):
        qs = q[s * s_len:(s + 1) * s_len, :]
        tiles = []
        for h in range(N_KV_HEADS):
            for r in range(KV_REP):
                head = KV_REP * h + r
                src = qs[:, (head // 2) * LANES:(head // 2 + 1) * LANES]
                lo_part, hi_part = _lane_halves(src)
                part = lo_part if head % 2 == 0 else hi_part
                if head % 2 != h % 2:
                    part = pltpu.roll(part, HEAD_DIM, 1)
                tiles.append(jnp.concatenate([part, zero_col] if h < 2 else [zero_col, part], axis=1))
        wt = jnp.concatenate(tiles, axis=0).astype(BF16)
        return lax.dot_general(wt, kf_scr[s], nt_dims, preferred_element_type=F32)

    def softmax(sc):
        sc = sc + bias
        m = jnp.maximum(jnp.max(sc, axis=-1, keepdims=True), sink)
        p = jnp.exp(sc - m)
        denom = jnp.sum(p, axis=-1, keepdims=True) + jnp.exp(sink - m)
        return p.astype(BF16), 1.0 / denom

    def weighted_values(s, p, inv_denom):
        o = jnp.dot(p, vf_scr[s], preferred_element_type=F32) * inv_denom
        cols = []
        for c in range(N_HEADS // 2):
            pieces = []
            for head in (2 * c, 2 * c + 1):
                h, r = divmod(head, KV_REP)
                blk = o[(KV_REP * h + r) * s_len:(KV_REP * h + r + 1) * s_len,
                        (h // 2) * LANES:(h // 2 + 1) * LANES]
                lo_part, hi_part = _lane_halves(blk)
                part = lo_part if h % 2 == 0 else hi_part
                if h % 2 != head % 2:
                    part = pltpu.roll(part, HEAD_DIM, 1)
                pieces.append(part)
            cols.append(pieces[0] + pieces[1])
        oall_scr[s * s_len:(s + 1) * s_len, :] = jnp.concatenate(cols, axis=-1)

    sc, probs = {}, {}
    for n in range(SEQ_BLOCK + SAMPLE_AHEAD):
        if n < SEQ_BLOCK:
            sc[n] = scores(n)
        d = n - SAMPLE_AHEAD // 2
        if 0 <= d < SEQ_BLOCK:
            probs[d] = softmax(sc.pop(d))
        d = n - SAMPLE_AHEAD
        if 0 <= d < SEQ_BLOCK:
            weighted_values(d, *probs.pop(d))

    out = jnp.dot(oall_scr[...].astype(BF16), wo_ref[...], preferred_element_type=F32)
    o_ref[...] = (x + out).reshape(SEQ_BLOCK, s_len, D_MODEL)


def _attn_sample_call(x, ck, cv, g, wqkv, ones_bd, gain, cos, sina, sinb, bias, sink_rows, wo, *, layer):
    bsz, s_len, _ = x.shape
    rows = SEQ_BLOCK * s_len
    body = functools.partial(_attn_sample_body, s_len=s_len)
    seq_spec = lambda shape: pl.BlockSpec(shape, lambda i: (i, 0, 0))
    cache_spec = pl.BlockSpec((None, SEQ_BLOCK, WINDOW, D_KV), lambda i: (layer, i, 0, 0))
    return pl.pallas_call(
        body,
        grid=(bsz // SEQ_BLOCK,),
        in_specs=[
            seq_spec((SEQ_BLOCK, s_len, D_MODEL)),
            cache_spec,
            cache_spec,
            _const_spec((1, D_MODEL)),
            _layer_spec(wqkv, layer),
            _const_spec((MXU_DIM, MXU_DIM)),
            _const_spec((1, D_QK)),
            _const_spec((rows, LANES)),
            _const_spec((rows, LANES)),
            _const_spec((rows, LANES)),
            _const_spec((N_HEADS * s_len, N_KEYS_PAD)),
            _const_spec((N_HEADS * s_len, 1)),
            _layer_spec(wo, layer),
        ],
        out_specs=[
            seq_spec((SEQ_BLOCK, s_len, D_MODEL)),
            seq_spec((SEQ_BLOCK, WINDOW, D_KV)),
            seq_spec((SEQ_BLOCK, WINDOW, D_KV)),
        ],
        out_shape=[
            jax.ShapeDtypeStruct(x.shape, F32),
            jax.ShapeDtypeStruct(ck.shape[1:], F32),
            jax.ShapeDtypeStruct(cv.shape[1:], F32),
        ],
        scratch_shapes=[
            pltpu.VMEM((SEQ_BLOCK, N_KEYS_PAD, D_KV), BF16),
            pltpu.VMEM((SEQ_BLOCK, N_KEYS_PAD, D_KV), BF16),
            pltpu.VMEM((rows, D_MODEL), F32),
        ],
        compiler_params=pltpu.CompilerParams(
            dimension_semantics=("arbitrary",), vmem_limit_bytes=VMEM_LIMIT),
        name="attn_sample",
    )(x, ck, cv, g, wqkv, ones_bd, gain, cos, sina, sinb, bias, sink_rows, wo)


def _ssm_weights(a_re, a_im, log_dt, b_re, b_im, c_re, c_im):
    nl = a_re.shape[0]
    dt = jnp.exp(log_dt)[..., None]
    mag = jnp.exp(a_re * dt)
    lam_re = mag * jnp.cos(a_im * dt)
    lam_im = mag * jnp.sin(a_im * dt)
    den = a_re * a_re + a_im * a_im
    nr = lam_re - 1.0
    ni = lam_im
    f_re = (nr * a_re + ni * a_im) / den
    f_im = (ni * a_re - nr * a_im) / den
    bb_re = f_re[..., None] * b_re - f_im[..., None] * b_im
    bb_im = f_re[..., None] * b_im + f_im[..., None] * b_re
    gpt = MXU_DIM // SSM_GROUP
    n_st = gpt * STATE_DIM
    same_group = jnp.asarray(
        (np.arange(MXU_DIM)[:, None] // SSM_GROUP) == (np.arange(n_st)[None, :] // STATE_DIM))

    def b_tiles(bb):
        t = jnp.swapaxes(bb, 2, 3).reshape(nl, N_GTILES, MXU_DIM, STATE_DIM)
        return jnp.where(same_group, jnp.tile(t, (1, 1, 1, gpt)), 0.0)

    def c_tiles(cc):
        t = jnp.swapaxes(cc, 2, 3).reshape(nl, N_GTILES, n_st, SSM_GROUP)
        return jnp.where(same_group.T, jnp.tile(t, (1, 1, 1, gpt)), 0.0)

    wb = jnp.concatenate([b_tiles(bb_re), b_tiles(bb_im)], axis=-1).astype(BF16)
    wc = jnp.concatenate([c_tiles(c_re), -c_tiles(c_im)], axis=2).astype(BF16)
    lam = jnp.concatenate([lam_re.reshape(nl, N_SLABS, 1, LANES), lam_im.reshape(nl, N_SLABS, 1, LANES)], axis=1)
    lam = jnp.broadcast_to(lam, (nl, 2 * N_SLABS, SUBLANES, LANES))
    return wb, wc, lam


def _rope_tables(pos):
    half = ROT_DIM // 2
    inv_freq = ROPE_THETA ** (-jnp.arange(half, dtype=F32) * 2.0 / ROT_DIM)
    ang = pos[:, None] * inv_freq[None, :]
    cos, sin = jnp.cos(ang), jnp.sin(ang)
    n = pos.shape[0]
    ones = jnp.ones((n, HEAD_DIM - ROT_DIM), F32)
    zeros = jnp.zeros((n, HEAD_DIM - ROT_DIM), F32)
    zh = jnp.zeros((n, half), F32)
    cos_h = jnp.concatenate([cos, cos, ones], axis=1)
    sina_h = jnp.concatenate([zh, sin, zeros], axis=1)
    sinb_h = jnp.concatenate([-sin, zh, zeros], axis=1)
    return tuple(jnp.concatenate([t, t], axis=1) for t in (cos_h, sina_h, sinb_h))


def _ones_block_diag():
    idx = np.arange(MXU_DIM) // HEAD_DIM
    return jnp.asarray(idx[:, None] == idx[None, :], dtype=BF16)


def _prompt_bias():
    js = np.arange(2 * WINDOW)[:, None]
    iq = np.arange(WINDOW)[None, :]
    diff = WINDOW + iq - js
    return jnp.asarray(np.where((diff >= 0) & (diff < WINDOW), 0.0, NEG_INF), dtype=F32)


def _sample_bias(s_len):
    t = (np.arange(N_HEADS * s_len) % s_len)[:, None]
    key = np.arange(N_KEYS_PAD)[None, :]
    diff = np.where(key < WINDOW, WINDOW + t - key, t - (key - WINDOW))
    ok = (diff >= 0) & (diff < WINDOW) & (key < WINDOW + s_len)
    return jnp.asarray(np.where(ok, 0.0, NEG_INF), dtype=F32)


SSM_PROMPT_TC = 64
SSM_SAMPLE_BB = 32
FFN_TM = 512


def kernel(x_prompt, x_sample, state_ssm_re, state_ssm_im, cache_swa_k, cache_swa_v, norm_mix, norm_ffn, ssm_a_re, ssm_a_im, ssm_log_dt, ssm_b_re, ssm_b_im, ssm_c_re, ssm_c_im, ssm_d, ssm_w_glu, attn_w_qkv, attn_q_norm, attn_k_norm, attn_sinks, attn_w_o, ffn_w_gate_up, ffn_w_down):
    bsz, seq, _ = x_prompt.shape
    dbsz, dseq, _ = x_sample.shape
    yp, ys = x_prompt, x_sample
    p_re, p_im, p_k, p_v, s_re, s_im, s_k, s_v = ([] for _ in range(8))

    ones_bd = _ones_block_diag()
    p_tabs = _rope_tables(jnp.arange(seq, dtype=F32))
    s_pos = PAST_LEN + jnp.arange(dseq, dtype=F32)
    s_tabs = tuple(jnp.tile(t, (SEQ_BLOCK, 1)) for t in _rope_tables(s_pos))
    p_bias = _prompt_bias()
    s_bias = _sample_bias(dseq)
    h0 = jnp.zeros((bsz, N_STATE), F32)

    wb, wc, lam = _ssm_weights(ssm_a_re, ssm_a_im, ssm_log_dt, ssm_b_re, ssm_b_im, ssm_c_re, ssm_c_im)
    wglu = ssm_w_glu.astype(BF16)
    wqkv = attn_w_qkv.astype(BF16)
    wo = attn_w_o.astype(BF16)
    wgu = ffn_w_gate_up.astype(BF16)
    wd = ffn_w_down.astype(BF16)
    n_swa = cache_swa_k.shape[0]
    ck_all = cache_swa_k.reshape(n_swa, dbsz, WINDOW, D_KV)
    cv_all = cache_swa_v.reshape(n_swa, dbsz, WINDOW, D_KV)

    for i in range(DEPTH):
        j = i // 2
        g_mix = norm_mix[i].reshape(1, D_MODEL)
        if i % 2 == 0:
            d = ssm_d[j].reshape(1, D_MODEL)
            yp, hr, hi = _ssm_call(yp, g_mix, wb, wc, lam, d, wglu, h0, h0,
                                   layer=j, bb=bsz, tc=SSM_PROMPT_TC)
            ys, sr, si = _ssm_call(ys, g_mix, wb, wc, lam, d, wglu,
                                   state_ssm_re[j].reshape(dbsz, N_STATE), state_ssm_im[j].reshape(dbsz, N_STATE),
                                   layer=j, bb=SSM_SAMPLE_BB, tc=dseq)
            shp = (N_GROUPS, STATE_DIM)
            p_re.append(hr.reshape(bsz, *shp))
            p_im.append(hi.reshape(bsz, *shp))
            s_re.append(sr.reshape(dbsz, *shp))
            s_im.append(si.reshape(dbsz, *shp))
        else:
            gain = jnp.concatenate([jnp.tile(attn_q_norm[j] * ATTN_SCALE, N_HEADS),
                                    jnp.tile(attn_k_norm[j], N_KV_HEADS)]).reshape(1, D_QK)
            sinks = attn_sinks[j]
            yp, kp, vp = _attn_prompt_call(yp, g_mix, wqkv, ones_bd, gain, *p_tabs, p_bias, sinks, wo, layer=j)
            sink_rows = jnp.repeat(sinks, dseq).reshape(N_HEADS * dseq, 1)
            ys, ks_, vs_ = _attn_sample_call(ys, ck_all, cv_all, g_mix, wqkv, ones_bd, gain, *s_tabs, s_bias,
                                             sink_rows, wo, layer=j)
            kv_shape = (WINDOW, N_KV_HEADS, HEAD_DIM)
            p_k.append(kp.reshape(bsz, *kv_shape))
            p_v.append(vp.reshape(bsz, *kv_shape))
            s_k.append(ks_.reshape(dbsz, *kv_shape))
            s_v.append(vs_.reshape(dbsz, *kv_shape))
        g_ffn = norm_ffn[i].reshape(1, D_MODEL)
        yp = _ffn_call(yp.reshape(bsz * seq, D_MODEL), g_ffn, wgu, wd, layer=i, tm=FFN_TM).reshape(bsz, seq, D_MODEL)
        ys = _ffn_call(ys.reshape(dbsz * dseq, D_MODEL), g_ffn, wgu, wd, layer=i, tm=FFN_TM).reshape(dbsz, dseq, D_MODEL)

    return (yp, ys, jnp.stack(p_re), jnp.stack(p_im), jnp.stack(p_k), jnp.stack(p_v),
            jnp.stack(s_re), jnp.stack(s_im), jnp.stack(s_k), jnp.stack(s_v))
```

```python
import functools

import jax
import jax.numpy as jnp
import numpy as np
from jax import lax
from jax.experimental import pallas as pl
from jax.experimental.pallas import tpu as pltpu

F32 = jnp.float32
BF16 = jnp.bfloat16

D_MODEL = 1024
DEPTH = 4
PAST_LEN = 8192
SSM_GROUP = 16
N_GROUPS = D_MODEL // SSM_GROUP
STATE_DIM = 64
N_STATE = N_GROUPS * STATE_DIM
HEAD_DIM = 64
N_HEADS = D_MODEL // HEAD_DIM
N_KV_HEADS = 4
KV_REP = N_HEADS // N_KV_HEADS
D_KV = N_KV_HEADS * HEAD_DIM
D_QK = D_MODEL + D_KV
D_QKV = D_MODEL + 2 * D_KV
WINDOW = 128
ROT_DIM = HEAD_DIM // 4
ROPE_THETA = 500000.0
ATTN_SCALE = HEAD_DIM ** -0.5
D_FF = 2816
NORM_EPS = 1e-6
NEG_INF = -1e30

LANES = 128
SUBLANES = 8
MXU_DIM = 256
N_SLABS = N_STATE // LANES
VMEM_LIMIT = 56 * 1024 * 1024


def _rms(x, g):
    ms = jnp.mean(x * x, axis=-1, keepdims=True)
    return x * lax.rsqrt(ms + NORM_EPS) * g


def _const_spec(shape):
    nd = len(shape)
    return pl.BlockSpec(shape, lambda *_: (0,) * nd, pipeline_mode=pl.Buffered(1))


def _layer_spec(stack, layer):
    nd = stack.ndim - 1
    return pl.BlockSpec((None,) + stack.shape[1:], lambda *_: (layer,) + (0,) * nd,
                        pipeline_mode=pl.Buffered(1))


FF_CHUNK = 256


def _ffn_body(x_ref, g_ref, wgu_ref, wd_ref, o_ref, a_scr):
    x = x_ref[...]
    xn = _rms(x, g_ref[...]).astype(BF16)
    for c in range(D_FF // FF_CHUNK):
        lo = c * FF_CHUNK
        gate = jnp.dot(xn, wgu_ref[:, lo:lo + FF_CHUNK], preferred_element_type=F32)
        up = jnp.dot(xn, wgu_ref[:, D_FF + lo:D_FF + lo + FF_CHUNK], preferred_element_type=F32)
        a_scr[:, lo:lo + FF_CHUNK] = (gate * jax.nn.sigmoid(gate) * up).astype(BF16)
    o_ref[...] = x + jnp.dot(a_scr[...], wd_ref[...], preferred_element_type=F32)


def _ffn_call(x, g, wgu, wd, *, layer, tm):
    m = x.shape[0]
    return pl.pallas_call(
        _ffn_body,
        grid=(m // tm,),
        in_specs=[
            pl.BlockSpec((tm, D_MODEL), lambda i: (i, 0)),
            _const_spec((1, D_MODEL)),
            _layer_spec(wgu, layer),
            _layer_spec(wd, layer),
        ],
        out_specs=pl.BlockSpec((tm, D_MODEL), lambda i: (i, 0)),
        out_shape=jax.ShapeDtypeStruct((m, D_MODEL), F32),
        scratch_shapes=[pltpu.VMEM((tm, D_FF), BF16)],
        compiler_params=pltpu.CompilerParams(
            dimension_semantics=("arbitrary",), vmem_limit_bytes=VMEM_LIMIT),
        name="ffn",
    )(x, g, wgu, wd)


N_GTILES = D_MODEL // MXU_DIM
SLABS_PER_TILE = N_SLABS // N_GTILES


SSM_AHEAD = 1
SSM_ROW_PAD = 1


def _ssm_body(x_ref, g_ref, wb_ref, wc_ref, lam_ref, d_ref, wglu_ref, h0r_ref, h0i_ref,
              o_ref, hr_ref, hi_ref, xs_scr, os_scr, *, bb, tc):
    pitch = tc + SSM_ROW_PAD
    spt = SLABS_PER_TILE
    n_cols = D_MODEL // LANES
    n_rg = bb // SUBLANES

    @pl.when(pl.program_id(1) == 0)
    def _():
        hr_ref[...] = h0r_ref[...]
        hi_ref[...] = h0i_ref[...]

    def gather_idx(rg, t):
        return pl.ds(rg * SUBLANES * pitch + t, SUBLANES, stride=pitch)

    g = g_ref[...]
    for b in range(bb):
        xn = _rms(x_ref[b], g)
        for c in range(n_cols):
            xs_scr[c, b * pitch:b * pitch + tc, :] = xn[:, c * LANES:(c + 1) * LANES]
    u = jnp.concatenate(
        [jnp.concatenate([xs_scr[c, gather_idx(rg, t), :] for c in range(n_cols)], axis=1)
         for rg in range(n_rg) for t in range(tc)], axis=0)
    ub = u.astype(BF16)

    def project_in(kt):
        return jnp.dot(ub[:, kt * MXU_DIM:(kt + 1) * MXU_DIM], wb_ref[kt], preferred_element_type=F32)

    def scan(kt, bu):
        re_cols, im_cols = [], []
        for j in range(spt):
            s = kt * spt + j
            lanes = slice(s * LANES, (s + 1) * LANES)
            lr = lam_ref[s]
            li = lam_ref[N_SLABS + s]
            re_rows, im_rows = [], []
            for rg in range(n_rg):
                seqs = slice(rg * SUBLANES, (rg + 1) * SUBLANES)
                hr = hr_ref[seqs, lanes]
                hi = hi_ref[seqs, lanes]
                for t in range(tc):
                    rows = slice((rg * tc + t) * SUBLANES, (rg * tc + t + 1) * SUBLANES)
                    nr = (lr * hr - li * hi) + bu[rows, (2 * j) * LANES:(2 * j + 1) * LANES]
                    ni = (lr * hi + li * hr) + bu[rows, (2 * j + 1) * LANES:(2 * j + 2) * LANES]
                    hr, hi = nr, ni
                    re_rows.append(nr)
                    im_rows.append(ni)
                hr_ref[seqs, lanes] = hr
                hi_ref[seqs, lanes] = hi
            re_cols.append(jnp.concatenate(re_rows, axis=0))
            im_cols.append(jnp.concatenate(im_rows, axis=0))
        return jnp.concatenate(re_cols + im_cols, axis=1).astype(BF16)

    ys = []
    bu = {kt: project_in(kt) for kt in range(min(SSM_AHEAD, N_GTILES))}
    for kt in range(N_GTILES):
        if kt + SSM_AHEAD < N_GTILES:
            bu[kt + SSM_AHEAD] = project_in(kt + SSM_AHEAD)
        hs = scan(kt, bu.pop(kt))
        ys.append(jnp.dot(hs, wc_ref[kt], preferred_element_type=F32))
    y = jnp.concatenate(ys, axis=-1) + d_ref[...] * u
    z = jax.nn.gelu(y).astype(BF16)
    gl = jnp.dot(z, wglu_ref[...], preferred_element_type=F32)
    out = gl[:, :D_MODEL] * jax.nn.sigmoid(gl[:, D_MODEL:])
    for rg in range(n_rg):
        for t in range(tc):
            rows = slice((rg * tc + t) * SUBLANES, (rg * tc + t + 1) * SUBLANES)
            for c in range(n_cols):
                os_scr[c, gather_idx(rg, t), :] = out[rows, c * LANES:(c + 1) * LANES]
    for b in range(bb):
        o_ref[b] = x_ref[b] + jnp.concatenate(
            [os_scr[c, b * pitch:b * pitch + tc, :] for c in range(n_cols)], axis=1)


def _ssm_call(x, g, wb, wc, lam, d, wglu, h0r, h0i, *, layer, bb, tc):
    bsz, seq, _ = x.shape
    mp = bb * (tc + SSM_ROW_PAD)
    body = functools.partial(_ssm_body, bb=bb, tc=tc)
    return pl.pallas_call(
        body,
        grid=(bsz // bb, seq // tc),
        in_specs=[
            pl.BlockSpec((bb, tc, D_MODEL), lambda i, j: (i, j, 0)),
            _const_spec((1, D_MODEL)),
            _layer_spec(wb, layer),
            _layer_spec(wc, layer),
            _layer_spec(lam, layer),
            _const_spec((1, D_MODEL)),
            _layer_spec(wglu, layer),
            pl.BlockSpec((bb, N_STATE), lambda i, j: (i, 0)),
            pl.BlockSpec((bb, N_STATE), lambda i, j: (i, 0)),
        ],
        out_specs=[
            pl.BlockSpec((bb, tc, D_MODEL), lambda i, j: (i, j, 0)),
            pl.BlockSpec((bb, N_STATE), lambda i, j: (i, 0)),
            pl.BlockSpec((bb, N_STATE), lambda i, j: (i, 0)),
        ],
        out_shape=[
            jax.ShapeDtypeStruct(x.shape, F32),
            jax.ShapeDtypeStruct((bsz, N_STATE), F32),
            jax.ShapeDtypeStruct((bsz, N_STATE), F32),
        ],
        scratch_shapes=[pltpu.VMEM((D_MODEL // LANES, mp, LANES), F32),
                        pltpu.VMEM((D_MODEL // LANES, mp, LANES), F32)],
        compiler_params=pltpu.CompilerParams(
            dimension_semantics=("arbitrary", "arbitrary"), vmem_limit_bytes=VMEM_LIMIT),
        name="ssm",
    )(x, g, wb, wc, lam, d, wglu, h0r, h0i)


def _qk_norm_rope(qk, ones_bd, gain, cos, sina, sinb):
    n_cols = qk.shape[1] // LANES
    sq = qk * qk
    hi = sq.astype(BF16)
    lo = (sq - hi.astype(F32)).astype(BF16)
    outs = []
    for c in range(n_cols // 2):
        sl = slice(c * MXU_DIM, (c + 1) * MXU_DIM)
        ss = (jnp.dot(hi[:, sl], ones_bd, preferred_element_type=F32)
              + jnp.dot(lo[:, sl], ones_bd, preferred_element_type=F32))
        outs.append(qk[:, sl] * lax.rsqrt(ss * (1.0 / HEAD_DIM) + NORM_EPS) * gain[:, sl])
    qn = jnp.concatenate(outs, axis=-1)
    cols = []
    for c in range(n_cols):
        xc = qn[:, c * LANES:(c + 1) * LANES]
        cols.append(xc * cos + pltpu.roll(xc, ROT_DIM // 2, 1) * sina
                    + pltpu.roll(xc, LANES - ROT_DIM // 2, 1) * sinb)
    return jnp.concatenate(cols, axis=-1)


def _lane_halves(x):
    lane = lax.broadcasted_iota(jnp.int32, x.shape, 1)
    low = lane < HEAD_DIM
    return jnp.where(low, x, 0.0), jnp.where(low, 0.0, x)


TQ = 256
ATTN_AHEAD = 2


def _attn_prompt_body(x_ref, g_ref, wqkv_ref, ones_ref, gain_ref, cos_ref, sina_ref, sinb_ref,
                      bias_ref, sink_ref, wo_ref, o_ref, kc_ref, vc_ref,
                      qt_cur, ka_cur, kb_cur, vt_cur, ot_scr, xprev_scr, qt_nxt, ka_nxt, kb_nxt, vt_nxt,
                      ka_prev, kb_prev, vt_prev, *, blocks_per_seq):
    g = pl.program_id(0)

    @pl.when(g == 0)
    def _():
        for scr in (ka_cur, kb_cur, ka_prev, kb_prev):
            scr[...] = jnp.zeros(scr.shape, BF16)
        for scr in (vt_cur, vt_prev):
            scr[:, 0:HEAD_DIM, :] = jnp.zeros((N_KV_HEADS, HEAD_DIM, scr.shape[2]), BF16)
            scr[:, HEAD_DIM:, :] = jnp.ones((N_KV_HEADS, HEAD_DIM, scr.shape[2]), BF16)
        qt_cur[...] = jnp.zeros((D_MODEL, TQ), BF16)
        xprev_scr[...] = jnp.zeros((TQ, D_MODEL), F32)

    bias_mid = jnp.concatenate([bias_ref[...], bias_ref[...]], axis=1)
    first = (g - 1) % blocks_per_seq == 0
    key_row = lax.broadcasted_iota(jnp.int32, bias_mid.shape, 0)
    bias_first = jnp.where(jnp.logical_and(first, key_row < WINDOW), NEG_INF, bias_mid)
    lane = lax.broadcasted_iota(jnp.int32, (1, 2 * WINDOW), 1)
    pair = 2 * HEAD_DIM

    def key_window(j, h, prev, cur):
        if j == 0:
            return [prev[h], cur[h, 0:WINDOW, :]]
        return [cur[h, (j - 1) * WINDOW:(j + 1) * WINDOW, :]]

    def value_window(j, h):
        if j == 0:
            return jnp.concatenate([vt_prev[h], vt_cur[h, :, 0:WINDOW]], axis=1)
        return vt_cur[h, :, (j - 1) * WINDOW:(j + 1) * WINDOW]

    def scores(j, h):
        qpos = slice(j * WINDOW, (j + 1) * WINDOW)
        qs = jnp.concatenate([qt_cur[(2 * h) * pair:(2 * h + 1) * pair, qpos],
                              qt_cur[(2 * h + 1) * pair:(2 * h + 2) * pair, qpos]], axis=1)
        kk = jnp.concatenate(key_window(j, h, ka_prev, ka_cur) + key_window(j, h, kb_prev, kb_cur),
                             axis=0)
        return jnp.dot(kk, qs, preferred_element_type=F32)

    def softmax(j, h, st):
        bias = bias_first if j == 0 else bias_mid
        out = []
        for par in range(2):
            s = st[par * 2 * WINDOW:(par + 1) * 2 * WINDOW, :] + bias
            sink = jnp.where(lane < WINDOW, sink_ref[4 * h + par], sink_ref[4 * h + 2 + par])
            m = jnp.maximum(jnp.max(s, axis=0, keepdims=True), sink)
            out.append((jnp.exp(s - m).astype(BF16), jnp.exp(sink - m)))
        return out

    def weighted_values(j, h, probs):
        qpos = slice(j * WINDOW, (j + 1) * WINDOW)
        vth = value_window(j, h)
        for par, (p, sink_term) in enumerate(probs):
            nd = jnp.dot(vth, p, preferred_element_type=F32)
            denom = nd[HEAD_DIM:, :] + sink_term
            on = nd[:HEAD_DIM, :] * (1.0 / denom)
            for rp in range(2):
                head = 4 * h + 2 * rp + par
                ot_scr[head * HEAD_DIM:(head + 1) * HEAD_DIM, qpos] = on[:, rp * WINDOW:(rp + 1) * WINDOW]

    x = x_ref[0]
    xn = _rms(x, g_ref[...]).astype(BF16)
    tabs = (cos_ref[...], sina_ref[...], sinb_ref[...])

    def project(c):
        cols = slice(c * MXU_DIM, (c + 1) * MXU_DIM)
        t = jnp.dot(xn, wqkv_ref[:, cols], preferred_element_type=F32)
        if c * MXU_DIM < D_QK:
            t = _qk_norm_rope(t, ones_ref[...], gain_ref[:, cols], *tabs)
        if c * MXU_DIM < D_MODEL:
            qt_nxt[cols, :] = t.T.astype(BF16)
        elif c * MXU_DIM < D_QK:
            kc_ref[0] = t[TQ - WINDOW:, :]
            for h in range(N_KV_HEADS):
                lo_part, hi_part = _lane_halves(t[:, (h // 2) * LANES:(h // 2 + 1) * LANES])
                if h % 2 == 0:
                    a, b = lo_part, pltpu.roll(lo_part, HEAD_DIM, 1)
                else:
                    a, b = pltpu.roll(hi_part, HEAD_DIM, 1), hi_part
                ka_nxt[h] = a.astype(BF16)
                kb_nxt[h] = b.astype(BF16)
        else:
            vc_ref[0] = t[TQ - WINDOW:, :]
            vt = t.T
            for h in range(N_KV_HEADS):
                vt_nxt[h] = vt[h * HEAD_DIM:(h + 1) * HEAD_DIM, :].astype(BF16)

    groups = [(j, h) for j in range(TQ // WINDOW) for h in range(N_KV_HEADS)]
    n_groups = len(groups)
    st, probs = {}, {}
    for n in range(n_groups + ATTN_AHEAD):
        if n < n_groups:
            st[n] = scores(*groups[n])
        if n < D_QKV // MXU_DIM:
            project(n)
        d = n - ATTN_AHEAD + 1
        if 0 <= d < n_groups:
            probs[d] = softmax(*groups[d], st.pop(d))
        d = n - ATTN_AHEAD
        if 0 <= d < n_groups:
            weighted_values(*groups[d], probs.pop(d))
    attn = ot_scr[...].T.astype(BF16)
    o_ref[0] = xprev_scr[...] + jnp.dot(attn, wo_ref[...], preferred_element_type=F32)

    xprev_scr[...] = x
    qt_cur[...] = qt_nxt[...]
    for prev, cur, nxt in ((ka_prev, ka_cur, ka_nxt), (kb_prev, kb_cur, kb_nxt)):
        prev[...] = cur[:, TQ - WINDOW:, :]
        cur[...] = nxt[...]
    vt_prev[:, 0:HEAD_DIM, :] = vt_cur[:, 0:HEAD_DIM, TQ - WINDOW:]
    vt_cur[:, 0:HEAD_DIM, :] = vt_nxt[...]


def _attn_prompt_call(x, g, wqkv, ones_bd, gain, cos, sina, sinb, bias, sinks, wo, *, layer):
    bsz, seq, _ = x.shape
    bps = seq // TQ
    n_blocks = bsz * bps
    k_blk = pltpu.VMEM((N_KV_HEADS, TQ, LANES), BF16)
    k_tail = pltpu.VMEM((N_KV_HEADS, WINDOW, LANES), BF16)
    vt_blk = pltpu.VMEM((N_KV_HEADS, 2 * HEAD_DIM, TQ), BF16)
    vt_tail = pltpu.VMEM((N_KV_HEADS, 2 * HEAD_DIM, WINDOW), BF16)

    def proj_block(gi):
        blk = jnp.minimum(gi, n_blocks - 1)
        return blk // bps, blk % bps

    def attn_block(gi):
        blk = jnp.maximum(gi - 1, 0)
        return blk // bps, blk % bps

    tab_spec = pl.BlockSpec((TQ, LANES), lambda gi: (proj_block(gi)[1], 0))
    cache_spec = pl.BlockSpec((1, WINDOW, D_KV), lambda gi: (proj_block(gi)[0], 0, 0))
    body = functools.partial(_attn_prompt_body, blocks_per_seq=bps)
    return pl.pallas_call(
        body,
        grid=(n_blocks + 1,),
        in_specs=[
            pl.BlockSpec((1, TQ, D_MODEL), lambda gi: (*proj_block(gi), 0)),
            _const_spec((1, D_MODEL)),
            _layer_spec(wqkv, layer),
            _const_spec((MXU_DIM, MXU_DIM)),
            _const_spec((1, D_QK)),
            tab_spec, tab_spec, tab_spec,
            _const_spec((2 * WINDOW, WINDOW)),
            pl.BlockSpec(memory_space=pltpu.SMEM),
            _layer_spec(wo, layer),
        ],
        out_specs=[
            pl.BlockSpec((1, TQ, D_MODEL), lambda gi: (*attn_block(gi), 0)),
            cache_spec, cache_spec,
        ],
        out_shape=[
            jax.ShapeDtypeStruct(x.shape, F32),
            jax.ShapeDtypeStruct((bsz, WINDOW, D_KV), F32),
            jax.ShapeDtypeStruct((bsz, WINDOW, D_KV), F32),
        ],
        scratch_shapes=[pltpu.VMEM((D_MODEL, TQ), BF16), k_blk, k_blk, vt_blk,
                        pltpu.VMEM((D_MODEL, TQ), F32),
                        pltpu.VMEM((TQ, D_MODEL), F32),
                        pltpu.VMEM((D_MODEL, TQ), BF16), k_blk, k_blk,
                        pltpu.VMEM((N_KV_HEADS, HEAD_DIM, TQ), BF16),
                        k_tail, k_tail, vt_tail],
        compiler_params=pltpu.CompilerParams(
            dimension_semantics=("arbitrary",), vmem_limit_bytes=VMEM_LIMIT),
        name="attn_prompt",
    )(x, g, wqkv, ones_bd, gain, cos, sina, sinb, bias, sinks, wo)


SEQ_BLOCK = 16
SAMPLE_AHEAD = 4
N_KEYS_PAD = 2 * WINDOW


def _attn_sample_body(x_ref, ck_ref, cv_ref, g_ref, wqkv_ref, ones_ref, gain_ref, cos_ref, sina_ref,
                      sinb_ref, bias_ref, sink_ref, wo_ref, o_ref, kc_ref, vc_ref,
                      kf_scr, vf_scr, oall_scr, *, s_len):
    rows = SEQ_BLOCK * s_len
    w_keep = WINDOW - s_len

    @pl.when(pl.program_id(0) == 0)
    def _():
        pad = (SEQ_BLOCK, N_KEYS_PAD - WINDOW, D_KV)
        kf_scr[:, WINDOW:, :] = jnp.zeros(pad, BF16)
        vf_scr[:, WINDOW:, :] = jnp.zeros(pad, BF16)

    x = x_ref[...].reshape(rows, D_MODEL)
    xn = _rms(x, g_ref[...]).astype(BF16)
    qkv = jnp.dot(xn, wqkv_ref[...], preferred_element_type=F32)
    qk = _qk_norm_rope(qkv[:, :D_QK], ones_ref[...], gain_ref[...],
                       cos_ref[...], sina_ref[...], sinb_ref[...])
    q = qk[:, :D_MODEL]
    k_new = qk[:, D_MODEL:].reshape(SEQ_BLOCK, s_len, D_KV)
    v_new = qkv[:, D_QK:].reshape(SEQ_BLOCK, s_len, D_KV)

    ck = ck_ref[...]
    cv = cv_ref[...]
    kc_ref[:, 0:w_keep, :] = ck[:, s_len:, :]
    vc_ref[:, 0:w_keep, :] = cv[:, s_len:, :]
    kc_ref[:, w_keep:, :] = k_new
    vc_ref[:, w_keep:, :] = v_new
    kf_scr[:, 0:WINDOW, :] = ck.astype(BF16)
    vf_scr[:, 0:WINDOW, :] = cv.astype(BF16)
    tile = 2 * SUBLANES
    zpad = jnp.zeros((SEQ_BLOCK, tile - s_len, D_KV), F32)
    kf_scr[:, WINDOW:WINDOW + tile, :] = jnp.concatenate([k_new, zpad], axis=1).astype(BF16)
    vf_scr[:, WINDOW:WINDOW + tile, :] = jnp.concatenate([v_new, zpad], axis=1).astype(BF16)

    bias = bias_ref[...]
    sink = sink_ref[...]
    nt_dims = (((1,), (1,)), ((), ()))
    zero_col = jnp.zeros((s_len, LANES), F32)
    def scores(s):
        qs = q[s * s_len:(s + 1) * s_len, :]
        tiles = []
        for h in range(N_KV_HEADS):
            for r in range(KV_REP):
                head = KV_REP * h + r
                src = qs[:, (head // 2) * LANES:(head // 2 + 1) * LANES]
                lo_part, hi_part = _lane_halves(src)
                part = lo_part if head % 2 == 0 else hi_part
                if head % 2 != h % 2:
                    part = pltpu.roll(part, HEAD_DIM, 1)
                tiles.append(jnp.concatenate([part, zero_col] if h < 2 else [zero_col, part], axis=1))
        wt = jnp.concatenate(tiles, axis=0).astype(BF16)
        return lax.dot_general(wt, kf_scr[s], nt_dims, preferred_element_type=F32)

    def softmax(sc):
        sc = sc + bias
        m = jnp.maximum(jnp.max(sc, axis=-1, keepdims=True), sink)
        p = jnp.exp(sc - m)
        denom = jnp.sum(p, axis=-1, keepdims=True) + jnp.exp(sink - m)
        return p.astype(BF16), 1.0 / denom

    def weighted_values(s, p, inv_denom):
        o = jnp.dot(p, vf_scr[s], preferred_element_type=F32) * inv_denom
        cols = []
        for c in range(N_HEADS // 2):
            pieces = []
            for head in (2 * c, 2 * c + 1):
                h, r = divmod(head, KV_REP)
                blk = o[(KV_REP * h + r) * s_len:(KV_REP * h + r + 1) * s_len,
                        (h // 2) * LANES:(h // 2 + 1) * LANES]
                lo_part, hi_part = _lane_halves(blk)
                part = lo_part if h % 2 == 0 else hi_part
                if h % 2 != head % 2:
                    part = pltpu.roll(part, HEAD_DIM, 1)
                pieces.append(part)
            cols.append(pieces[0] + pieces[1])
        oall_scr[s * s_len:(s + 1) * s_len, :] = jnp.concatenate(cols, axis=-1)

    sc, probs = {}, {}
    for n in range(SEQ_BLOCK + SAMPLE_AHEAD):
        if n < SEQ_BLOCK:
            sc[n] = scores(n)
        d = n - SAMPLE_AHEAD // 2
        if 0 <= d < SEQ_BLOCK:
            probs[d] = softmax(sc.pop(d))
        d = n - SAMPLE_AHEAD
        if 0 <= d < SEQ_BLOCK:
            weighted_values(d, *probs.pop(d))

    out = jnp.dot(oall_scr[...].astype(BF16), wo_ref[...], preferred_element_type=F32)
    o_ref[...] = (x + out).reshape(SEQ_BLOCK, s_len, D_MODEL)


def _attn_sample_call(x, ck, cv, g, wqkv, ones_bd, gain, cos, sina, sinb, bias, sink_rows, wo, *, layer):
    bsz, s_len, _ = x.shape
    rows = SEQ_BLOCK * s_len
    body = functools.partial(_attn_sample_body, s_len=s_len)
    seq_spec = lambda shape: pl.BlockSpec(shape, lambda i: (i, 0, 0))
    cache_spec = pl.BlockSpec((None, SEQ_BLOCK, WINDOW, D_KV), lambda i: (layer, i, 0, 0))
    return pl.pallas_call(
        body,
        grid=(bsz // SEQ_BLOCK,),
        in_specs=[
            seq_spec((SEQ_BLOCK, s_len, D_MODEL)),
            cache_spec,
            cache_spec,
            _const_spec((1, D_MODEL)),
            _layer_spec(wqkv, layer),
            _const_spec((MXU_DIM, MXU_DIM)),
            _const_spec((1, D_QK)),
            _const_spec((rows, LANES)),
            _const_spec((rows, LANES)),
            _const_spec((rows, LANES)),
            _const_spec((N_HEADS * s_len, N_KEYS_PAD)),
            _const_spec((N_HEADS * s_len, 1)),
            _layer_spec(wo, layer),
        ],
        out_specs=[
            seq_spec((SEQ_BLOCK, s_len, D_MODEL)),
            seq_spec((SEQ_BLOCK, WINDOW, D_KV)),
            seq_spec((SEQ_BLOCK, WINDOW, D_KV)),
        ],
        out_shape=[
            jax.ShapeDtypeStruct(x.shape, F32),
            jax.ShapeDtypeStruct(ck.shape[1:], F32),
            jax.ShapeDtypeStruct(cv.shape[1:], F32),
        ],
        scratch_shapes=[
            pltpu.VMEM((SEQ_BLOCK, N_KEYS_PAD, D_KV), BF16),
            pltpu.VMEM((SEQ_BLOCK, N_KEYS_PAD, D_KV), BF16),
            pltpu.VMEM((rows, D_MODEL), F32),
        ],
        compiler_params=pltpu.CompilerParams(
            dimension_semantics=("arbitrary",), vmem_limit_bytes=VMEM_LIMIT),
        name="attn_sample",
    )(x, ck, cv, g, wqkv, ones_bd, gain, cos, sina, sinb, bias, sink_rows, wo)


def _ssm_weights(a_re, a_im, log_dt, b_re, b_im, c_re, c_im):
    nl = a_re.shape[0]
    dt = jnp.exp(log_dt)[..., None]
    mag = jnp.exp(a_re * dt)
    lam_re = mag * jnp.cos(a_im * dt)
    lam_im = mag * jnp.sin(a_im * dt)
    den = a_re * a_re + a_im * a_im
    nr = lam_re - 1.0
    ni = lam_im
    f_re = (nr * a_re + ni * a_im) / den
    f_im = (ni * a_re - nr * a_im) / den
    bb_re = f_re[..., None] * b_re - f_im[..., None] * b_im
    bb_im = f_re[..., None] * b_im + f_im[..., None] * b_re
    gpt = MXU_DIM // SSM_GROUP
    n_st = gpt * STATE_DIM
    same_group = jnp.asarray(
        (np.arange(MXU_DIM)[:, None] // SSM_GROUP) == (np.arange(n_st)[None, :] // STATE_DIM))

    def b_tiles(bb):
        t = jnp.swapaxes(bb, 2, 3).reshape(nl, N_GTILES, MXU_DIM, STATE_DIM)
        return jnp.where(same_group, jnp.tile(t, (1, 1, 1, gpt)), 0.0)

    def c_tiles(cc):
        t = jnp.swapaxes(cc, 2, 3).reshape(nl, N_GTILES, n_st, SSM_GROUP)
        return jnp.where(same_group.T, jnp.tile(t, (1, 1, 1, gpt)), 0.0)

    slabbed = (nl, N_GTILES, MXU_DIM, SLABS_PER_TILE, 1, LANES)
    wb = jnp.concatenate([b_tiles(bb_re).reshape(slabbed), b_tiles(bb_im).reshape(slabbed)], axis=4)
    wb = wb.reshape(nl, N_GTILES, MXU_DIM, 2 * n_st).astype(BF16)
    wc = jnp.concatenate([c_tiles(c_re), -c_tiles(c_im)], axis=2).astype(BF16)
    lam = jnp.concatenate([lam_re.reshape(nl, N_SLABS, 1, LANES), lam_im.reshape(nl, N_SLABS, 1, LANES)], axis=1)
    lam = jnp.broadcast_to(lam, (nl, 2 * N_SLABS, SUBLANES, LANES))
    return wb, wc, lam


def _rope_tables(pos):
    half = ROT_DIM // 2
    inv_freq = ROPE_THETA ** (-jnp.arange(half, dtype=F32) * 2.0 / ROT_DIM)
    ang = pos[:, None] * inv_freq[None, :]
    cos, sin = jnp.cos(ang), jnp.sin(ang)
    n = pos.shape[0]
    ones = jnp.ones((n, HEAD_DIM - ROT_DIM), F32)
    zeros = jnp.zeros((n, HEAD_DIM - ROT_DIM), F32)
    zh = jnp.zeros((n, half), F32)
    cos_h = jnp.concatenate([cos, cos, ones], axis=1)
    sina_h = jnp.concatenate([zh, sin, zeros], axis=1)
    sinb_h = jnp.concatenate([-sin, zh, zeros], axis=1)
    return tuple(jnp.concatenate([t, t], axis=1) for t in (cos_h, sina_h, sinb_h))


def _ones_block_diag():
    idx = np.arange(MXU_DIM) // HEAD_DIM
    return jnp.asarray(idx[:, None] == idx[None, :], dtype=BF16)


def _prompt_bias():
    js = np.arange(2 * WINDOW)[:, None]
    iq = np.arange(WINDOW)[None, :]
    diff = WINDOW + iq - js
    return jnp.asarray(np.where((diff >= 0) & (diff < WINDOW), 0.0, NEG_INF), dtype=F32)


def _sample_bias(s_len):
    t = (np.arange(N_HEADS * s_len) % s_len)[:, None]
    key = np.arange(N_KEYS_PAD)[None, :]
    diff = np.where(key < WINDOW, WINDOW + t - key, t - (key - WINDOW))
    ok = (diff >= 0) & (diff < WINDOW) & (key < WINDOW + s_len)
    return jnp.asarray(np.where(ok, 0.0, NEG_INF), dtype=F32)


SSM_PROMPT_TC = 64
SSM_SAMPLE_BB = 32
FFN_TM = 512


def kernel(x_prompt, x_sample, state_ssm_re, state_ssm_im, cache_swa_k, cache_swa_v, norm_mix, norm_ffn, ssm_a_re, ssm_a_im, ssm_log_dt, ssm_b_re, ssm_b_im, ssm_c_re, ssm_c_im, ssm_d, ssm_w_glu, attn_w_qkv, attn_q_norm, attn_k_norm, attn_sinks, attn_w_o, ffn_w_gate_up, ffn_w_down):
    bsz, seq, _ = x_prompt.shape
    dbsz, dseq, _ = x_sample.shape
    yp, ys = x_prompt, x_sample
    p_re, p_im, p_k, p_v, s_re, s_im, s_k, s_v = ([] for _ in range(8))

    ones_bd = _ones_block_diag()
    p_tabs = _rope_tables(jnp.arange(seq, dtype=F32))
    s_pos = PAST_LEN + jnp.arange(dseq, dtype=F32)
    s_tabs = tuple(jnp.tile(t, (SEQ_BLOCK, 1)) for t in _rope_tables(s_pos))
    p_bias = _prompt_bias()
    s_bias = _sample_bias(dseq)
    h0 = jnp.zeros((bsz, N_STATE), F32)

    wb, wc, lam = _ssm_weights(ssm_a_re, ssm_a_im, ssm_log_dt, ssm_b_re, ssm_b_im, ssm_c_re, ssm_c_im)
    wglu = ssm_w_glu.astype(BF16)
    wqkv = attn_w_qkv.astype(BF16)
    wo = attn_w_o.astype(BF16)
    wgu = ffn_w_gate_up.astype(BF16)
    wd = ffn_w_down.astype(BF16)
    n_swa = cache_swa_k.shape[0]
    ck_all = cache_swa_k.reshape(n_swa, dbsz, WINDOW, D_KV)
    cv_all = cache_swa_v.reshape(n_swa, dbsz, WINDOW, D_KV)

    for i in range(DEPTH):
        j = i // 2
        g_mix = norm_mix[i].reshape(1, D_MODEL)
        if i % 2 == 0:
            d = ssm_d[j].reshape(1, D_MODEL)
            yp, hr, hi = _ssm_call(yp, g_mix, wb, wc, lam, d, wglu, h0, h0,
                                   layer=j, bb=bsz, tc=SSM_PROMPT_TC)
            ys, sr, si = _ssm_call(ys, g_mix, wb, wc, lam, d, wglu,
                                   state_ssm_re[j].reshape(dbsz, N_STATE), state_ssm_im[j].reshape(dbsz, N_STATE),
                                   layer=j, bb=SSM_SAMPLE_BB, tc=dseq)
            shp = (N_GROUPS, STATE_DIM)
            p_re.append(hr.reshape(bsz, *shp))
            p_im.append(hi.reshape(bsz, *shp))
            s_re.append(sr.reshape(dbsz, *shp))
            s_im.append(si.reshape(dbsz, *shp))
        else:
            gain = jnp.concatenate([jnp.tile(attn_q_norm[j] * ATTN_SCALE, N_HEADS),
                                    jnp.tile(attn_k_norm[j], N_KV_HEADS)]).reshape(1, D_QK)
            sinks = attn_sinks[j]
            yp, kp, vp = _attn_prompt_call(yp, g_mix, wqkv, ones_bd, gain, *p_tabs, p_bias, sinks, wo, layer=j)
            sink_rows = jnp.repeat(sinks, dseq).reshape(N_HEADS * dseq, 1)
            ys, ks_, vs_ = _attn_sample_call(ys, ck_all, cv_all, g_mix, wqkv, ones_bd, gain, *s_tabs, s_bias,
                                             sink_rows, wo, layer=j)
            kv_shape = (WINDOW, N_KV_HEADS, HEAD_DIM)
            p_k.append(kp.reshape(bsz, *kv_shape))
            p_v.append(vp.reshape(bsz, *kv_shape))
            s_k.append(ks_.reshape(dbsz, *kv_shape))
            s_v.append(vs_.reshape(dbsz, *kv_shape))
        g_ffn = norm_ffn[i].reshape(1, D_MODEL)
        yp = _ffn_call(yp.reshape(bsz * seq, D_MODEL), g_ffn, wgu, wd, layer=i, tm=FFN_TM).reshape(bsz, seq, D_MODEL)
        ys = _ffn_call(ys.reshape(dbsz * dseq, D_MODEL), g_ffn, wgu, wd, layer=i, tm=FFN_TM).reshape(dbsz, dseq, D_MODEL)

    return (yp, ys, jnp.stack(p_re), jnp.stack(p_im), jnp.stack(p_k), jnp.stack(p_v),
            jnp.stack(s_re), jnp.stack(s_im), jnp.stack(s_k), jnp.stack(s_v))
```

```python
import functools

import jax
import jax.numpy as jnp
import numpy as np
from jax import lax
from jax.experimental import pallas as pl
from jax.experimental.pallas import tpu as pltpu

F32 = jnp.float32
BF16 = jnp.bfloat16

D_MODEL = 1024
DEPTH = 4
PAST_LEN = 8192
SSM_GROUP = 16
N_GROUPS = D_MODEL // SSM_GROUP
STATE_DIM = 64
N_STATE = N_GROUPS * STATE_DIM
HEAD_DIM = 64
N_HEADS = D_MODEL // HEAD_DIM
N_KV_HEADS = 4
KV_REP = N_HEADS // N_KV_HEADS
D_KV = N_KV_HEADS * HEAD_DIM
D_QK = D_MODEL + D_KV
D_QKV = D_MODEL + 2 * D_KV
WINDOW = 128
ROT_DIM = HEAD_DIM // 4
ROPE_THETA = 500000.0
ATTN_SCALE = HEAD_DIM ** -0.5
D_FF = 2816
NORM_EPS = 1e-6
NEG_INF = -1e30

LANES = 128
SUBLANES = 8
MXU_DIM = 256
N_SLABS = N_STATE // LANES
VMEM_LIMIT = 56 * 1024 * 1024


def _rms(x, g):
    ms = jnp.mean(x * x, axis=-1, keepdims=True)
    return x * lax.rsqrt(ms + NORM_EPS) * g


def _const_spec(shape):
    nd = len(shape)
    return pl.BlockSpec(shape, lambda *_: (0,) * nd, pipeline_mode=pl.Buffered(1))


def _layer_spec(stack, layer):
    nd = stack.ndim - 1
    return pl.BlockSpec((None,) + stack.shape[1:], lambda *_: (layer,) + (0,) * nd,
                        pipeline_mode=pl.Buffered(1))


FF_CHUNK = 256


def _ffn_body(x_ref, g_ref, wg_ref, wu_ref, wd_ref, o_ref, a_scr):
    x = x_ref[...]
    xn = _rms(x, g_ref[...]).astype(BF16)
    for c in range(D_FF // FF_CHUNK):
        cols = slice(c * FF_CHUNK, (c + 1) * FF_CHUNK)
        gate = jnp.dot(xn, wg_ref[:, cols], preferred_element_type=F32)
        up = jnp.dot(xn, wu_ref[:, cols], preferred_element_type=F32)
        a_scr[:, cols] = (gate * jax.nn.sigmoid(gate) * up).astype(BF16)
    o_ref[...] = x + jnp.dot(a_scr[...], wd_ref[...], preferred_element_type=F32)


def _ffn_call(x, g, wg, wu, wd, *, tm):
    m = x.shape[0]
    return pl.pallas_call(
        _ffn_body,
        grid=(m // tm,),
        in_specs=[
            pl.BlockSpec((tm, D_MODEL), lambda i: (i, 0)),
            _const_spec((1, D_MODEL)),
            _const_spec(wg.shape),
            _const_spec(wu.shape),
            _const_spec(wd.shape),
        ],
        out_specs=pl.BlockSpec((tm, D_MODEL), lambda i: (i, 0)),
        out_shape=jax.ShapeDtypeStruct((m, D_MODEL), F32),
        scratch_shapes=[pltpu.VMEM((tm, D_FF), BF16)],
        compiler_params=pltpu.CompilerParams(
            dimension_semantics=("arbitrary",), vmem_limit_bytes=VMEM_LIMIT),
        name="ffn",
    )(x, g, wg, wu, wd)


def _ffn_cast_body(x_ref, g_ref, wg_ref, wu_ref, wd_ref, o_ref, wgb_ref, wub_ref, wdb_ref, xn_scr):
    c = pl.program_id(0)

    @pl.when(c == 0)
    def _():
        x = x_ref[...]
        xn_scr[...] = _rms(x, g_ref[...]).astype(BF16)
        o_ref[...] = x

    wg = wg_ref[...].astype(BF16)
    wu = wu_ref[...].astype(BF16)
    wd = wd_ref[...].astype(BF16)
    wgb_ref[...] = wg
    wub_ref[...] = wu
    wdb_ref[...] = wd
    xn = xn_scr[...]
    gate = jnp.dot(xn, wg, preferred_element_type=F32)
    up = jnp.dot(xn, wu, preferred_element_type=F32)
    a = (gate * jax.nn.sigmoid(gate) * up).astype(BF16)
    o_ref[...] += jnp.dot(a, wd, preferred_element_type=F32)


def _ffn_cast_call(x, g, wgu, wd, *, layer):
    m = x.shape[0]
    n_chunks = D_FF // FF_CHUNK
    return pl.pallas_call(
        _ffn_cast_body,
        grid=(n_chunks,),
        in_specs=[
            pl.BlockSpec((m, D_MODEL), lambda c: (0, 0)),
            _const_spec((1, D_MODEL)),
            pl.BlockSpec((None, D_MODEL, FF_CHUNK), lambda c: (layer, 0, c)),
            pl.BlockSpec((None, D_MODEL, FF_CHUNK), lambda c: (layer, 0, n_chunks + c)),
            pl.BlockSpec((None, FF_CHUNK, D_MODEL), lambda c: (layer, c, 0)),
        ],
        out_specs=[
            pl.BlockSpec((m, D_MODEL), lambda c: (0, 0)),
            pl.BlockSpec((D_MODEL, FF_CHUNK), lambda c: (0, c)),
            pl.BlockSpec((D_MODEL, FF_CHUNK), lambda c: (0, c)),
            pl.BlockSpec((FF_CHUNK, D_MODEL), lambda c: (c, 0)),
        ],
        out_shape=[
            jax.ShapeDtypeStruct((m, D_MODEL), F32),
            jax.ShapeDtypeStruct((D_MODEL, D_FF), BF16),
            jax.ShapeDtypeStruct((D_MODEL, D_FF), BF16),
            jax.ShapeDtypeStruct((D_FF, D_MODEL), BF16),
        ],
        scratch_shapes=[pltpu.VMEM((m, D_MODEL), BF16)],
        compiler_params=pltpu.CompilerParams(
            dimension_semantics=("arbitrary",), vmem_limit_bytes=VMEM_LIMIT),
        name="ffn_cast",
    )(x, g, wgu, wgu, wd)


N_GTILES = D_MODEL // MXU_DIM
SLABS_PER_TILE = N_SLABS // N_GTILES


SSM_AHEAD = 1
SSM_ROW_PAD = 1


def _ssm_body(x_ref, g_ref, wb_ref, wc_ref, lam_ref, d_ref, wglu_ref, h0r_ref, h0i_ref,
              o_ref, hr_ref, hi_ref, xs_scr, os_scr, *, bb, tc):
    pitch = tc + SSM_ROW_PAD
    spt = SLABS_PER_TILE
    n_cols = D_MODEL // LANES
    n_rg = bb // SUBLANES

    @pl.when(pl.program_id(1) == 0)
    def _():
        hr_ref[...] = h0r_ref[...]
        hi_ref[...] = h0i_ref[...]

    def gather_idx(rg, t):
        return pl.ds(rg * SUBLANES * pitch + t, SUBLANES, stride=pitch)

    g = g_ref[...]
    for b in range(bb):
        xn = _rms(x_ref[b], g)
        for c in range(n_cols):
            xs_scr[c, b * pitch:b * pitch + tc, :] = xn[:, c * LANES:(c + 1) * LANES]
    u = jnp.concatenate(
        [jnp.concatenate([xs_scr[c, gather_idx(rg, t), :] for c in range(n_cols)], axis=1)
         for rg in range(n_rg) for t in range(tc)], axis=0)
    ub = u.astype(BF16)

    def project_in(kt):
        return jnp.dot(ub[:, kt * MXU_DIM:(kt + 1) * MXU_DIM], wb_ref[kt], preferred_element_type=F32)

    def scan(kt, bu):
        re_cols, im_cols = [], []
        for j in range(spt):
            s = kt * spt + j
            lanes = slice(s * LANES, (s + 1) * LANES)
            lr = lam_ref[s]
            li = lam_ref[N_SLABS + s]
            re_rows, im_rows = [], []
            for rg in range(n_rg):
                seqs = slice(rg * SUBLANES, (rg + 1) * SUBLANES)
                hr = hr_ref[seqs, lanes]
                hi = hi_ref[seqs, lanes]
                for t in range(tc):
                    rows = slice((rg * tc + t) * SUBLANES, (rg * tc + t + 1) * SUBLANES)
                    nr = (lr * hr - li * hi) + bu[rows, (2 * j) * LANES:(2 * j + 1) * LANES]
                    ni = (lr * hi + li * hr) + bu[rows, (2 * j + 1) * LANES:(2 * j + 2) * LANES]
                    hr, hi = nr, ni
                    re_rows.append(nr)
                    im_rows.append(ni)
                hr_ref[seqs, lanes] = hr
                hi_ref[seqs, lanes] = hi
            re_cols.append(jnp.concatenate(re_rows, axis=0))
            im_cols.append(jnp.concatenate(im_rows, axis=0))
        return jnp.concatenate(re_cols + im_cols, axis=1).astype(BF16)

    ys = []
    bu = {kt: project_in(kt) for kt in range(min(SSM_AHEAD, N_GTILES))}
    for kt in range(N_GTILES):
        if kt + SSM_AHEAD < N_GTILES:
            bu[kt + SSM_AHEAD] = project_in(kt + SSM_AHEAD)
        hs = scan(kt, bu.pop(kt))
        ys.append(jnp.dot(hs, wc_ref[kt], preferred_element_type=F32))
    y = jnp.concatenate(ys, axis=-1) + d_ref[...] * u
    z = jax.nn.gelu(y).astype(BF16)
    gl = jnp.dot(z, wglu_ref[...], preferred_element_type=F32)
    out = gl[:, :D_MODEL] * jax.nn.sigmoid(gl[:, D_MODEL:])
    for rg in range(n_rg):
        for t in range(tc):
            rows = slice((rg * tc + t) * SUBLANES, (rg * tc + t + 1) * SUBLANES)
            for c in range(n_cols):
                os_scr[c, gather_idx(rg, t), :] = out[rows, c * LANES:(c + 1) * LANES]
    for b in range(bb):
        o_ref[b] = x_ref[b] + jnp.concatenate(
            [os_scr[c, b * pitch:b * pitch + tc, :] for c in range(n_cols)], axis=1)


def _ssm_call(x, g, wb, wc, lam, d, wglu, h0r, h0i, *, layer, bb, tc):
    bsz, seq, _ = x.shape
    mp = bb * (tc + SSM_ROW_PAD)
    body = functools.partial(_ssm_body, bb=bb, tc=tc)
    return pl.pallas_call(
        body,
        grid=(bsz // bb, seq // tc),
        in_specs=[
            pl.BlockSpec((bb, tc, D_MODEL), lambda i, j: (i, j, 0)),
            _const_spec((1, D_MODEL)),
            _layer_spec(wb, layer),
            _layer_spec(wc, layer),
            _layer_spec(lam, layer),
            _const_spec((1, D_MODEL)),
            _layer_spec(wglu, layer),
            pl.BlockSpec((bb, N_STATE), lambda i, j: (i, 0)),
            pl.BlockSpec((bb, N_STATE), lambda i, j: (i, 0)),
        ],
        out_specs=[
            pl.BlockSpec((bb, tc, D_MODEL), lambda i, j: (i, j, 0)),
            pl.BlockSpec((bb, N_STATE), lambda i, j: (i, 0)),
            pl.BlockSpec((bb, N_STATE), lambda i, j: (i, 0)),
        ],
        out_shape=[
            jax.ShapeDtypeStruct(x.shape, F32),
            jax.ShapeDtypeStruct((bsz, N_STATE), F32),
            jax.ShapeDtypeStruct((bsz, N_STATE), F32),
        ],
        scratch_shapes=[pltpu.VMEM((D_MODEL // LANES, mp, LANES), F32),
                        pltpu.VMEM((D_MODEL // LANES, mp, LANES), F32)],
        compiler_params=pltpu.CompilerParams(
            dimension_semantics=("arbitrary", "arbitrary"), vmem_limit_bytes=VMEM_LIMIT),
        name="ssm",
    )(x, g, wb, wc, lam, d, wglu, h0r, h0i)


def _qk_norm_rope(qk, ones_bd, gain, cos, sina, sinb):
    n_cols = qk.shape[1] // LANES
    sq = qk * qk
    hi = sq.astype(BF16)
    lo = (sq - hi.astype(F32)).astype(BF16)
    outs = []
    for c in range(n_cols // 2):
        sl = slice(c * MXU_DIM, (c + 1) * MXU_DIM)
        ss = (jnp.dot(hi[:, sl], ones_bd, preferred_element_type=F32)
              + jnp.dot(lo[:, sl], ones_bd, preferred_element_type=F32))
        outs.append(qk[:, sl] * lax.rsqrt(ss * (1.0 / HEAD_DIM) + NORM_EPS) * gain[:, sl])
    qn = jnp.concatenate(outs, axis=-1)
    cols = []
    for c in range(n_cols):
        xc = qn[:, c * LANES:(c + 1) * LANES]
        cols.append(xc * cos + pltpu.roll(xc, ROT_DIM // 2, 1) * sina
                    + pltpu.roll(xc, LANES - ROT_DIM // 2, 1) * sinb)
    return jnp.concatenate(cols, axis=-1)


def _lane_halves(x):
    lane = lax.broadcasted_iota(jnp.int32, x.shape, 1)
    low = lane < HEAD_DIM
    return jnp.where(low, x, 0.0), jnp.where(low, 0.0, x)


TQ = 256
ATTN_AHEAD = 2


def _attn_prompt_body(x_ref, g_ref, wqkv_ref, ones_ref, gain_ref, cos_ref, sina_ref, sinb_ref,
                      bias_ref, sink_ref, wo_ref, o_ref, kc_ref, vc_ref,
                      qt_cur, ka_cur, kb_cur, vt_cur, ot_scr, xprev_scr, qt_nxt, ka_nxt, kb_nxt, vt_nxt,
                      ka_prev, kb_prev, vt_prev, *, blocks_per_seq):
    g = pl.program_id(0)

    @pl.when(g == 0)
    def _():
        for scr in (ka_cur, kb_cur, ka_prev, kb_prev):
            scr[...] = jnp.zeros(scr.shape, BF16)
        for scr in (vt_cur, vt_prev):
            scr[:, 0:HEAD_DIM, :] = jnp.zeros((N_KV_HEADS, HEAD_DIM, scr.shape[2]), BF16)
            scr[:, HEAD_DIM:, :] = jnp.ones((N_KV_HEADS, HEAD_DIM, scr.shape[2]), BF16)
        qt_cur[...] = jnp.zeros((D_MODEL, TQ), BF16)
        xprev_scr[...] = jnp.zeros((TQ, D_MODEL), F32)

    bias_mid = jnp.concatenate([bias_ref[...], bias_ref[...]], axis=1)
    first = (g - 1) % blocks_per_seq == 0
    key_row = lax.broadcasted_iota(jnp.int32, bias_mid.shape, 0)
    bias_first = jnp.where(jnp.logical_and(first, key_row < WINDOW), NEG_INF, bias_mid)
    lane = lax.broadcasted_iota(jnp.int32, (1, 2 * WINDOW), 1)
    pair = 2 * HEAD_DIM

    def key_window(j, h, prev, cur):
        if j == 0:
            return [prev[h], cur[h, 0:WINDOW, :]]
        return [cur[h, (j - 1) * WINDOW:(j + 1) * WINDOW, :]]

    def value_window(j, h):
        if j == 0:
            return jnp.concatenate([vt_prev[h], vt_cur[h, :, 0:WINDOW]], axis=1)
        return vt_cur[h, :, (j - 1) * WINDOW:(j + 1) * WINDOW]

    def scores(j, h):
        qpos = slice(j * WINDOW, (j + 1) * WINDOW)
        qs = jnp.concatenate([qt_cur[(2 * h) * pair:(2 * h + 1) * pair, qpos],
                              qt_cur[(2 * h + 1) * pair:(2 * h + 2) * pair, qpos]], axis=1)
        kk = jnp.concatenate(key_window(j, h, ka_prev, ka_cur) + key_window(j, h, kb_prev, kb_cur),
                             axis=0)
        return jnp.dot(kk, qs, preferred_element_type=F32)

    def softmax(j, h, st):
        bias = bias_first if j == 0 else bias_mid
        out = []
        for par in range(2):
            s = st[par * 2 * WINDOW:(par + 1) * 2 * WINDOW, :] + bias
            sink = jnp.where(lane < WINDOW, sink_ref[4 * h + par], sink_ref[4 * h + 2 + par])
            m = jnp.maximum(jnp.max(s, axis=0, keepdims=True), sink)
            out.append((jnp.exp(s - m).astype(BF16), jnp.exp(sink - m)))
        return out

    def weighted_values(j, h, probs):
        qpos = slice(j * WINDOW, (j + 1) * WINDOW)
        vth = value_window(j, h)
        for par, (p, sink_term) in enumerate(probs):
            nd = jnp.dot(vth, p, preferred_element_type=F32)
            denom = nd[HEAD_DIM:, :] + sink_term
            on = nd[:HEAD_DIM, :] * (1.0 / denom)
            for rp in range(2):
                head = 4 * h + 2 * rp + par
                ot_scr[head * HEAD_DIM:(head + 1) * HEAD_DIM, qpos] = on[:, rp * WINDOW:(rp + 1) * WINDOW]

    x = x_ref[0]
    xn = _rms(x, g_ref[...]).astype(BF16)
    tabs = (cos_ref[...], sina_ref[...], sinb_ref[...])

    def project(c):
        cols = slice(c * MXU_DIM, (c + 1) * MXU_DIM)
        t = jnp.dot(xn, wqkv_ref[:, cols], preferred_element_type=F32)
        if c * MXU_DIM < D_QK:
            t = _qk_norm_rope(t, ones_ref[...], gain_ref[:, cols], *tabs)
        if c * MXU_DIM < D_MODEL:
            qt_nxt[cols, :] = t.T.astype(BF16)
        elif c * MXU_DIM < D_QK:
            kc_ref[0] = t[TQ - WINDOW:, :]
            for h in range(N_KV_HEADS):
                lo_part, hi_part = _lane_halves(t[:, (h // 2) * LANES:(h // 2 + 1) * LANES])
                if h % 2 == 0:
                    a, b = lo_part, pltpu.roll(lo_part, HEAD_DIM, 1)
                else:
                    a, b = pltpu.roll(hi_part, HEAD_DIM, 1), hi_part
                ka_nxt[h] = a.astype(BF16)
                kb_nxt[h] = b.astype(BF16)
        else:
            vc_ref[0] = t[TQ - WINDOW:, :]
            vt = t.T
            for h in range(N_KV_HEADS):
                vt_nxt[h] = vt[h * HEAD_DIM:(h + 1) * HEAD_DIM, :].astype(BF16)

    groups = [(j, h) for j in range(TQ // WINDOW) for h in range(N_KV_HEADS)]
    n_groups = len(groups)
    st, probs = {}, {}
    for n in range(n_groups + ATTN_AHEAD):
        if n < n_groups:
            st[n] = scores(*groups[n])
        if n < D_QKV // MXU_DIM:
            project(n)
        d = n - ATTN_AHEAD + 1
        if 0 <= d < n_groups:
            probs[d] = softmax(*groups[d], st.pop(d))
        d = n - ATTN_AHEAD
        if 0 <= d < n_groups:
            weighted_values(*groups[d], probs.pop(d))
    attn = ot_scr[...].T.astype(BF16)
    o_ref[0] = xprev_scr[...] + jnp.dot(attn, wo_ref[...], preferred_element_type=F32)

    xprev_scr[...] = x
    qt_cur[...] = qt_nxt[...]
    for prev, cur, nxt in ((ka_prev, ka_cur, ka_nxt), (kb_prev, kb_cur, kb_nxt)):
        prev[...] = cur[:, TQ - WINDOW:, :]
        cur[...] = nxt[...]
    vt_prev[:, 0:HEAD_DIM, :] = vt_cur[:, 0:HEAD_DIM, TQ - WINDOW:]
    vt_cur[:, 0:HEAD_DIM, :] = vt_nxt[...]


def _attn_prompt_call(x, g, wqkv, ones_bd, gain, cos, sina, sinb, bias, sinks, wo, *, layer):
    bsz, seq, _ = x.shape
    bps = seq // TQ
    n_blocks = bsz * bps
    k_blk = pltpu.VMEM((N_KV_HEADS, TQ, LANES), BF16)
    k_tail = pltpu.VMEM((N_KV_HEADS, WINDOW, LANES), BF16)
    vt_blk = pltpu.VMEM((N_KV_HEADS, 2 * HEAD_DIM, TQ), BF16)
    vt_tail = pltpu.VMEM((N_KV_HEADS, 2 * HEAD_DIM, WINDOW), BF16)

    def proj_block(gi):
        blk = jnp.minimum(gi, n_blocks - 1)
        return blk // bps, blk % bps

    def attn_block(gi):
        blk = jnp.maximum(gi - 1, 0)
        return blk // bps, blk % bps

    tab_spec = pl.BlockSpec((TQ, LANES), lambda gi: (proj_block(gi)[1], 0))
    cache_spec = pl.BlockSpec((1, WINDOW, D_KV), lambda gi: (proj_block(gi)[0], 0, 0))
    body = functools.partial(_attn_prompt_body, blocks_per_seq=bps)
    return pl.pallas_call(
        body,
        grid=(n_blocks + 1,),
        in_specs=[
            pl.BlockSpec((1, TQ, D_MODEL), lambda gi: (*proj_block(gi), 0)),
            _const_spec((1, D_MODEL)),
            _layer_spec(wqkv, layer),
            _const_spec((MXU_DIM, MXU_DIM)),
            _const_spec((1, D_QK)),
            tab_spec, tab_spec, tab_spec,
            _const_spec((2 * WINDOW, WINDOW)),
            pl.BlockSpec(memory_space=pltpu.SMEM),
            _layer_spec(wo, layer),
        ],
        out_specs=[
            pl.BlockSpec((1, TQ, D_MODEL), lambda gi: (*attn_block(gi), 0)),
            cache_spec, cache_spec,
        ],
        out_shape=[
            jax.ShapeDtypeStruct(x.shape, F32),
            jax.ShapeDtypeStruct((bsz, WINDOW, D_KV), F32),
            jax.ShapeDtypeStruct((bsz, WINDOW, D_KV), F32),
        ],
        scratch_shapes=[pltpu.VMEM((D_MODEL, TQ), BF16), k_blk, k_blk, vt_blk,
                        pltpu.VMEM((D_MODEL, TQ), F32),
                        pltpu.VMEM((TQ, D_MODEL), F32),
                        pltpu.VMEM((D_MODEL, TQ), BF16), k_blk, k_blk,
                        pltpu.VMEM((N_KV_HEADS, HEAD_DIM, TQ), BF16),
                        k_tail, k_tail, vt_tail],
        compiler_params=pltpu.CompilerParams(
            dimension_semantics=("arbitrary",), vmem_limit_bytes=VMEM_LIMIT),
        name="attn_prompt",
    )(x, g, wqkv, ones_bd, gain, cos, sina, sinb, bias, sinks, wo)


SEQ_BLOCK = 16
SAMPLE_AHEAD = 4
N_KEYS_PAD = 2 * WINDOW


def _attn_sample_body(x_ref, ck_ref, cv_ref, g_ref, wqkv_ref, ones_ref, gain_ref, cos_ref, sina_ref,
                      sinb_ref, bias_ref, sink_ref, wo_ref, o_ref, kc_ref, vc_ref,
                      kf_scr, vf_scr, oall_scr, *, s_len):
    rows = SEQ_BLOCK * s_len
    w_keep = WINDOW - s_len

    @pl.when(pl.program_id(0) == 0)
    def _():
        pad = (SEQ_BLOCK, N_KEYS_PAD - WINDOW, D_KV)
        kf_scr[:, WINDOW:, :] = jnp.zeros(pad, BF16)
        vf_scr[:, WINDOW:, :] = jnp.zeros(pad, BF16)

    x = x_ref[...].reshape(rows, D_MODEL)
    xn = _rms(x, g_ref[...]).astype(BF16)
    qkv = jnp.dot(xn, wqkv_ref[...], preferred_element_type=F32)
    qk = _qk_norm_rope(qkv[:, :D_QK], ones_ref[...], gain_ref[...],
                       cos_ref[...], sina_ref[...], sinb_ref[...])
    q = qk[:, :D_MODEL]
    k_new = qk[:, D_MODEL:].reshape(SEQ_BLOCK, s_len, D_KV)
    v_new = qkv[:, D_QK:].reshape(SEQ_BLOCK, s_len, D_KV)

    ck = ck_ref[...]
    cv = cv_ref[...]
    kc_ref[:, 0:w_keep, :] = ck[:, s_len:, :]
    vc_ref[:, 0:w_keep, :] = cv[:, s_len:, :]
    kc_ref[:, w_keep:, :] = k_new
    vc_ref[:, w_keep:, :] = v_new
    kf_scr[:, 0:WINDOW, :] = ck.astype(BF16)
    vf_scr[:, 0:WINDOW, :] = cv.astype(BF16)
    tile = 2 * SUBLANES
    zpad = jnp.zeros((SEQ_BLOCK, tile - s_len, D_KV), F32)
    kf_scr[:, WINDOW:WINDOW + tile, :] = jnp.concatenate([k_new, zpad], axis=1).astype(BF16)
    vf_scr[:, WINDOW:WINDOW + tile, :] = jnp.concatenate([v_new, zpad], axis=1).astype(BF16)

    bias = bias_ref[...]
    sink = sink_ref[...]
    nt_dims = (((1,), (1,)), ((), ()))
    zero_col = jnp.zeros((s_len, LANES), F32)
    def scores(s):
        qs = q[s * s_len:(s + 1) * s_len, :]
        tiles = []
        for h in range(N_KV_HEADS):
            for r in range(KV_REP):
                head = KV_REP * h + r
                src = qs[:, (head // 2) * LANES:(head // 2 + 1) * LANES]
                lo_part, hi_part = _lane_halves(src)
                part = lo_part if head % 2 == 0 else hi_part
                if head % 2 != h % 2:
                    part = pltpu.roll(part, HEAD_DIM, 1)
                tiles.append(jnp.concatenate([part, zero_col] if h < 2 else [zero_col, part], axis=1))
        wt = jnp.concatenate(tiles, axis=0).astype(BF16)
        return lax.dot_general(wt, kf_scr[s], nt_dims, preferred_element_type=F32)

    def softmax(sc):
        sc = sc + bias
        m = jnp.maximum(jnp.max(sc, axis=-1, keepdims=True), sink)
        p = jnp.exp(sc - m)
        denom = jnp.sum(p, axis=-1, keepdims=True) + jnp.exp(sink - m)
        return p.astype(BF16), 1.0 / denom

    def weighted_values(s, p, inv_denom):
        o = jnp.dot(p, vf_scr[s], preferred_element_type=F32) * inv_denom
        cols = []
        for c in range(N_HEADS // 2):
            pieces = []
            for head in (2 * c, 2 * c + 1):
                h, r = divmod(head, KV_REP)
                blk = o[(KV_REP * h + r) * s_len:(KV_REP * h + r + 1) * s_len,
                        (h // 2) * LANES:(h // 2 + 1) * LANES]
                lo_part, hi_part = _lane_halves(blk)
                part = lo_part if h % 2 == 0 else hi_part
                if h % 2 != head % 2:
                    part = pltpu.roll(part, HEAD_DIM, 1)
                pieces.append(part)
            cols.append(pieces[0] + pieces[1])
        oall_scr[s * s_len:(s + 1) * s_len, :] = jnp.concatenate(cols, axis=-1)

    sc, probs = {}, {}
    for n in range(SEQ_BLOCK + SAMPLE_AHEAD):
        if n < SEQ_BLOCK:
            sc[n] = scores(n)
        d = n - SAMPLE_AHEAD // 2
        if 0 <= d < SEQ_BLOCK:
            probs[d] = softmax(sc.pop(d))
        d = n - SAMPLE_AHEAD
        if 0 <= d < SEQ_BLOCK:
            weighted_values(d, *probs.pop(d))

    out = jnp.dot(oall_scr[...].astype(BF16), wo_ref[...], preferred_element_type=F32)
    o_ref[...] = (x + out).reshape(SEQ_BLOCK, s_len, D_MODEL)


def _attn_sample_call(x, ck, cv, g, wqkv, ones_bd, gain, cos, sina, sinb, bias, sink_rows, wo, *, layer):
    bsz, s_len, _ = x.shape
    rows = SEQ_BLOCK * s_len
    body = functools.partial(_attn_sample_body, s_len=s_len)
    seq_spec = lambda shape: pl.BlockSpec(shape, lambda i: (i, 0, 0))
    cache_spec = pl.BlockSpec((None, SEQ_BLOCK, WINDOW, D_KV), lambda i: (layer, i, 0, 0))
    return pl.pallas_call(
        body,
        grid=(bsz // SEQ_BLOCK,),
        in_specs=[
            seq_spec((SEQ_BLOCK, s_len, D_MODEL)),
            cache_spec,
            cache_spec,
            _const_spec((1, D_MODEL)),
            _layer_spec(wqkv, layer),
            _const_spec((MXU_DIM, MXU_DIM)),
            _const_spec((1, D_QK)),
            _const_spec((rows, LANES)),
            _const_spec((rows, LANES)),
            _const_spec((rows, LANES)),
            _const_spec((N_HEADS * s_len, N_KEYS_PAD)),
            _const_spec((N_HEADS * s_len, 1)),
            _layer_spec(wo, layer),
        ],
        out_specs=[
            seq_spec((SEQ_BLOCK, s_len, D_MODEL)),
            seq_spec((SEQ_BLOCK, WINDOW, D_KV)),
            seq_spec((SEQ_BLOCK, WINDOW, D_KV)),
        ],
        out_shape=[
            jax.ShapeDtypeStruct(x.shape, F32),
            jax.ShapeDtypeStruct(ck.shape[1:], F32),
            jax.ShapeDtypeStruct(cv.shape[1:], F32),
        ],
        scratch_shapes=[
            pltpu.VMEM((SEQ_BLOCK, N_KEYS_PAD, D_KV), BF16),
            pltpu.VMEM((SEQ_BLOCK, N_KEYS_PAD, D_KV), BF16),
            pltpu.VMEM((rows, D_MODEL), F32),
        ],
        compiler_params=pltpu.CompilerParams(
            dimension_semantics=("arbitrary",), vmem_limit_bytes=VMEM_LIMIT),
        name="attn_sample",
    )(x, ck, cv, g, wqkv, ones_bd, gain, cos, sina, sinb, bias, sink_rows, wo)


def _ssm_weights(a_re, a_im, log_dt, b_re, b_im, c_re, c_im):
    nl = a_re.shape[0]
    dt = jnp.exp(log_dt)[..., None]
    mag = jnp.exp(a_re * dt)
    lam_re = mag * jnp.cos(a_im * dt)
    lam_im = mag * jnp.sin(a_im * dt)
    den = a_re * a_re + a_im * a_im
    nr = lam_re - 1.0
    ni = lam_im
    f_re = (nr * a_re + ni * a_im) / den
    f_im = (ni * a_re - nr * a_im) / den
    bb_re = f_re[..., None] * b_re - f_im[..., None] * b_im
    bb_im = f_re[..., None] * b_im + f_im[..., None] * b_re
    gpt = MXU_DIM // SSM_GROUP
    n_st = gpt * STATE_DIM
    same_group = jnp.asarray(
        (np.arange(MXU_DIM)[:, None] // SSM_GROUP) == (np.arange(n_st)[None, :] // STATE_DIM))

    def b_tiles(bb):
        t = jnp.swapaxes(bb, 2, 3).reshape(nl, N_GTILES, MXU_DIM, STATE_DIM)
        return jnp.where(same_group, jnp.tile(t, (1, 1, 1, gpt)), 0.0)

    def c_tiles(cc):
        t = jnp.swapaxes(cc, 2, 3).reshape(nl, N_GTILES, n_st, SSM_GROUP)
        return jnp.where(same_group.T, jnp.tile(t, (1, 1, 1, gpt)), 0.0)

    wb_re = b_tiles(bb_re).astype(BF16)
    wb_im = b_tiles(bb_im).astype(BF16)
    wb = jnp.concatenate([part[..., j * LANES:(j + 1) * LANES]
                          for j in range(SLABS_PER_TILE) for part in (wb_re, wb_im)], axis=-1)
    wc = jnp.concatenate([c_tiles(c_re), -c_tiles(c_im)], axis=2).astype(BF16)
    lam = jnp.concatenate([lam_re.reshape(nl, N_SLABS, 1, LANES), lam_im.reshape(nl, N_SLABS, 1, LANES)], axis=1)
    lam = jnp.broadcast_to(lam, (nl, 2 * N_SLABS, SUBLANES, LANES))
    return wb, wc, lam


def _rope_tables(pos):
    half = ROT_DIM // 2
    inv_freq = ROPE_THETA ** (-jnp.arange(half, dtype=F32) * 2.0 / ROT_DIM)
    ang = pos[:, None] * inv_freq[None, :]
    cos, sin = jnp.cos(ang), jnp.sin(ang)
    n = pos.shape[0]
    ones = jnp.ones((n, HEAD_DIM - ROT_DIM), F32)
    zeros = jnp.zeros((n, HEAD_DIM - ROT_DIM), F32)
    zh = jnp.zeros((n, half), F32)
    cos_h = jnp.concatenate([cos, cos, ones], axis=1)
    sina_h = jnp.concatenate([zh, sin, zeros], axis=1)
    sinb_h = jnp.concatenate([-sin, zh, zeros], axis=1)
    return tuple(jnp.concatenate([t, t], axis=1) for t in (cos_h, sina_h, sinb_h))


def _ones_block_diag():
    idx = np.arange(MXU_DIM) // HEAD_DIM
    return jnp.asarray(idx[:, None] == idx[None, :], dtype=BF16)


def _prompt_bias():
    js = np.arange(2 * WINDOW)[:, None]
    iq = np.arange(WINDOW)[None, :]
    diff = WINDOW + iq - js
    return jnp.asarray(np.where((diff >= 0) & (diff < WINDOW), 0.0, NEG_INF), dtype=F32)


def _sample_bias(s_len):
    t = (np.arange(N_HEADS * s_len) % s_len)[:, None]
    key = np.arange(N_KEYS_PAD)[None, :]
    diff = np.where(key < WINDOW, WINDOW + t - key, t - (key - WINDOW))
    ok = (diff >= 0) & (diff < WINDOW) & (key < WINDOW + s_len)
    return jnp.asarray(np.where(ok, 0.0, NEG_INF), dtype=F32)


SSM_PROMPT_TC = 64
SSM_SAMPLE_BB = 32
FFN_TM = 1024


def kernel(x_prompt, x_sample, state_ssm_re, state_ssm_im, cache_swa_k, cache_swa_v, norm_mix, norm_ffn, ssm_a_re, ssm_a_im, ssm_log_dt, ssm_b_re, ssm_b_im, ssm_c_re, ssm_c_im, ssm_d, ssm_w_glu, attn_w_qkv, attn_q_norm, attn_k_norm, attn_sinks, attn_w_o, ffn_w_gate_up, ffn_w_down):
    bsz, seq, _ = x_prompt.shape
    dbsz, dseq, _ = x_sample.shape
    yp, ys = x_prompt, x_sample
    p_re, p_im, p_k, p_v, s_re, s_im, s_k, s_v = ([] for _ in range(8))

    ones_bd = _ones_block_diag()
    p_tabs = _rope_tables(jnp.arange(seq, dtype=F32))
    s_pos = PAST_LEN + jnp.arange(dseq, dtype=F32)
    s_tabs = tuple(jnp.tile(t, (SEQ_BLOCK, 1)) for t in _rope_tables(s_pos))
    p_bias = _prompt_bias()
    s_bias = _sample_bias(dseq)
    h0 = jnp.zeros((bsz, N_STATE), F32)

    wb, wc, lam = _ssm_weights(ssm_a_re, ssm_a_im, ssm_log_dt, ssm_b_re, ssm_b_im, ssm_c_re, ssm_c_im)
    wglu = ssm_w_glu.astype(BF16)
    wqkv = attn_w_qkv.astype(BF16)
    wo = attn_w_o.astype(BF16)
    n_swa = cache_swa_k.shape[0]
    ck_all = cache_swa_k.reshape(n_swa, dbsz, WINDOW, D_KV)
    cv_all = cache_swa_v.reshape(n_swa, dbsz, WINDOW, D_KV)

    for i in range(DEPTH):
        j = i // 2
        g_mix = norm_mix[i].reshape(1, D_MODEL)
        if i % 2 == 0:
            d = ssm_d[j].reshape(1, D_MODEL)
            yp, hr, hi = _ssm_call(yp, g_mix, wb, wc, lam, d, wglu, h0, h0,
                                   layer=j, bb=bsz, tc=SSM_PROMPT_TC)
            ys, sr, si = _ssm_call(ys, g_mix, wb, wc, lam, d, wglu,
                                   state_ssm_re[j].reshape(dbsz, N_STATE), state_ssm_im[j].reshape(dbsz, N_STATE),
                                   layer=j, bb=SSM_SAMPLE_BB, tc=dseq)
            shp = (N_GROUPS, STATE_DIM)
            p_re.append(hr.reshape(bsz, *shp))
            p_im.append(hi.reshape(bsz, *shp))
            s_re.append(sr.reshape(dbsz, *shp))
            s_im.append(si.reshape(dbsz, *shp))
        else:
            gain = jnp.concatenate([jnp.tile(attn_q_norm[j] * ATTN_SCALE, N_HEADS),
                                    jnp.tile(attn_k_norm[j], N_KV_HEADS)]).reshape(1, D_QK)
            sinks = attn_sinks[j]
            yp, kp, vp = _attn_prompt_call(yp, g_mix, wqkv, ones_bd, gain, *p_tabs, p_bias, sinks, wo, layer=j)
            sink_rows = jnp.repeat(sinks, dseq).reshape(N_HEADS * dseq, 1)
            ys, ks_, vs_ = _attn_sample_call(ys, ck_all, cv_all, g_mix, wqkv, ones_bd, gain, *s_tabs, s_bias,
                                             sink_rows, wo, layer=j)
            kv_shape = (WINDOW, N_KV_HEADS, HEAD_DIM)
            p_k.append(kp.reshape(bsz, *kv_shape))
            p_v.append(vp.reshape(bsz, *kv_shape))
            s_k.append(ks_.reshape(dbsz, *kv_shape))
            s_v.append(vs_.reshape(dbsz, *kv_shape))
        g_ffn = norm_ffn[i].reshape(1, D_MODEL)
        ys, wg, wu, wd = _ffn_cast_call(ys.reshape(dbsz * dseq, D_MODEL), g_ffn, ffn_w_gate_up, ffn_w_down, layer=i)
        ys = ys.reshape(dbsz, dseq, D_MODEL)
        yp = _ffn_call(yp.reshape(bsz * seq, D_MODEL), g_ffn, wg, wu, wd, tm=FFN_TM).reshape(bsz, seq, D_MODEL)

    return (yp, ys, jnp.stack(p_re), jnp.stack(p_im), jnp.stack(p_k), jnp.stack(p_v),
            jnp.stack(s_re), jnp.stack(s_im), jnp.stack(s_k), jnp.stack(s_v))
```

```python
import functools

import jax
import jax.numpy as jnp
import numpy as np
from jax import lax
from jax.experimental import pallas as pl
from jax.experimental.pallas import tpu as pltpu

F32 = jnp.float32
BF16 = jnp.bfloat16

D_MODEL = 1024
DEPTH = 4
PAST_LEN = 8192
SSM_GROUP = 16
N_GROUPS = D_MODEL // SSM_GROUP
STATE_DIM = 64
N_STATE = N_GROUPS * STATE_DIM
HEAD_DIM = 64
N_HEADS = D_MODEL // HEAD_DIM
N_KV_HEADS = 4
KV_REP = N_HEADS // N_KV_HEADS
D_KV = N_KV_HEADS * HEAD_DIM
D_QK = D_MODEL + D_KV
D_QKV = D_MODEL + 2 * D_KV
WINDOW = 128
ROT_DIM = HEAD_DIM // 4
ROPE_THETA = 500000.0
ATTN_SCALE = HEAD_DIM ** -0.5
D_FF = 2816
NORM_EPS = 1e-6
NEG_INF = -1e30

LANES = 128
SUBLANES = 8
MXU_DIM = 256
N_SLABS = N_STATE // LANES
VMEM_LIMIT = 56 * 1024 * 1024


def _rms(x, g):
    ms = jnp.mean(x * x, axis=-1, keepdims=True)
    return x * lax.rsqrt(ms + NORM_EPS) * g


def _const_spec(shape):
    nd = len(shape)
    return pl.BlockSpec(shape, lambda *_: (0,) * nd, pipeline_mode=pl.Buffered(1))


def _layer_spec(stack, layer):
    nd = stack.ndim - 1
    return pl.BlockSpec((None,) + stack.shape[1:], lambda *_: (layer,) + (0,) * nd,
                        pipeline_mode=pl.Buffered(1))


FF_CHUNK = 256


def _ffn_body(x_ref, g_ref, wg_ref, wu_ref, wd_ref, o_ref, a_scr):
    x = x_ref[...]
    xn = _rms(x, g_ref[...]).astype(BF16)
    for c in range(D_FF // FF_CHUNK):
        cols = slice(c * FF_CHUNK, (c + 1) * FF_CHUNK)
        gate = jnp.dot(xn, wg_ref[:, cols], preferred_element_type=F32)
        up = jnp.dot(xn, wu_ref[:, cols], preferred_element_type=F32)
        a_scr[:, cols] = (gate * jax.nn.sigmoid(gate) * up).astype(BF16)
    o_ref[...] = x + jnp.dot(a_scr[...], wd_ref[...], preferred_element_type=F32)


def _ffn_call(x, g, wg, wu, wd, *, tm):
    m = x.shape[0]
    return pl.pallas_call(
        _ffn_body,
        grid=(m // tm,),
        in_specs=[
            pl.BlockSpec((tm, D_MODEL), lambda i: (i, 0)),
            _const_spec((1, D_MODEL)),
            _const_spec(wg.shape),
            _const_spec(wu.shape),
            _const_spec(wd.shape),
        ],
        out_specs=pl.BlockSpec((tm, D_MODEL), lambda i: (i, 0)),
        out_shape=jax.ShapeDtypeStruct((m, D_MODEL), F32),
        scratch_shapes=[pltpu.VMEM((tm, D_FF), BF16)],
        compiler_params=pltpu.CompilerParams(
            dimension_semantics=("arbitrary",), vmem_limit_bytes=VMEM_LIMIT),
        name="ffn",
    )(x, g, wg, wu, wd)


def _ffn_cast_body(x_ref, g_ref, wg_ref, wu_ref, wd_ref, o_ref, wgb_ref, wub_ref, wdb_ref, xn_scr):
    c = pl.program_id(0)

    @pl.when(c == 0)
    def _():
        x = x_ref[...]
        xn_scr[...] = _rms(x, g_ref[...]).astype(BF16)
        o_ref[...] = x

    wg = wg_ref[...].astype(BF16)
    wu = wu_ref[...].astype(BF16)
    wd = wd_ref[...].astype(BF16)
    wgb_ref[...] = wg
    wub_ref[...] = wu
    wdb_ref[...] = wd
    xn = xn_scr[...]
    gate = jnp.dot(xn, wg, preferred_element_type=F32)
    up = jnp.dot(xn, wu, preferred_element_type=F32)
    a = (gate * jax.nn.sigmoid(gate) * up).astype(BF16)
    o_ref[...] += jnp.dot(a, wd, preferred_element_type=F32)


def _ffn_cast_call(x, g, wgu, wd, *, layer):
    m = x.shape[0]
    n_chunks = D_FF // FF_CHUNK
    return pl.pallas_call(
        _ffn_cast_body,
        grid=(n_chunks,),
        in_specs=[
            pl.BlockSpec((m, D_MODEL), lambda c: (0, 0)),
            _const_spec((1, D_MODEL)),
            pl.BlockSpec((None, D_MODEL, FF_CHUNK), lambda c: (layer, 0, c)),
            pl.BlockSpec((None, D_MODEL, FF_CHUNK), lambda c: (layer, 0, n_chunks + c)),
            pl.BlockSpec((None, FF_CHUNK, D_MODEL), lambda c: (layer, c, 0)),
        ],
        out_specs=[
            pl.BlockSpec((m, D_MODEL), lambda c: (0, 0)),
            pl.BlockSpec((D_MODEL, FF_CHUNK), lambda c: (0, c)),
            pl.BlockSpec((D_MODEL, FF_CHUNK), lambda c: (0, c)),
            pl.BlockSpec((FF_CHUNK, D_MODEL), lambda c: (c, 0)),
        ],
        out_shape=[
            jax.ShapeDtypeStruct((m, D_MODEL), F32),
            jax.ShapeDtypeStruct((D_MODEL, D_FF), BF16),
            jax.ShapeDtypeStruct((D_MODEL, D_FF), BF16),
            jax.ShapeDtypeStruct((D_FF, D_MODEL), BF16),
        ],
        scratch_shapes=[pltpu.VMEM((m, D_MODEL), BF16)],
        compiler_params=pltpu.CompilerParams(
            dimension_semantics=("arbitrary",), vmem_limit_bytes=VMEM_LIMIT),
        name="ffn_cast",
    )(x, g, wgu, wgu, wd)


N_GTILES = D_MODEL // MXU_DIM
SLABS_PER_TILE = N_SLABS // N_GTILES


SSM_AHEAD = 1
SSM_ROW_PAD = 1


def _ssm_body(x_ref, g_ref, wb_ref, wc_ref, lam_ref, d_ref, wglu_ref, h0r_ref, h0i_ref,
              o_ref, hr_ref, hi_ref, xs_scr, os_scr, *, bb, tc):
    pitch = tc + SSM_ROW_PAD
    spt = SLABS_PER_TILE
    n_cols = D_MODEL // LANES
    n_rg = bb // SUBLANES

    @pl.when(pl.program_id(1) == 0)
    def _():
        hr_ref[...] = h0r_ref[...]
        hi_ref[...] = h0i_ref[...]

    def gather_idx(rg, t):
        return pl.ds(rg * SUBLANES * pitch + t, SUBLANES, stride=pitch)

    g = g_ref[...]
    for b in range(bb):
        xn = _rms(x_ref[b], g)
        for c in range(n_cols):
            xs_scr[c, b * pitch:b * pitch + tc, :] = xn[:, c * LANES:(c + 1) * LANES]
    u = jnp.concatenate(
        [jnp.concatenate([xs_scr[c, gather_idx(rg, t), :] for c in range(n_cols)], axis=1)
         for rg in range(n_rg) for t in range(tc)], axis=0)
    ub = u.astype(BF16)

    def project_in(kt):
        return jnp.dot(ub[:, kt * MXU_DIM:(kt + 1) * MXU_DIM], wb_ref[kt], preferred_element_type=F32)

    def scan(kt, bu):
        re_cols, im_cols = [], []
        for j in range(spt):
            s = kt * spt + j
            lanes = slice(s * LANES, (s + 1) * LANES)
            lr = lam_ref[s]
            li = lam_ref[N_SLABS + s]
            re_rows, im_rows = [], []
            for rg in range(n_rg):
                seqs = slice(rg * SUBLANES, (rg + 1) * SUBLANES)
                hr = hr_ref[seqs, lanes]
                hi = hi_ref[seqs, lanes]
                for t in range(tc):
                    rows = slice((rg * tc + t) * SUBLANES, (rg * tc + t + 1) * SUBLANES)
                    nr = (lr * hr - li * hi) + bu[rows, (2 * j) * LANES:(2 * j + 1) * LANES]
                    ni = (lr * hi + li * hr) + bu[rows, (2 * j + 1) * LANES:(2 * j + 2) * LANES]
                    hr, hi = nr, ni
                    re_rows.append(nr)
                    im_rows.append(ni)
                hr_ref[seqs, lanes] = hr
                hi_ref[seqs, lanes] = hi
            re_cols.append(jnp.concatenate(re_rows, axis=0))
            im_cols.append(jnp.concatenate(im_rows, axis=0))
        return jnp.concatenate(re_cols + im_cols, axis=1).astype(BF16)

    ys = []
    bu = {kt: project_in(kt) for kt in range(min(SSM_AHEAD, N_GTILES))}
    for kt in range(N_GTILES):
        if kt + SSM_AHEAD < N_GTILES:
            bu[kt + SSM_AHEAD] = project_in(kt + SSM_AHEAD)
        hs = scan(kt, bu.pop(kt))
        ys.append(jnp.dot(hs, wc_ref[kt], preferred_element_type=F32))
    y = jnp.concatenate(ys, axis=-1) + d_ref[...] * u
    z = jax.nn.gelu(y).astype(BF16)
    gl = jnp.dot(z, wglu_ref[...], preferred_element_type=F32)
    out = gl[:, :D_MODEL] * jax.nn.sigmoid(gl[:, D_MODEL:])
    for rg in range(n_rg):
        for t in range(tc):
            rows = slice((rg * tc + t) * SUBLANES, (rg * tc + t + 1) * SUBLANES)
            for c in range(n_cols):
                os_scr[c, gather_idx(rg, t), :] = out[rows, c * LANES:(c + 1) * LANES]
    for b in range(bb):
        o_ref[b] = x_ref[b] + jnp.concatenate(
            [os_scr[c, b * pitch:b * pitch + tc, :] for c in range(n_cols)], axis=1)


def _ssm_call(x, g, wb, wc, lam, d, wglu, h0r, h0i, *, layer, bb, tc):
    bsz, seq, _ = x.shape
    mp = bb * (tc + SSM_ROW_PAD)
    body = functools.partial(_ssm_body, bb=bb, tc=tc)
    return pl.pallas_call(
        body,
        grid=(bsz // bb, seq // tc),
        in_specs=[
            pl.BlockSpec((bb, tc, D_MODEL), lambda i, j: (i, j, 0)),
            _const_spec((1, D_MODEL)),
            _layer_spec(wb, layer),
            _layer_spec(wc, layer),
            _layer_spec(lam, layer),
            _const_spec((1, D_MODEL)),
            _layer_spec(wglu, layer),
            pl.BlockSpec((bb, N_STATE), lambda i, j: (i, 0)),
            pl.BlockSpec((bb, N_STATE), lambda i, j: (i, 0)),
        ],
        out_specs=[
            pl.BlockSpec((bb, tc, D_MODEL), lambda i, j: (i, j, 0)),
            pl.BlockSpec((bb, N_STATE), lambda i, j: (i, 0)),
            pl.BlockSpec((bb, N_STATE), lambda i, j: (i, 0)),
        ],
        out_shape=[
            jax.ShapeDtypeStruct(x.shape, F32),
            jax.ShapeDtypeStruct((bsz, N_STATE), F32),
            jax.ShapeDtypeStruct((bsz, N_STATE), F32),
        ],
        scratch_shapes=[pltpu.VMEM((D_MODEL // LANES, mp, LANES), F32),
                        pltpu.VMEM((D_MODEL // LANES, mp, LANES), F32)],
        compiler_params=pltpu.CompilerParams(
            dimension_semantics=("arbitrary", "arbitrary"), vmem_limit_bytes=VMEM_LIMIT),
        name="ssm",
    )(x, g, wb, wc, lam, d, wglu, h0r, h0i)


def _qk_norm_rope(qk, ones_bd, gain, cos, sina, sinb):
    n_cols = qk.shape[1] // LANES
    sq = qk * qk
    hi = sq.astype(BF16)
    lo = (sq - hi.astype(F32)).astype(BF16)
    outs = []
    for c in range(n_cols // 2):
        sl = slice(c * MXU_DIM, (c + 1) * MXU_DIM)
        ss = (jnp.dot(hi[:, sl], ones_bd, preferred_element_type=F32)
              + jnp.dot(lo[:, sl], ones_bd, preferred_element_type=F32))
        outs.append(qk[:, sl] * lax.rsqrt(ss * (1.0 / HEAD_DIM) + NORM_EPS) * gain[:, sl])
    qn = jnp.concatenate(outs, axis=-1)
    cols = []
    for c in range(n_cols):
        xc = qn[:, c * LANES:(c + 1) * LANES]
        cols.append(xc * cos + pltpu.roll(xc, ROT_DIM // 2, 1) * sina
                    + pltpu.roll(xc, LANES - ROT_DIM // 2, 1) * sinb)
    return jnp.concatenate(cols, axis=-1)


def _lane_halves(x):
    lane = lax.broadcasted_iota(jnp.int32, x.shape, 1)
    low = lane < HEAD_DIM
    return jnp.where(low, x, 0.0), jnp.where(low, 0.0, x)


TQ = 256
ATTN_AHEAD = 2


def _attn_prompt_body(x_ref, g_ref, wqkv_ref, ones_ref, gain_ref, cos_ref, sina_ref, sinb_ref,
                      gaint_ref, cost_ref, sint_ref, bias_ref, sink_ref, wo_ref, o_ref, kc_ref, vc_ref,
                      qt_cur, ka_cur, kb_cur, vt_cur, ot_scr, xprev_scr, qt_nxt, ka_nxt, kb_nxt, vt_nxt,
                      ka_prev, kb_prev, vt_prev, *, blocks_per_seq):
    g = pl.program_id(0)

    @pl.when(g == 0)
    def _():
        for scr in (ka_cur, kb_cur, ka_prev, kb_prev):
            scr[...] = jnp.zeros(scr.shape, BF16)
        for scr in (vt_cur, vt_prev):
            scr[:, 0:HEAD_DIM, :] = jnp.zeros((N_KV_HEADS, HEAD_DIM, scr.shape[2]), BF16)
            scr[:, HEAD_DIM:, :] = jnp.ones((N_KV_HEADS, HEAD_DIM, scr.shape[2]), BF16)
        qt_cur[...] = jnp.zeros((D_MODEL, TQ), BF16)
        xprev_scr[...] = jnp.zeros((TQ, D_MODEL), F32)

    bias_mid = jnp.concatenate([bias_ref[...], bias_ref[...]], axis=1)
    first = (g - 1) % blocks_per_seq == 0
    key_row = lax.broadcasted_iota(jnp.int32, bias_mid.shape, 0)
    bias_first = jnp.where(jnp.logical_and(first, key_row < WINDOW), NEG_INF, bias_mid)
    lane = lax.broadcasted_iota(jnp.int32, (1, 2 * WINDOW), 1)
    pair = 2 * HEAD_DIM

    def key_window(j, h, prev, cur):
        if j == 0:
            return [prev[h], cur[h, 0:WINDOW, :]]
        return [cur[h, (j - 1) * WINDOW:(j + 1) * WINDOW, :]]

    def value_window(j, h):
        if j == 0:
            return jnp.concatenate([vt_prev[h], vt_cur[h, :, 0:WINDOW]], axis=1)
        return vt_cur[h, :, (j - 1) * WINDOW:(j + 1) * WINDOW]

    def scores(j, h):
        qpos = slice(j * WINDOW, (j + 1) * WINDOW)
        qs = jnp.concatenate([qt_cur[(2 * h) * pair:(2 * h + 1) * pair, qpos],
                              qt_cur[(2 * h + 1) * pair:(2 * h + 2) * pair, qpos]], axis=1)
        kk = jnp.concatenate(key_window(j, h, ka_prev, ka_cur) + key_window(j, h, kb_prev, kb_cur),
                             axis=0)
        return jnp.dot(kk, qs, preferred_element_type=F32)

    def softmax(j, h, st):
        bias = bias_first if j == 0 else bias_mid
        out = []
        for par in range(2):
            s = st[par * 2 * WINDOW:(par + 1) * 2 * WINDOW, :] + bias
            sink = jnp.where(lane < WINDOW, sink_ref[4 * h + par], sink_ref[4 * h + 2 + par])
            m = jnp.maximum(jnp.max(s, axis=0, keepdims=True), sink)
            out.append((jnp.exp(s - m).astype(BF16), jnp.exp(sink - m)))
        return out

    def weighted_values(j, h, probs):
        qpos = slice(j * WINDOW, (j + 1) * WINDOW)
        vth = value_window(j, h)
        for par, (p, sink_term) in enumerate(probs):
            nd = jnp.dot(vth, p, preferred_element_type=F32)
            denom = nd[HEAD_DIM:, :] + sink_term
            on = nd[:HEAD_DIM, :] * (1.0 / denom)
            for rp in range(2):
                head = 4 * h + 2 * rp + par
                ot_scr[head * HEAD_DIM:(head + 1) * HEAD_DIM, qpos] = on[:, rp * WINDOW:(rp + 1) * WINDOW]

    x = x_ref[0]
    xn = _rms(x, g_ref[...]).astype(BF16)
    tabs = (cos_ref[...], sina_ref[...], sinb_ref[...])
    cos_t = cost_ref[...]
    sin_t = sint_ref[...]
    half = ROT_DIM // 2

    def project(c):
        cols = slice(c * MXU_DIM, (c + 1) * MXU_DIM)
        t = jnp.dot(xn, wqkv_ref[:, cols], preferred_element_type=F32)
        if c * MXU_DIM < D_MODEL:
            tt = t.T
            heads = []
            for hl in range(MXU_DIM // HEAD_DIM):
                blk = tt[hl * HEAD_DIM:(hl + 1) * HEAD_DIM, :]
                ms = jnp.sum(blk * blk, axis=0, keepdims=True) * (1.0 / HEAD_DIM)
                gcol = gaint_ref[c * MXU_DIM + hl * HEAD_DIM:c * MXU_DIM + (hl + 1) * HEAD_DIM, :]
                y = blk * lax.rsqrt(ms + NORM_EPS) * jnp.concatenate([gcol] * (TQ // LANES), axis=1)
                x1, x2 = y[:half], y[half:ROT_DIM]
                heads.append(jnp.concatenate(
                    [x1 * cos_t - x2 * sin_t, x2 * cos_t + x1 * sin_t, y[ROT_DIM:]], axis=0))
            qt_nxt[cols, :] = jnp.concatenate(heads, axis=0).astype(BF16)
        elif c * MXU_DIM < D_QK:
            t = _qk_norm_rope(t, ones_ref[...], gain_ref[:, cols], *tabs)
            kc_ref[0] = t[TQ - WINDOW:, :]
            for h in range(N_KV_HEADS):
                lo_part, hi_part = _lane_halves(t[:, (h // 2) * LANES:(h // 2 + 1) * LANES])
                if h % 2 == 0:
                    a, b = lo_part, pltpu.roll(lo_part, HEAD_DIM, 1)
                else:
                    a, b = pltpu.roll(hi_part, HEAD_DIM, 1), hi_part
                ka_nxt[h] = a.astype(BF16)
                kb_nxt[h] = b.astype(BF16)
        else:
            vc_ref[0] = t[TQ - WINDOW:, :]
            vt = t.T
            for h in range(N_KV_HEADS):
                vt_nxt[h] = vt[h * HEAD_DIM:(h + 1) * HEAD_DIM, :].astype(BF16)

    groups = [(j, h) for j in range(TQ // WINDOW) for h in range(N_KV_HEADS)]
    n_groups = len(groups)
    st, probs = {}, {}
    for n in range(n_groups + ATTN_AHEAD):
        if n < n_groups:
            st[n] = scores(*groups[n])
        if n < D_QKV // MXU_DIM:
            project(n)
        d = n - ATTN_AHEAD + 1
        if 0 <= d < n_groups:
            probs[d] = softmax(*groups[d], st.pop(d))
        d = n - ATTN_AHEAD
        if 0 <= d < n_groups:
            weighted_values(*groups[d], probs.pop(d))
    attn = ot_scr[...].T.astype(BF16)
    o_ref[0] = xprev_scr[...] + jnp.dot(attn, wo_ref[...], preferred_element_type=F32)

    xprev_scr[...] = x
    qt_cur[...] = qt_nxt[...]
    for prev, cur, nxt in ((ka_prev, ka_cur, ka_nxt), (kb_prev, kb_cur, kb_nxt)):
        prev[...] = cur[:, TQ - WINDOW:, :]
        cur[...] = nxt[...]
    vt_prev[:, 0:HEAD_DIM, :] = vt_cur[:, 0:HEAD_DIM, TQ - WINDOW:]
    vt_cur[:, 0:HEAD_DIM, :] = vt_nxt[...]


def _attn_prompt_call(x, g, wqkv, ones_bd, gain, cos, sina, sinb, bias, sinks, wo, *, layer):
    bsz, seq, _ = x.shape
    bps = seq // TQ
    n_blocks = bsz * bps
    k_blk = pltpu.VMEM((N_KV_HEADS, TQ, LANES), BF16)
    k_tail = pltpu.VMEM((N_KV_HEADS, WINDOW, LANES), BF16)
    vt_blk = pltpu.VMEM((N_KV_HEADS, 2 * HEAD_DIM, TQ), BF16)
    vt_tail = pltpu.VMEM((N_KV_HEADS, 2 * HEAD_DIM, WINDOW), BF16)

    def proj_block(gi):
        blk = jnp.minimum(gi, n_blocks - 1)
        return blk // bps, blk % bps

    def attn_block(gi):
        blk = jnp.maximum(gi - 1, 0)
        return blk // bps, blk % bps

    tab_spec = pl.BlockSpec((TQ, LANES), lambda gi: (proj_block(gi)[1], 0))
    cache_spec = pl.BlockSpec((1, WINDOW, D_KV), lambda gi: (proj_block(gi)[0], 0, 0))
    half = ROT_DIM // 2
    gain_t = jnp.broadcast_to(gain[0, :D_MODEL, None], (D_MODEL, LANES))
    cos_t = cos[:, :half].T
    sin_t = sina[:, half:ROT_DIM].T
    tab_t_spec = pl.BlockSpec((half, TQ), lambda gi: (0, proj_block(gi)[1]))
    body = functools.partial(_attn_prompt_body, blocks_per_seq=bps)
    return pl.pallas_call(
        body,
        grid=(n_blocks + 1,),
        in_specs=[
            pl.BlockSpec((1, TQ, D_MODEL), lambda gi: (*proj_block(gi), 0)),
            _const_spec((1, D_MODEL)),
            _layer_spec(wqkv, layer),
            _const_spec((MXU_DIM, MXU_DIM)),
            _const_spec((1, D_QK)),
            tab_spec, tab_spec, tab_spec,
            _const_spec((D_MODEL, LANES)),
            tab_t_spec, tab_t_spec,
            _const_spec((2 * WINDOW, WINDOW)),
            pl.BlockSpec(memory_space=pltpu.SMEM),
            _layer_spec(wo, layer),
        ],
        out_specs=[
            pl.BlockSpec((1, TQ, D_MODEL), lambda gi: (*attn_block(gi), 0)),
            cache_spec, cache_spec,
        ],
        out_shape=[
            jax.ShapeDtypeStruct(x.shape, F32),
            jax.ShapeDtypeStruct((bsz, WINDOW, D_KV), F32),
            jax.ShapeDtypeStruct((bsz, WINDOW, D_KV), F32),
        ],
        scratch_shapes=[pltpu.VMEM((D_MODEL, TQ), BF16), k_blk, k_blk, vt_blk,
                        pltpu.VMEM((D_MODEL, TQ), F32),
                        pltpu.VMEM((TQ, D_MODEL), F32),
                        pltpu.VMEM((D_MODEL, TQ), BF16), k_blk, k_blk,
                        pltpu.VMEM((N_KV_HEADS, HEAD_DIM, TQ), BF16),
                        k_tail, k_tail, vt_tail],
        compiler_params=pltpu.CompilerParams(
            dimension_semantics=("arbitrary",), vmem_limit_bytes=VMEM_LIMIT),
        name="attn_prompt",
    )(x, g, wqkv, ones_bd, gain, cos, sina, sinb, gain_t, cos_t, sin_t, bias, sinks, wo)


SEQ_BLOCK = 16
SAMPLE_AHEAD = 4
N_KEYS_PAD = 2 * WINDOW


def _attn_sample_body(x_ref, ck_ref, cv_ref, g_ref, wqkv_ref, ones_ref, gain_ref, cos_ref, sina_ref,
                      sinb_ref, bias_ref, sink_ref, wo_ref, o_ref, kc_ref, vc_ref,
                      kf_scr, vf_scr, oall_scr, *, s_len):
    rows = SEQ_BLOCK * s_len
    w_keep = WINDOW - s_len

    @pl.when(pl.program_id(0) == 0)
    def _():
        pad = (SEQ_BLOCK, N_KEYS_PAD - WINDOW, D_KV)
        kf_scr[:, WINDOW:, :] = jnp.zeros(pad, BF16)
        vf_scr[:, WINDOW:, :] = jnp.zeros(pad, BF16)

    x = x_ref[...].reshape(rows, D_MODEL)
    xn = _rms(x, g_ref[...]).astype(BF16)
    qkv = jnp.dot(xn, wqkv_ref[...], preferred_element_type=F32)
    qk = _qk_norm_rope(qkv[:, :D_QK], ones_ref[...], gain_ref[...],
                       cos_ref[...], sina_ref[...], sinb_ref[...])
    q = qk[:, :D_MODEL]
    k_new = qk[:, D_MODEL:].reshape(SEQ_BLOCK, s_len, D_KV)
    v_new = qkv[:, D_QK:].reshape(SEQ_BLOCK, s_len, D_KV)

    ck = ck_ref[...]
    cv = cv_ref[...]
    kc_ref[:, 0:w_keep, :] = ck[:, s_len:, :]
    vc_ref[:, 0:w_keep, :] = cv[:, s_len:, :]
    kc_ref[:, w_keep:, :] = k_new
    vc_ref[:, w_keep:, :] = v_new
    kf_scr[:, 0:WINDOW, :] = ck.astype(BF16)
    vf_scr[:, 0:WINDOW, :] = cv.astype(BF16)
    tile = 2 * SUBLANES
    zpad = jnp.zeros((SEQ_BLOCK, tile - s_len, D_KV), F32)
    kf_scr[:, WINDOW:WINDOW + tile, :] = jnp.concatenate([k_new, zpad], axis=1).astype(BF16)
    vf_scr[:, WINDOW:WINDOW + tile, :] = jnp.concatenate([v_new, zpad], axis=1).astype(BF16)

    bias = bias_ref[...]
    sink = sink_ref[...]
    nt_dims = (((1,), (1,)), ((), ()))
    zero_col = jnp.zeros((s_len, LANES), F32)
    def scores(s):
        qs = q[s * s_len:(s + 1) * s_len, :]
        tiles = []
        for h in range(N_KV_HEADS):
            for r in range(KV_REP):
                head = KV_REP * h + r
                src = qs[:, (head // 2) * LANES:(head // 2 + 1) * LANES]
                lo_part, hi_part = _lane_halves(src)
                part = lo_part if head % 2 == 0 else hi_part
                if head % 2 != h % 2:
                    part = pltpu.roll(part, HEAD_DIM, 1)
                tiles.append(jnp.concatenate([part, zero_col] if h < 2 else [zero_col, part], axis=1))
        wt = jnp.concatenate(tiles, axis=0).astype(BF16)
        return lax.dot_general(wt, kf_scr[s], nt_dims, preferred_element_type=F32)

    def softmax(sc):
        sc = sc + bias
        m = jnp.maximum(jnp.max(sc, axis=-1, keepdims=True), sink)
        p = jnp.exp(sc - m)
        denom = jnp.sum(p, axis=-1, keepdims=True) + jnp.exp(sink - m)
        return p.astype(BF16), 1.0 / denom

    def weighted_values(s, p, inv_denom):
        o = jnp.dot(p, vf_scr[s], preferred_element_type=F32) * inv_denom
        cols = []
        for c in range(N_HEADS // 2):
            pieces = []
            for head in (2 * c, 2 * c + 1):
                h, r = divmod(head, KV_REP)
                blk = o[(KV_REP * h + r) * s_len:(KV_REP * h + r + 1) * s_len,
                        (h // 2) * LANES:(h // 2 + 1) * LANES]
                lo_part, hi_part = _lane_halves(blk)
                part = lo_part if h % 2 == 0 else hi_part
                if h % 2 != head % 2:
                    part = pltpu.roll(part, HEAD_DIM, 1)
                pieces.append(part)
            cols.append(pieces[0] + pieces[1])
        oall_scr[s * s_len:(s + 1) * s_len, :] = jnp.concatenate(cols, axis=-1)

    sc, probs = {}, {}
    for n in range(SEQ_BLOCK + SAMPLE_AHEAD):
        if n < SEQ_BLOCK:
            sc[n] = scores(n)
        d = n - SAMPLE_AHEAD // 2
        if 0 <= d < SEQ_BLOCK:
            probs[d] = softmax(sc.pop(d))
        d = n - SAMPLE_AHEAD
        if 0 <= d < SEQ_BLOCK:
            weighted_values(d, *probs.pop(d))

    out = jnp.dot(oall_scr[...].astype(BF16), wo_ref[...], preferred_element_type=F32)
    o_ref[...] = (x + out).reshape(SEQ_BLOCK, s_len, D_MODEL)


def _attn_sample_call(x, ck, cv, g, wqkv, ones_bd, gain, cos, sina, sinb, bias, sink_rows, wo, *, layer):
    bsz, s_len, _ = x.shape
    rows = SEQ_BLOCK * s_len
    body = functools.partial(_attn_sample_body, s_len=s_len)
    seq_spec = lambda shape: pl.BlockSpec(shape, lambda i: (i, 0, 0))
    cache_spec = pl.BlockSpec((None, SEQ_BLOCK, WINDOW, D_KV), lambda i: (layer, i, 0, 0))
    return pl.pallas_call(
        body,
        grid=(bsz // SEQ_BLOCK,),
        in_specs=[
            seq_spec((SEQ_BLOCK, s_len, D_MODEL)),
            cache_spec,
            cache_spec,
            _const_spec((1, D_MODEL)),
            _layer_spec(wqkv, layer),
            _const_spec((MXU_DIM, MXU_DIM)),
            _const_spec((1, D_QK)),
            _const_spec((rows, LANES)),
            _const_spec((rows, LANES)),
            _const_spec((rows, LANES)),
            _const_spec((N_HEADS * s_len, N_KEYS_PAD)),
            _const_spec((N_HEADS * s_len, 1)),
            _layer_spec(wo, layer),
        ],
        out_specs=[
            seq_spec((SEQ_BLOCK, s_len, D_MODEL)),
            seq_spec((SEQ_BLOCK, WINDOW, D_KV)),
            seq_spec((SEQ_BLOCK, WINDOW, D_KV)),
        ],
        out_shape=[
            jax.ShapeDtypeStruct(x.shape, F32),
            jax.ShapeDtypeStruct(ck.shape[1:], F32),
            jax.ShapeDtypeStruct(cv.shape[1:], F32),
        ],
        scratch_shapes=[
            pltpu.VMEM((SEQ_BLOCK, N_KEYS_PAD, D_KV), BF16),
            pltpu.VMEM((SEQ_BLOCK, N_KEYS_PAD, D_KV), BF16),
            pltpu.VMEM((rows, D_MODEL), F32),
        ],
        compiler_params=pltpu.CompilerParams(
            dimension_semantics=("arbitrary",), vmem_limit_bytes=VMEM_LIMIT),
        name="attn_sample",
    )(x, ck, cv, g, wqkv, ones_bd, gain, cos, sina, sinb, bias, sink_rows, wo)


def _ssm_weights(a_re, a_im, log_dt, b_re, b_im, c_re, c_im):
    nl = a_re.shape[0]
    dt = jnp.exp(log_dt)[..., None]
    mag = jnp.exp(a_re * dt)
    lam_re = mag * jnp.cos(a_im * dt)
    lam_im = mag * jnp.sin(a_im * dt)
    den = a_re * a_re + a_im * a_im
    nr = lam_re - 1.0
    ni = lam_im
    f_re = (nr * a_re + ni * a_im) / den
    f_im = (ni * a_re - nr * a_im) / den
    bb_re = f_re[..., None] * b_re - f_im[..., None] * b_im
    bb_im = f_re[..., None] * b_im + f_im[..., None] * b_re
    gpt = MXU_DIM // SSM_GROUP
    n_st = gpt * STATE_DIM
    same_group = jnp.asarray(
        (np.arange(MXU_DIM)[:, None] // SSM_GROUP) == (np.arange(n_st)[None, :] // STATE_DIM))

    def b_tiles(bb):
        t = jnp.swapaxes(bb, 2, 3).reshape(nl, N_GTILES, MXU_DIM, STATE_DIM)
        return jnp.where(same_group, jnp.tile(t, (1, 1, 1, gpt)), 0.0)

    def c_tiles(cc):
        t = jnp.swapaxes(cc, 2, 3).reshape(nl, N_GTILES, n_st, SSM_GROUP)
        return jnp.where(same_group.T, jnp.tile(t, (1, 1, 1, gpt)), 0.0)

    wb_re = b_tiles(bb_re).astype(BF16)
    wb_im = b_tiles(bb_im).astype(BF16)
    wb = jnp.concatenate([part[..., j * LANES:(j + 1) * LANES]
                          for j in range(SLABS_PER_TILE) for part in (wb_re, wb_im)], axis=-1)
    wc = jnp.concatenate([c_tiles(c_re), -c_tiles(c_im)], axis=2).astype(BF16)
    lam = jnp.concatenate([lam_re.reshape(nl, N_SLABS, 1, LANES), lam_im.reshape(nl, N_SLABS, 1, LANES)], axis=1)
    lam = jnp.broadcast_to(lam, (nl, 2 * N_SLABS, SUBLANES, LANES))
    return wb, wc, lam


def _rope_tables(pos):
    half = ROT_DIM // 2
    inv_freq = ROPE_THETA ** (-jnp.arange(half, dtype=F32) * 2.0 / ROT_DIM)
    ang = pos[:, None] * inv_freq[None, :]
    cos, sin = jnp.cos(ang), jnp.sin(ang)
    n = pos.shape[0]
    ones = jnp.ones((n, HEAD_DIM - ROT_DIM), F32)
    zeros = jnp.zeros((n, HEAD_DIM - ROT_DIM), F32)
    zh = jnp.zeros((n, half), F32)
    cos_h = jnp.concatenate([cos, cos, ones], axis=1)
    sina_h = jnp.concatenate([zh, sin, zeros], axis=1)
    sinb_h = jnp.concatenate([-sin, zh, zeros], axis=1)
    return tuple(jnp.concatenate([t, t], axis=1) for t in (cos_h, sina_h, sinb_h))


def _ones_block_diag():
    idx = np.arange(MXU_DIM) // HEAD_DIM
    return jnp.asarray(idx[:, None] == idx[None, :], dtype=BF16)


def _prompt_bias():
    js = np.arange(2 * WINDOW)[:, None]
    iq = np.arange(WINDOW)[None, :]
    diff = WINDOW + iq - js
    return jnp.asarray(np.where((diff >= 0) & (diff < WINDOW), 0.0, NEG_INF), dtype=F32)


def _sample_bias(s_len):
    t = (np.arange(N_HEADS * s_len) % s_len)[:, None]
    key = np.arange(N_KEYS_PAD)[None, :]
    diff = np.where(key < WINDOW, WINDOW + t - key, t - (key - WINDOW))
    ok = (diff >= 0) & (diff < WINDOW) & (key < WINDOW + s_len)
    return jnp.asarray(np.where(ok, 0.0, NEG_INF), dtype=F32)


SSM_PROMPT_TC = 64
SSM_SAMPLE_BB = 32
FFN_TM = 1024


def kernel(x_prompt, x_sample, state_ssm_re, state_ssm_im, cache_swa_k, cache_swa_v, norm_mix, norm_ffn, ssm_a_re, ssm_a_im, ssm_log_dt, ssm_b_re, ssm_b_im, ssm_c_re, ssm_c_im, ssm_d, ssm_w_glu, attn_w_qkv, attn_q_norm, attn_k_norm, attn_sinks, attn_w_o, ffn_w_gate_up, ffn_w_down):
    bsz, seq, _ = x_prompt.shape
    dbsz, dseq, _ = x_sample.shape
    yp, ys = x_prompt, x_sample
    p_re, p_im, p_k, p_v, s_re, s_im, s_k, s_v = ([] for _ in range(8))

    ones_bd = _ones_block_diag()
    p_tabs = _rope_tables(jnp.arange(seq, dtype=F32))
    s_pos = PAST_LEN + jnp.arange(dseq, dtype=F32)
    s_tabs = tuple(jnp.tile(t, (SEQ_BLOCK, 1)) for t in _rope_tables(s_pos))
    p_bias = _prompt_bias()
    s_bias = _sample_bias(dseq)
    h0 = jnp.zeros((bsz, N_STATE), F32)

    wb, wc, lam = _ssm_weights(ssm_a_re, ssm_a_im, ssm_log_dt, ssm_b_re, ssm_b_im, ssm_c_re, ssm_c_im)
    wglu = ssm_w_glu.astype(BF16)
    wqkv = attn_w_qkv.astype(BF16)
    wo = attn_w_o.astype(BF16)
    n_swa = cache_swa_k.shape[0]
    ck_all = cache_swa_k.reshape(n_swa, dbsz, WINDOW, D_KV)
    cv_all = cache_swa_v.reshape(n_swa, dbsz, WINDOW, D_KV)

    for i in range(DEPTH):
        j = i // 2
        g_mix = norm_mix[i].reshape(1, D_MODEL)
        if i % 2 == 0:
            d = ssm_d[j].reshape(1, D_MODEL)
            yp, hr, hi = _ssm_call(yp, g_mix, wb, wc, lam, d, wglu, h0, h0,
                                   layer=j, bb=bsz, tc=SSM_PROMPT_TC)
            ys, sr, si = _ssm_call(ys, g_mix, wb, wc, lam, d, wglu,
                                   state_ssm_re[j].reshape(dbsz, N_STATE), state_ssm_im[j].reshape(dbsz, N_STATE),
                                   layer=j, bb=SSM_SAMPLE_BB, tc=dseq)
            shp = (N_GROUPS, STATE_DIM)
            p_re.append(hr.reshape(bsz, *shp))
            p_im.append(hi.reshape(bsz, *shp))
            s_re.append(sr.reshape(dbsz, *shp))
            s_im.append(si.reshape(dbsz, *shp))
        else:
            gain = jnp.concatenate([jnp.tile(attn_q_norm[j] * ATTN_SCALE, N_HEADS),
                                    jnp.tile(attn_k_norm[j], N_KV_HEADS)]).reshape(1, D_QK)
            sinks = attn_sinks[j]
            yp, kp, vp = _attn_prompt_call(yp, g_mix, wqkv, ones_bd, gain, *p_tabs, p_bias, sinks, wo, layer=j)
            sink_rows = jnp.repeat(sinks, dseq).reshape(N_HEADS * dseq, 1)
            ys, ks_, vs_ = _attn_sample_call(ys, ck_all, cv_all, g_mix, wqkv, ones_bd, gain, *s_tabs, s_bias,
                                             sink_rows, wo, layer=j)
            kv_shape = (WINDOW, N_KV_HEADS, HEAD_DIM)
            p_k.append(kp.reshape(bsz, *kv_shape))
            p_v.append(vp.reshape(bsz, *kv_shape))
            s_k.append(ks_.reshape(dbsz, *kv_shape))
            s_v.append(vs_.reshape(dbsz, *kv_shape))
        g_ffn = norm_ffn[i].reshape(1, D_MODEL)
        ys, wg, wu, wd = _ffn_cast_call(ys.reshape(dbsz * dseq, D_MODEL), g_ffn, ffn_w_gate_up, ffn_w_down, layer=i)
        ys = ys.reshape(dbsz, dseq, D_MODEL)
        yp = _ffn_call(yp.reshape(bsz * seq, D_MODEL), g_ffn, wg, wu, wd, tm=FFN_TM).reshape(bsz, seq, D_MODEL)

    return (yp, ys, jnp.stack(p_re), jnp.stack(p_im), jnp.stack(p_k), jnp.stack(p_v),
            jnp.stack(s_re), jnp.stack(s_im), jnp.stack(s_k), jnp.stack(s_v))
```

```python
import functools

import jax
import jax.numpy as jnp
import numpy as np
from jax import lax
from jax.experimental import pallas as pl
from jax.experimental.pallas import tpu as pltpu

F32 = jnp.float32
BF16 = jnp.bfloat16

D_MODEL = 1024
DEPTH = 4
PAST_LEN = 8192
SSM_GROUP = 16
N_GROUPS = D_MODEL // SSM_GROUP
STATE_DIM = 64
N_STATE = N_GROUPS * STATE_DIM
HEAD_DIM = 64
N_HEADS = D_MODEL // HEAD_DIM
N_KV_HEADS = 4
KV_REP = N_HEADS // N_KV_HEADS
D_KV = N_KV_HEADS * HEAD_DIM
D_QK = D_MODEL + D_KV
D_QKV = D_MODEL + 2 * D_KV
WINDOW = 128
ROT_DIM = HEAD_DIM // 4
ROPE_THETA = 500000.0
ATTN_SCALE = HEAD_DIM ** -0.5
D_FF = 2816
NORM_EPS = 1e-6
NEG_INF = -1e30

LANES = 128
SUBLANES = 8
MXU_DIM = 256
N_SLABS = N_STATE // LANES
VMEM_LIMIT = 56 * 1024 * 1024


def _rms(x, g):
    ms = jnp.mean(x * x, axis=-1, keepdims=True)
    return x * lax.rsqrt(ms + NORM_EPS) * g


def _const_spec(shape):
    nd = len(shape)
    return pl.BlockSpec(shape, lambda *_: (0,) * nd, pipeline_mode=pl.Buffered(1))


def _layer_spec(stack, layer):
    nd = stack.ndim - 1
    return pl.BlockSpec((None,) + stack.shape[1:], lambda *_: (layer,) + (0,) * nd,
                        pipeline_mode=pl.Buffered(1))


FF_CHUNK = 256


def _ffn_body(x_ref, g_ref, wg_ref, wu_ref, wd_ref, o_ref, a_scr):
    x = x_ref[...]
    xn = _rms(x, g_ref[...]).astype(BF16)
    for c in range(D_FF // FF_CHUNK):
        gate = jnp.dot(xn, wg_ref[c], preferred_element_type=F32)
        up = jnp.dot(xn, wu_ref[c], preferred_element_type=F32)
        a_scr[:, c * FF_CHUNK:(c + 1) * FF_CHUNK] = (gate * jax.nn.sigmoid(gate) * up).astype(BF16)
    o_ref[...] = x + jnp.dot(a_scr[...], wd_ref[...], preferred_element_type=F32)


def _ffn_call(x, g, wg, wu, wd, *, tm):
    m = x.shape[0]
    return pl.pallas_call(
        _ffn_body,
        grid=(m // tm,),
        in_specs=[
            pl.BlockSpec((tm, D_MODEL), lambda i: (i, 0)),
            _const_spec((1, D_MODEL)),
            _const_spec(wg.shape),
            _const_spec(wu.shape),
            _const_spec(wd.shape),
        ],
        out_specs=pl.BlockSpec((tm, D_MODEL), lambda i: (i, 0)),
        out_shape=jax.ShapeDtypeStruct((m, D_MODEL), F32),
        scratch_shapes=[pltpu.VMEM((tm, D_FF), BF16)],
        compiler_params=pltpu.CompilerParams(
            dimension_semantics=("arbitrary",), vmem_limit_bytes=VMEM_LIMIT),
        name="ffn",
    )(x, g, wg, wu, wd)


def _ffn_cast_body(x_ref, g_ref, wg_ref, wu_ref, wd_ref, o_ref, wgb_ref, wub_ref, wdb_ref, xn_scr):
    c = pl.program_id(0)

    @pl.when(c == 0)
    def _():
        x = x_ref[...]
        xn_scr[...] = _rms(x, g_ref[...]).astype(BF16)
        o_ref[...] = x

    wg = wg_ref[...].astype(BF16)
    wu = wu_ref[...].astype(BF16)
    wd = wd_ref[...].astype(BF16)
    wgb_ref[...] = wg
    wub_ref[...] = wu
    wdb_ref[...] = wd
    xn = xn_scr[...]
    gate = jnp.dot(xn, wg, preferred_element_type=F32)
    up = jnp.dot(xn, wu, preferred_element_type=F32)
    a = (gate * jax.nn.sigmoid(gate) * up).astype(BF16)
    o_ref[...] += jnp.dot(a, wd, preferred_element_type=F32)


def _ffn_cast_call(x, g, wgu, wd, *, layer):
    m = x.shape[0]
    n_chunks = D_FF // FF_CHUNK
    return pl.pallas_call(
        _ffn_cast_body,
        grid=(n_chunks,),
        in_specs=[
            pl.BlockSpec((m, D_MODEL), lambda c: (0, 0)),
            _const_spec((1, D_MODEL)),
            pl.BlockSpec((None, D_MODEL, FF_CHUNK), lambda c: (layer, 0, c)),
            pl.BlockSpec((None, D_MODEL, FF_CHUNK), lambda c: (layer, 0, n_chunks + c)),
            pl.BlockSpec((None, FF_CHUNK, D_MODEL), lambda c: (layer, c, 0)),
        ],
        out_specs=[
            pl.BlockSpec((m, D_MODEL), lambda c: (0, 0)),
            pl.BlockSpec((None, D_MODEL, FF_CHUNK), lambda c: (c, 0, 0)),
            pl.BlockSpec((None, D_MODEL, FF_CHUNK), lambda c: (c, 0, 0)),
            pl.BlockSpec((FF_CHUNK, D_MODEL), lambda c: (c, 0)),
        ],
        out_shape=[
            jax.ShapeDtypeStruct((m, D_MODEL), F32),
            jax.ShapeDtypeStruct((n_chunks, D_MODEL, FF_CHUNK), BF16),
            jax.ShapeDtypeStruct((n_chunks, D_MODEL, FF_CHUNK), BF16),
            jax.ShapeDtypeStruct((D_FF, D_MODEL), BF16),
        ],
        scratch_shapes=[pltpu.VMEM((m, D_MODEL), BF16)],
        compiler_params=pltpu.CompilerParams(
            dimension_semantics=("arbitrary",), vmem_limit_bytes=VMEM_LIMIT),
        name="ffn_cast",
    )(x, g, wgu, wgu, wd)


N_GTILES = D_MODEL // MXU_DIM
SLABS_PER_TILE = N_SLABS // N_GTILES


SSM_AHEAD = 1
SSM_ROW_PAD = 1


def _ssm_body(x_ref, g_ref, wb_ref, wc_ref, lam_ref, d_ref, wglu_ref, h0r_ref, h0i_ref,
              o_ref, hr_ref, hi_ref, xs_scr, os_scr, *, bb, tc):
    pitch = tc + SSM_ROW_PAD
    spt = SLABS_PER_TILE
    n_cols = D_MODEL // LANES
    n_rg = bb // SUBLANES

    @pl.when(pl.program_id(1) == 0)
    def _():
        hr_ref[...] = h0r_ref[...]
        hi_ref[...] = h0i_ref[...]

    def gather_idx(rg, t):
        return pl.ds(rg * SUBLANES * pitch + t, SUBLANES, stride=pitch)

    g = g_ref[...]
    for b in range(bb):
        xn = _rms(x_ref[b], g)
        for c in range(n_cols):
            xs_scr[c, b * pitch:b * pitch + tc, :] = xn[:, c * LANES:(c + 1) * LANES]
    u = jnp.concatenate(
        [jnp.concatenate([xs_scr[c, gather_idx(rg, t), :] for c in range(n_cols)], axis=1)
         for rg in range(n_rg) for t in range(tc)], axis=0)
    ub = u.astype(BF16)

    def project_in(kt):
        return jnp.dot(ub[:, kt * MXU_DIM:(kt + 1) * MXU_DIM], wb_ref[kt], preferred_element_type=F32)

    def scan(kt, bu):
        re_cols, im_cols = [], []
        for j in range(spt):
            s = kt * spt + j
            lanes = slice(s * LANES, (s + 1) * LANES)
            lr = lam_ref[s]
            li = lam_ref[N_SLABS + s]
            re_rows, im_rows = [], []
            for rg in range(n_rg):
                seqs = slice(rg * SUBLANES, (rg + 1) * SUBLANES)
                hr = hr_ref[seqs, lanes]
                hi = hi_ref[seqs, lanes]
                for t in range(tc):
                    rows = slice((rg * tc + t) * SUBLANES, (rg * tc + t + 1) * SUBLANES)
                    nr = (lr * hr - li * hi) + bu[rows, (2 * j) * LANES:(2 * j + 1) * LANES]
                    ni = (lr * hi + li * hr) + bu[rows, (2 * j + 1) * LANES:(2 * j + 2) * LANES]
                    hr, hi = nr, ni
                    re_rows.append(nr)
                    im_rows.append(ni)
                hr_ref[seqs, lanes] = hr
                hi_ref[seqs, lanes] = hi
            re_cols.append(jnp.concatenate(re_rows, axis=0))
            im_cols.append(jnp.concatenate(im_rows, axis=0))
        return jnp.concatenate(re_cols + im_cols, axis=1).astype(BF16)

    ys = []
    bu = {kt: project_in(kt) for kt in range(min(SSM_AHEAD, N_GTILES))}
    for kt in range(N_GTILES):
        if kt + SSM_AHEAD < N_GTILES:
            bu[kt + SSM_AHEAD] = project_in(kt + SSM_AHEAD)
        hs = scan(kt, bu.pop(kt))
        ys.append(jnp.dot(hs, wc_ref[kt], preferred_element_type=F32))
    y = jnp.concatenate(ys, axis=-1) + d_ref[...] * u
    z = jax.nn.gelu(y).astype(BF16)
    gl = jnp.dot(z, wglu_ref[...], preferred_element_type=F32)
    out = gl[:, :D_MODEL] * jax.nn.sigmoid(gl[:, D_MODEL:])
    for rg in range(n_rg):
        for t in range(tc):
            rows = slice((rg * tc + t) * SUBLANES, (rg * tc + t + 1) * SUBLANES)
            for c in range(n_cols):
                os_scr[c, gather_idx(rg, t), :] = out[rows, c * LANES:(c + 1) * LANES]
    for b in range(bb):
        o_ref[b] = x_ref[b] + jnp.concatenate(
            [os_scr[c, b * pitch:b * pitch + tc, :] for c in range(n_cols)], axis=1)


def _ssm_call(x, g, wb, wc, lam, d, wglu, h0r, h0i, *, layer, bb, tc):
    bsz, seq, _ = x.shape
    mp = bb * (tc + SSM_ROW_PAD)
    body = functools.partial(_ssm_body, bb=bb, tc=tc)
    return pl.pallas_call(
        body,
        grid=(bsz // bb, seq // tc),
        in_specs=[
            pl.BlockSpec((bb, tc, D_MODEL), lambda i, j: (i, j, 0)),
            _const_spec((1, D_MODEL)),
            _layer_spec(wb, layer),
            _layer_spec(wc, layer),
            _layer_spec(lam, layer),
            _const_spec((1, D_MODEL)),
            _layer_spec(wglu, layer),
            pl.BlockSpec((bb, N_STATE), lambda i, j: (i, 0)),
            pl.BlockSpec((bb, N_STATE), lambda i, j: (i, 0)),
        ],
        out_specs=[
            pl.BlockSpec((bb, tc, D_MODEL), lambda i, j: (i, j, 0)),
            pl.BlockSpec((bb, N_STATE), lambda i, j: (i, 0)),
            pl.BlockSpec((bb, N_STATE), lambda i, j: (i, 0)),
        ],
        out_shape=[
            jax.ShapeDtypeStruct(x.shape, F32),
            jax.ShapeDtypeStruct((bsz, N_STATE), F32),
            jax.ShapeDtypeStruct((bsz, N_STATE), F32),
        ],
        scratch_shapes=[pltpu.VMEM((D_MODEL // LANES, mp, LANES), F32),
                        pltpu.VMEM((D_MODEL // LANES, mp, LANES), F32)],
        compiler_params=pltpu.CompilerParams(
            dimension_semantics=("arbitrary", "arbitrary"), vmem_limit_bytes=VMEM_LIMIT),
        name="ssm",
    )(x, g, wb, wc, lam, d, wglu, h0r, h0i)


def _qk_norm_rope(qk, ones_bd, gain, cos, sina, sinb):
    n_cols = qk.shape[1] // LANES
    sq = qk * qk
    hi = sq.astype(BF16)
    lo = (sq - hi.astype(F32)).astype(BF16)
    outs = []
    for c in range(n_cols // 2):
        sl = slice(c * MXU_DIM, (c + 1) * MXU_DIM)
        ss = (jnp.dot(hi[:, sl], ones_bd, preferred_element_type=F32)
              + jnp.dot(lo[:, sl], ones_bd, preferred_element_type=F32))
        outs.append(qk[:, sl] * lax.rsqrt(ss * (1.0 / HEAD_DIM) + NORM_EPS) * gain[:, sl])
    qn = jnp.concatenate(outs, axis=-1)
    cols = []
    for c in range(n_cols):
        xc = qn[:, c * LANES:(c + 1) * LANES]
        cols.append(xc * cos + pltpu.roll(xc, ROT_DIM // 2, 1) * sina
                    + pltpu.roll(xc, LANES - ROT_DIM // 2, 1) * sinb)
    return jnp.concatenate(cols, axis=-1)


def _lane_halves(x):
    lane = lax.broadcasted_iota(jnp.int32, x.shape, 1)
    low = lane < HEAD_DIM
    return jnp.where(low, x, 0.0), jnp.where(low, 0.0, x)


TQ = 256
ATTN_AHEAD = 2
ONES_ROWS = 16


def _attn_prompt_body(x_ref, g_ref, wqkv_ref, ones_ref, gain_ref, cos_ref, sina_ref, sinb_ref,
                      gaint_ref, cost_ref, sint_ref, bias_ref, sink_ref, wo_ref, o_ref, kc_ref, vc_ref,
                      qt_cur, ka_cur, kb_cur, vt_cur, ot_scr, xprev_scr, qt_nxt, ka_nxt, kb_nxt, vt_nxt,
                      ka_prev, kb_prev, vt_prev, *, blocks_per_seq):
    g = pl.program_id(0)

    @pl.when(g == 0)
    def _():
        for scr in (ka_cur, kb_cur, ka_prev, kb_prev):
            scr[...] = jnp.zeros(scr.shape, BF16)
        for scr in (vt_cur, vt_prev):
            scr[:, 0:HEAD_DIM, :] = jnp.zeros((N_KV_HEADS, HEAD_DIM, scr.shape[2]), BF16)
            scr[:, HEAD_DIM:, :] = jnp.ones((N_KV_HEADS, ONES_ROWS, scr.shape[2]), BF16)
        qt_cur[...] = jnp.zeros((D_MODEL, TQ), BF16)
        xprev_scr[...] = jnp.zeros((TQ, D_MODEL), F32)

    bias_mid = jnp.concatenate([bias_ref[...], bias_ref[...]], axis=1)
    first = (g - 1) % blocks_per_seq == 0
    key_row = lax.broadcasted_iota(jnp.int32, bias_mid.shape, 0)
    bias_first = jnp.where(jnp.logical_and(first, key_row < WINDOW), NEG_INF, bias_mid)
    lane = lax.broadcasted_iota(jnp.int32, (1, 2 * WINDOW), 1)
    pair = 2 * HEAD_DIM

    def key_window(j, h, prev, cur):
        if j == 0:
            return [prev[h], cur[h, 0:WINDOW, :]]
        return [cur[h, (j - 1) * WINDOW:(j + 1) * WINDOW, :]]

    def value_window(j, h):
        if j == 0:
            return jnp.concatenate([vt_prev[h], vt_cur[h, :, 0:WINDOW]], axis=1)
        return vt_cur[h, :, (j - 1) * WINDOW:(j + 1) * WINDOW]

    def scores(j, h):
        qpos = slice(j * WINDOW, (j + 1) * WINDOW)
        qs = jnp.concatenate([qt_cur[(2 * h) * pair:(2 * h + 1) * pair, qpos],
                              qt_cur[(2 * h + 1) * pair:(2 * h + 2) * pair, qpos]], axis=1)
        kk = jnp.concatenate(key_window(j, h, ka_prev, ka_cur) + key_window(j, h, kb_prev, kb_cur),
                             axis=0)
        return jnp.dot(kk, qs, preferred_element_type=F32)

    def softmax(j, h, st):
        bias = bias_first if j == 0 else bias_mid
        out = []
        for par in range(2):
            s = st[par * 2 * WINDOW:(par + 1) * 2 * WINDOW, :] + bias
            sink = jnp.where(lane < WINDOW, sink_ref[4 * h + par], sink_ref[4 * h + 2 + par])
            m = jnp.maximum(jnp.max(s, axis=0, keepdims=True), sink)
            out.append((jnp.exp(s - m).astype(BF16), jnp.exp(sink - m)))
        return out

    def weighted_values(j, h, probs):
        qpos = slice(j * WINDOW, (j + 1) * WINDOW)
        vth = value_window(j, h)
        for par, (p, sink_term) in enumerate(probs):
            nd = jnp.dot(vth, p, preferred_element_type=F32)
            denom = nd[HEAD_DIM:HEAD_DIM + 1, :] + sink_term
            on = nd[:HEAD_DIM, :] * (1.0 / denom)
            for rp in range(2):
                head = 4 * h + 2 * rp + par
                ot_scr[head * HEAD_DIM:(head + 1) * HEAD_DIM, qpos] = on[:, rp * WINDOW:(rp + 1) * WINDOW]

    x = x_ref[0]
    xn = _rms(x, g_ref[...]).astype(BF16)
    tabs = (cos_ref[...], sina_ref[...], sinb_ref[...])
    cos_t = cost_ref[...]
    sin_t = sint_ref[...]
    half = ROT_DIM // 2

    def project(c):
        cols = slice(c * MXU_DIM, (c + 1) * MXU_DIM)
        t = jnp.dot(xn, wqkv_ref[:, cols], preferred_element_type=F32)
        if c * MXU_DIM < D_MODEL:
            tt = t.T
            heads = []
            for hl in range(MXU_DIM // HEAD_DIM):
                blk = tt[hl * HEAD_DIM:(hl + 1) * HEAD_DIM, :]
                ms = jnp.sum(blk * blk, axis=0, keepdims=True) * (1.0 / HEAD_DIM)
                gcol = gaint_ref[c * MXU_DIM + hl * HEAD_DIM:c * MXU_DIM + (hl + 1) * HEAD_DIM, :]
                y = blk * lax.rsqrt(ms + NORM_EPS) * jnp.concatenate([gcol] * (TQ // LANES), axis=1)
                x1, x2 = y[:half], y[half:ROT_DIM]
                heads.append(jnp.concatenate(
                    [x1 * cos_t - x2 * sin_t, x2 * cos_t + x1 * sin_t, y[ROT_DIM:]], axis=0))
            qt_nxt[cols, :] = jnp.concatenate(heads, axis=0).astype(BF16)
        elif c * MXU_DIM < D_QK:
            t = _qk_norm_rope(t, ones_ref[...], gain_ref[:, cols], *tabs)
            kc_ref[0] = t[TQ - WINDOW:, :]
            for h in range(N_KV_HEADS):
                lo_part, hi_part = _lane_halves(t[:, (h // 2) * LANES:(h // 2 + 1) * LANES])
                if h % 2 == 0:
                    a, b = lo_part, pltpu.roll(lo_part, HEAD_DIM, 1)
                else:
                    a, b = pltpu.roll(hi_part, HEAD_DIM, 1), hi_part
                ka_nxt[h] = a.astype(BF16)
                kb_nxt[h] = b.astype(BF16)
        else:
            vc_ref[0] = t[TQ - WINDOW:, :]
            vt = t.T
            for h in range(N_KV_HEADS):
                vt_nxt[h] = vt[h * HEAD_DIM:(h + 1) * HEAD_DIM, :].astype(BF16)

    groups = [(j, h) for j in range(TQ // WINDOW) for h in range(N_KV_HEADS)]
    n_groups = len(groups)
    st, probs = {}, {}
    for n in range(n_groups + ATTN_AHEAD):
        if n < n_groups:
            st[n] = scores(*groups[n])
        if n < D_QKV // MXU_DIM:
            project(n)
        d = n - ATTN_AHEAD + 1
        if 0 <= d < n_groups:
            probs[d] = softmax(*groups[d], st.pop(d))
        d = n - ATTN_AHEAD
        if 0 <= d < n_groups:
            weighted_values(*groups[d], probs.pop(d))
    attn = ot_scr[...].T.astype(BF16)
    o_ref[0] = xprev_scr[...] + jnp.dot(attn, wo_ref[...], preferred_element_type=F32)

    xprev_scr[...] = x
    qt_cur[...] = qt_nxt[...]
    for prev, cur, nxt in ((ka_prev, ka_cur, ka_nxt), (kb_prev, kb_cur, kb_nxt)):
        prev[...] = cur[:, TQ - WINDOW:, :]
        cur[...] = nxt[...]
    vt_prev[:, 0:HEAD_DIM, :] = vt_cur[:, 0:HEAD_DIM, TQ - WINDOW:]
    vt_cur[:, 0:HEAD_DIM, :] = vt_nxt[...]


def _attn_prompt_call(x, g, wqkv, ones_bd, gain, cos, sina, sinb, bias, sinks, wo, *, layer):
    bsz, seq, _ = x.shape
    bps = seq // TQ
    n_blocks = bsz * bps
    k_blk = pltpu.VMEM((N_KV_HEADS, TQ, LANES), BF16)
    k_tail = pltpu.VMEM((N_KV_HEADS, WINDOW, LANES), BF16)
    vt_blk = pltpu.VMEM((N_KV_HEADS, HEAD_DIM + ONES_ROWS, TQ), BF16)
    vt_tail = pltpu.VMEM((N_KV_HEADS, HEAD_DIM + ONES_ROWS, WINDOW), BF16)

    def proj_block(gi):
        blk = jnp.minimum(gi, n_blocks - 1)
        return blk // bps, blk % bps

    def attn_block(gi):
        blk = jnp.maximum(gi - 1, 0)
        return blk // bps, blk % bps

    tab_spec = pl.BlockSpec((TQ, LANES), lambda gi: (proj_block(gi)[1], 0))
    cache_spec = pl.BlockSpec((1, WINDOW, D_KV), lambda gi: (proj_block(gi)[0], 0, 0))
    half = ROT_DIM // 2
    gain_t = jnp.broadcast_to(gain[0, :D_MODEL, None], (D_MODEL, LANES))
    cos_t = cos[:, :half].T
    sin_t = sina[:, half:ROT_DIM].T
    tab_t_spec = pl.BlockSpec((half, TQ), lambda gi: (0, proj_block(gi)[1]))
    body = functools.partial(_attn_prompt_body, blocks_per_seq=bps)
    return pl.pallas_call(
        body,
        grid=(n_blocks + 1,),
        in_specs=[
            pl.BlockSpec((1, TQ, D_MODEL), lambda gi: (*proj_block(gi), 0)),
            _const_spec((1, D_MODEL)),
            _layer_spec(wqkv, layer),
            _const_spec((MXU_DIM, MXU_DIM)),
            _const_spec((1, D_QK)),
            tab_spec, tab_spec, tab_spec,
            _const_spec((D_MODEL, LANES)),
            tab_t_spec, tab_t_spec,
            _const_spec((2 * WINDOW, WINDOW)),
            pl.BlockSpec(memory_space=pltpu.SMEM),
            _layer_spec(wo, layer),
        ],
        out_specs=[
            pl.BlockSpec((1, TQ, D_MODEL), lambda gi: (*attn_block(gi), 0)),
            cache_spec, cache_spec,
        ],
        out_shape=[
            jax.ShapeDtypeStruct(x.shape, F32),
            jax.ShapeDtypeStruct((bsz, WINDOW, D_KV), F32),
            jax.ShapeDtypeStruct((bsz, WINDOW, D_KV), F32),
        ],
        scratch_shapes=[pltpu.VMEM((D_MODEL, TQ), BF16), k_blk, k_blk, vt_blk,
                        pltpu.VMEM((D_MODEL, TQ), F32),
                        pltpu.VMEM((TQ, D_MODEL), F32),
                        pltpu.VMEM((D_MODEL, TQ), BF16), k_blk, k_blk,
                        pltpu.VMEM((N_KV_HEADS, HEAD_DIM, TQ), BF16),
                        k_tail, k_tail, vt_tail],
        compiler_params=pltpu.CompilerParams(
            dimension_semantics=("arbitrary",), vmem_limit_bytes=VMEM_LIMIT),
        name="attn_prompt",
    )(x, g, wqkv, ones_bd, gain, cos, sina, sinb, gain_t, cos_t, sin_t, bias, sinks, wo)


SEQ_BLOCK = 16
SAMPLE_AHEAD = 4
N_KEYS_PAD = 2 * WINDOW


def _attn_sample_body(x_ref, ck_ref, cv_ref, g_ref, wqkv_ref, ones_ref, gain_ref, cos_ref, sina_ref,
                      sinb_ref, bias_ref, sink_ref, wo_ref, o_ref, kc_ref, vc_ref,
                      kf_scr, vf_scr, oall_scr, *, s_len):
    rows = SEQ_BLOCK * s_len
    w_keep = WINDOW - s_len

    @pl.when(pl.program_id(0) == 0)
    def _():
        pad = (SEQ_BLOCK, N_KEYS_PAD - WINDOW, D_KV)
        kf_scr[:, WINDOW:, :] = jnp.zeros(pad, BF16)
        vf_scr[:, WINDOW:, :] = jnp.zeros(pad, BF16)

    x = x_ref[...].reshape(rows, D_MODEL)
    xn = _rms(x, g_ref[...]).astype(BF16)
    qkv = jnp.dot(xn, wqkv_ref[...], preferred_element_type=F32)
    qk = _qk_norm_rope(qkv[:, :D_QK], ones_ref[...], gain_ref[...],
                       cos_ref[...], sina_ref[...], sinb_ref[...])
    q = qk[:, :D_MODEL]
    k_new = qk[:, D_MODEL:].reshape(SEQ_BLOCK, s_len, D_KV)
    v_new = qkv[:, D_QK:].reshape(SEQ_BLOCK, s_len, D_KV)

    ck = ck_ref[...]
    cv = cv_ref[...]
    kc_ref[:, 0:w_keep, :] = ck[:, s_len:, :]
    vc_ref[:, 0:w_keep, :] = cv[:, s_len:, :]
    kc_ref[:, w_keep:, :] = k_new
    vc_ref[:, w_keep:, :] = v_new
    kf_scr[:, 0:WINDOW, :] = ck.astype(BF16)
    vf_scr[:, 0:WINDOW, :] = cv.astype(BF16)
    tile = 2 * SUBLANES
    zpad = jnp.zeros((SEQ_BLOCK, tile - s_len, D_KV), F32)
    kf_scr[:, WINDOW:WINDOW + tile, :] = jnp.concatenate([k_new, zpad], axis=1).astype(BF16)
    vf_scr[:, WINDOW:WINDOW + tile, :] = jnp.concatenate([v_new, zpad], axis=1).astype(BF16)

    bias = bias_ref[...]
    sink = sink_ref[...]
    nt_dims = (((1,), (1,)), ((), ()))
    zero_col = jnp.zeros((s_len, LANES), F32)
    def scores(s):
        qs = q[s * s_len:(s + 1) * s_len, :]
        tiles = []
        for h in range(N_KV_HEADS):
            for r in range(KV_REP):
                head = KV_REP * h + r
                src = qs[:, (head // 2) * LANES:(head // 2 + 1) * LANES]
                lo_part, hi_part = _lane_halves(src)
                part = lo_part if head % 2 == 0 else hi_part
                if head % 2 != h % 2:
                    part = pltpu.roll(part, HEAD_DIM, 1)
                tiles.append(jnp.concatenate([part, zero_col] if h < 2 else [zero_col, part], axis=1))
        wt = jnp.concatenate(tiles, axis=0).astype(BF16)
        return lax.dot_general(wt, kf_scr[s], nt_dims, preferred_element_type=F32)

    def softmax(sc):
        sc = sc + bias
        m = jnp.maximum(jnp.max(sc, axis=-1, keepdims=True), sink)
        p = jnp.exp(sc - m)
        denom = jnp.sum(p, axis=-1, keepdims=True) + jnp.exp(sink - m)
        return p.astype(BF16), 1.0 / denom

    def weighted_values(s, p, inv_denom):
        o = jnp.dot(p, vf_scr[s], preferred_element_type=F32) * inv_denom
        cols = []
        for c in range(N_HEADS // 2):
            pieces = []
            for head in (2 * c, 2 * c + 1):
                h, r = divmod(head, KV_REP)
                blk = o[(KV_REP * h + r) * s_len:(KV_REP * h + r + 1) * s_len,
                        (h // 2) * LANES:(h // 2 + 1) * LANES]
                lo_part, hi_part = _lane_halves(blk)
                part = lo_part if h % 2 == 0 else hi_part
                if h % 2 != head % 2:
                    part = pltpu.roll(part, HEAD_DIM, 1)
                pieces.append(part)
            cols.append(pieces[0] + pieces[1])
        oall_scr[s * s_len:(s + 1) * s_len, :] = jnp.concatenate(cols, axis=-1)

    sc, probs = {}, {}
    for n in range(SEQ_BLOCK + SAMPLE_AHEAD):
        if n < SEQ_BLOCK:
            sc[n] = scores(n)
        d = n - SAMPLE_AHEAD // 2
        if 0 <= d < SEQ_BLOCK:
            probs[d] = softmax(sc.pop(d))
        d = n - SAMPLE_AHEAD
        if 0 <= d < SEQ_BLOCK:
            weighted_values(d, *probs.pop(d))

    out = jnp.dot(oall_scr[...].astype(BF16), wo_ref[...], preferred_element_type=F32)
    o_ref[...] = (x + out).reshape(SEQ_BLOCK, s_len, D_MODEL)


def _attn_sample_call(x, ck, cv, g, wqkv, ones_bd, gain, cos, sina, sinb, bias, sink_rows, wo, *, layer):
    bsz, s_len, _ = x.shape
    rows = SEQ_BLOCK * s_len
    body = functools.partial(_attn_sample_body, s_len=s_len)
    seq_spec = lambda shape: pl.BlockSpec(shape, lambda i: (i, 0, 0))
    cache_spec = pl.BlockSpec((None, SEQ_BLOCK, WINDOW, D_KV), lambda i: (layer, i, 0, 0))
    return pl.pallas_call(
        body,
        grid=(bsz // SEQ_BLOCK,),
        in_specs=[
            seq_spec((SEQ_BLOCK, s_len, D_MODEL)),
            cache_spec,
            cache_spec,
            _const_spec((1, D_MODEL)),
            _layer_spec(wqkv, layer),
            _const_spec((MXU_DIM, MXU_DIM)),
            _const_spec((1, D_QK)),
            _const_spec((rows, LANES)),
            _const_spec((rows, LANES)),
            _const_spec((rows, LANES)),
            _const_spec((N_HEADS * s_len, N_KEYS_PAD)),
            _const_spec((N_HEADS * s_len, 1)),
            _layer_spec(wo, layer),
        ],
        out_specs=[
            seq_spec((SEQ_BLOCK, s_len, D_MODEL)),
            seq_spec((SEQ_BLOCK, WINDOW, D_KV)),
            seq_spec((SEQ_BLOCK, WINDOW, D_KV)),
        ],
        out_shape=[
            jax.ShapeDtypeStruct(x.shape, F32),
            jax.ShapeDtypeStruct(ck.shape[1:], F32),
            jax.ShapeDtypeStruct(cv.shape[1:], F32),
        ],
        scratch_shapes=[
            pltpu.VMEM((SEQ_BLOCK, N_KEYS_PAD, D_KV), BF16),
            pltpu.VMEM((SEQ_BLOCK, N_KEYS_PAD, D_KV), BF16),
            pltpu.VMEM((rows, D_MODEL), F32),
        ],
        compiler_params=pltpu.CompilerParams(
            dimension_semantics=("arbitrary",), vmem_limit_bytes=VMEM_LIMIT),
        name="attn_sample",
    )(x, ck, cv, g, wqkv, ones_bd, gain, cos, sina, sinb, bias, sink_rows, wo)


def _ssm_weights(a_re, a_im, log_dt, b_re, b_im, c_re, c_im):
    nl = a_re.shape[0]
    dt = jnp.exp(log_dt)[..., None]
    mag = jnp.exp(a_re * dt)
    lam_re = mag * jnp.cos(a_im * dt)
    lam_im = mag * jnp.sin(a_im * dt)
    den = a_re * a_re + a_im * a_im
    nr = lam_re - 1.0
    ni = lam_im
    f_re = (nr * a_re + ni * a_im) / den
    f_im = (ni * a_re - nr * a_im) / den
    bb_re = f_re[..., None] * b_re - f_im[..., None] * b_im
    bb_im = f_re[..., None] * b_im + f_im[..., None] * b_re
    gpt = MXU_DIM // SSM_GROUP
    n_st = gpt * STATE_DIM
    same_group = jnp.asarray(
        (np.arange(MXU_DIM)[:, None] // SSM_GROUP) == (np.arange(n_st)[None, :] // STATE_DIM))

    def b_tiles(bb):
        t = jnp.swapaxes(bb, 2, 3).reshape(nl, N_GTILES, MXU_DIM, STATE_DIM)
        return jnp.where(same_group, jnp.tile(t, (1, 1, 1, gpt)), 0.0)

    def c_tiles(cc):
        t = jnp.swapaxes(cc, 2, 3).reshape(nl, N_GTILES, n_st, SSM_GROUP)
        return jnp.where(same_group.T, jnp.tile(t, (1, 1, 1, gpt)), 0.0)

    wb_re = b_tiles(bb_re).astype(BF16)
    wb_im = b_tiles(bb_im).astype(BF16)
    wb = jnp.concatenate([part[..., j * LANES:(j + 1) * LANES]
                          for j in range(SLABS_PER_TILE) for part in (wb_re, wb_im)], axis=-1)
    wc = jnp.concatenate([c_tiles(c_re), -c_tiles(c_im)], axis=2).astype(BF16)
    lam = jnp.concatenate([lam_re.reshape(nl, N_SLABS, 1, LANES), lam_im.reshape(nl, N_SLABS, 1, LANES)], axis=1)
    lam = jnp.broadcast_to(lam, (nl, 2 * N_SLABS, SUBLANES, LANES))
    return wb, wc, lam


def _rope_tables(pos):
    half = ROT_DIM // 2
    inv_freq = ROPE_THETA ** (-jnp.arange(half, dtype=F32) * 2.0 / ROT_DIM)
    ang = pos[:, None] * inv_freq[None, :]
    cos, sin = jnp.cos(ang), jnp.sin(ang)
    n = pos.shape[0]
    ones = jnp.ones((n, HEAD_DIM - ROT_DIM), F32)
    zeros = jnp.zeros((n, HEAD_DIM - ROT_DIM), F32)
    zh = jnp.zeros((n, half), F32)
    cos_h = jnp.concatenate([cos, cos, ones], axis=1)
    sina_h = jnp.concatenate([zh, sin, zeros], axis=1)
    sinb_h = jnp.concatenate([-sin, zh, zeros], axis=1)
    return tuple(jnp.concatenate([t, t], axis=1) for t in (cos_h, sina_h, sinb_h))


def _ones_block_diag():
    idx = np.arange(MXU_DIM) // HEAD_DIM
    return jnp.asarray(idx[:, None] == idx[None, :], dtype=BF16)


def _prompt_bias():
    js = np.arange(2 * WINDOW)[:, None]
    iq = np.arange(WINDOW)[None, :]
    diff = WINDOW + iq - js
    return jnp.asarray(np.where((diff >= 0) & (diff < WINDOW), 0.0, NEG_INF), dtype=F32)


def _sample_bias(s_len):
    t = (np.arange(N_HEADS * s_len) % s_len)[:, None]
    key = np.arange(N_KEYS_PAD)[None, :]
    diff = np.where(key < WINDOW, WINDOW + t - key, t - (key - WINDOW))
    ok = (diff >= 0) & (diff < WINDOW) & (key < WINDOW + s_len)
    return jnp.asarray(np.where(ok, 0.0, NEG_INF), dtype=F32)


SSM_PROMPT_TC = 64
SSM_SAMPLE_BB = 32
FFN_TM = 1024


def kernel(x_prompt, x_sample, state_ssm_re, state_ssm_im, cache_swa_k, cache_swa_v, norm_mix, norm_ffn, ssm_a_re, ssm_a_im, ssm_log_dt, ssm_b_re, ssm_b_im, ssm_c_re, ssm_c_im, ssm_d, ssm_w_glu, attn_w_qkv, attn_q_norm, attn_k_norm, attn_sinks, attn_w_o, ffn_w_gate_up, ffn_w_down):
    bsz, seq, _ = x_prompt.shape
    dbsz, dseq, _ = x_sample.shape
    yp, ys = x_prompt, x_sample
    p_re, p_im, p_k, p_v, s_re, s_im, s_k, s_v = ([] for _ in range(8))

    ones_bd = _ones_block_diag()
    p_tabs = _rope_tables(jnp.arange(seq, dtype=F32))
    s_pos = PAST_LEN + jnp.arange(dseq, dtype=F32)
    s_tabs = tuple(jnp.tile(t, (SEQ_BLOCK, 1)) for t in _rope_tables(s_pos))
    p_bias = _prompt_bias()
    s_bias = _sample_bias(dseq)
    h0 = jnp.zeros((bsz, N_STATE), F32)

    wb, wc, lam = _ssm_weights(ssm_a_re, ssm_a_im, ssm_log_dt, ssm_b_re, ssm_b_im, ssm_c_re, ssm_c_im)
    wglu = ssm_w_glu.astype(BF16)
    wqkv = attn_w_qkv.astype(BF16)
    wo = attn_w_o.astype(BF16)
    n_swa = cache_swa_k.shape[0]
    ck_all = cache_swa_k.reshape(n_swa, dbsz, WINDOW, D_KV)
    cv_all = cache_swa_v.reshape(n_swa, dbsz, WINDOW, D_KV)

    for i in range(DEPTH):
        j = i // 2
        g_mix = norm_mix[i].reshape(1, D_MODEL)
        if i % 2 == 0:
            d = ssm_d[j].reshape(1, D_MODEL)
            yp, hr, hi = _ssm_call(yp, g_mix, wb, wc, lam, d, wglu, h0, h0,
                                   layer=j, bb=bsz, tc=SSM_PROMPT_TC)
            ys, sr, si = _ssm_call(ys, g_mix, wb, wc, lam, d, wglu,
                                   state_ssm_re[j].reshape(dbsz, N_STATE), state_ssm_im[j].reshape(dbsz, N_STATE),
                                   layer=j, bb=SSM_SAMPLE_BB, tc=dseq)
            shp = (N_GROUPS, STATE_DIM)
            p_re.append(hr.reshape(bsz, *shp))
            p_im.append(hi.reshape(bsz, *shp))
            s_re.append(sr.reshape(dbsz, *shp))
            s_im.append(si.reshape(dbsz, *shp))
        else:
            gain = jnp.concatenate([jnp.tile(attn_q_norm[j] * ATTN_SCALE, N_HEADS),
                                    jnp.tile(attn_k_norm[j], N_KV_HEADS)]).reshape(1, D_QK)
            sinks = attn_sinks[j]
            yp, kp, vp = _attn_prompt_call(yp, g_mix, wqkv, ones_bd, gain, *p_tabs, p_bias, sinks, wo, layer=j)
            sink_rows = jnp.repeat(sinks, dseq).reshape(N_HEADS * dseq, 1)
            ys, ks_, vs_ = _attn_sample_call(ys, ck_all, cv_all, g_mix, wqkv, ones_bd, gain, *s_tabs, s_bias,
                                             sink_rows, wo, layer=j)
            kv_shape = (WINDOW, N_KV_HEADS, HEAD_DIM)
            p_k.append(kp.reshape(bsz, *kv_shape))
            p_v.append(vp.reshape(bsz, *kv_shape))
            s_k.append(ks_.reshape(dbsz, *kv_shape))
            s_v.append(vs_.reshape(dbsz, *kv_shape))
        g_ffn = norm_ffn[i].reshape(1, D_MODEL)
        ys, wg, wu, wd = _ffn_cast_call(ys.reshape(dbsz * dseq, D_MODEL), g_ffn, ffn_w_gate_up, ffn_w_down, layer=i)
        ys = ys.reshape(dbsz, dseq, D_MODEL)
        yp = _ffn_call(yp.reshape(bsz * seq, D_MODEL), g_ffn, wg, wu, wd, tm=FFN_TM).reshape(bsz, seq, D_MODEL)

    return (yp, ys, jnp.stack(p_re), jnp.stack(p_im), jnp.stack(p_k), jnp.stack(p_v),
            jnp.stack(s_re), jnp.stack(s_im), jnp.stack(s_k), jnp.stack(s_v))
```

```python
import functools

import jax
import jax.numpy as jnp
import numpy as np
from jax import lax
from jax.experimental import pallas as pl
from jax.experimental.pallas import tpu as pltpu

F32 = jnp.float32
BF16 = jnp.bfloat16

D_MODEL = 1024
DEPTH = 4
PAST_LEN = 8192
SSM_GROUP = 16
N_GROUPS = D_MODEL // SSM_GROUP
STATE_DIM = 64
N_STATE = N_GROUPS * STATE_DIM
HEAD_DIM = 64
N_HEADS = D_MODEL // HEAD_DIM
N_KV_HEADS = 4
KV_REP = N_HEADS // N_KV_HEADS
D_KV = N_KV_HEADS * HEAD_DIM
D_QK = D_MODEL + D_KV
D_QKV = D_MODEL + 2 * D_KV
WINDOW = 128
ROT_DIM = HEAD_DIM // 4
ROPE_THETA = 500000.0
ATTN_SCALE = HEAD_DIM ** -0.5
D_FF = 2816
NORM_EPS = 1e-6
NEG_INF = -1e30

LANES = 128
SUBLANES = 8
MXU_DIM = 256
N_SLABS = N_STATE // LANES
VMEM_LIMIT = 56 * 1024 * 1024


def _rms(x, g):
    ms = jnp.mean(x * x, axis=-1, keepdims=True)
    return x * lax.rsqrt(ms + NORM_EPS) * g


def _const_spec(shape):
    nd = len(shape)
    return pl.BlockSpec(shape, lambda *_: (0,) * nd, pipeline_mode=pl.Buffered(1))


def _layer_spec(stack, layer):
    nd = stack.ndim - 1
    return pl.BlockSpec((None,) + stack.shape[1:], lambda *_: (layer,) + (0,) * nd,
                        pipeline_mode=pl.Buffered(1))


FF_CHUNK = 256


def _ffn_body(x_ref, g_ref, wg_ref, wu_ref, wd_ref, o_ref, a_scr):
    x = x_ref[...]
    xn = _rms(x, g_ref[...]).astype(BF16)
    for c in range(D_FF // FF_CHUNK):
        gate = jnp.dot(xn, wg_ref[c], preferred_element_type=F32)
        up = jnp.dot(xn, wu_ref[c], preferred_element_type=F32)
        a_scr[:, c * FF_CHUNK:(c + 1) * FF_CHUNK] = (gate * jax.nn.sigmoid(gate) * up).astype(BF16)
    o_ref[...] = x + jnp.dot(a_scr[...], wd_ref[...], preferred_element_type=F32)


def _ffn_call(x, g, wg, wu, wd, *, tm):
    m = x.shape[0]
    return pl.pallas_call(
        _ffn_body,
        grid=(m // tm,),
        in_specs=[
            pl.BlockSpec((tm, D_MODEL), lambda i: (i, 0)),
            _const_spec((1, D_MODEL)),
            _const_spec(wg.shape),
            _const_spec(wu.shape),
            _const_spec(wd.shape),
        ],
        out_specs=pl.BlockSpec((tm, D_MODEL), lambda i: (i, 0)),
        out_shape=jax.ShapeDtypeStruct((m, D_MODEL), F32),
        scratch_shapes=[pltpu.VMEM((tm, D_FF), BF16)],
        compiler_params=pltpu.CompilerParams(
            dimension_semantics=("arbitrary",), vmem_limit_bytes=VMEM_LIMIT),
        name="ffn",
    )(x, g, wg, wu, wd)


def _ffn_cast_body(x_ref, g_ref, wg_ref, wu_ref, wd_ref, o_ref, wgb_ref, wub_ref, wdb_ref, xn_scr):
    c = pl.program_id(0)

    @pl.when(c == 0)
    def _():
        x = x_ref[...]
        xn_scr[...] = _rms(x, g_ref[...]).astype(BF16)
        o_ref[...] = x

    wg = wg_ref[...].astype(BF16)
    wu = wu_ref[...].astype(BF16)
    wd = wd_ref[...].astype(BF16)
    wgb_ref[...] = wg
    wub_ref[...] = wu
    wdb_ref[...] = wd
    xn = xn_scr[...]
    gate = jnp.dot(xn, wg, preferred_element_type=F32)
    up = jnp.dot(xn, wu, preferred_element_type=F32)
    a = (gate * jax.nn.sigmoid(gate) * up).astype(BF16)
    o_ref[...] += jnp.dot(a, wd, preferred_element_type=F32)


def _ffn_cast_call(x, g, wgu, wd, *, layer):
    m = x.shape[0]
    n_chunks = D_FF // FF_CHUNK
    return pl.pallas_call(
        _ffn_cast_body,
        grid=(n_chunks,),
        in_specs=[
            pl.BlockSpec((m, D_MODEL), lambda c: (0, 0)),
            _const_spec((1, D_MODEL)),
            pl.BlockSpec((None, D_MODEL, FF_CHUNK), lambda c: (layer, 0, c)),
            pl.BlockSpec((None, D_MODEL, FF_CHUNK), lambda c: (layer, 0, n_chunks + c)),
            pl.BlockSpec((None, FF_CHUNK, D_MODEL), lambda c: (layer, c, 0)),
        ],
        out_specs=[
            pl.BlockSpec((m, D_MODEL), lambda c: (0, 0)),
            pl.BlockSpec((None, D_MODEL, FF_CHUNK), lambda c: (c, 0, 0)),
            pl.BlockSpec((None, D_MODEL, FF_CHUNK), lambda c: (c, 0, 0)),
            pl.BlockSpec((FF_CHUNK, D_MODEL), lambda c: (c, 0)),
        ],
        out_shape=[
            jax.ShapeDtypeStruct((m, D_MODEL), F32),
            jax.ShapeDtypeStruct((n_chunks, D_MODEL, FF_CHUNK), BF16),
            jax.ShapeDtypeStruct((n_chunks, D_MODEL, FF_CHUNK), BF16),
            jax.ShapeDtypeStruct((D_FF, D_MODEL), BF16),
        ],
        scratch_shapes=[pltpu.VMEM((m, D_MODEL), BF16)],
        compiler_params=pltpu.CompilerParams(
            dimension_semantics=("arbitrary",), vmem_limit_bytes=VMEM_LIMIT),
        name="ffn_cast",
    )(x, g, wgu, wgu, wd)


N_GTILES = D_MODEL // MXU_DIM
SLABS_PER_TILE = N_SLABS // N_GTILES


SSM_AHEAD = 1
SSM_ROW_PAD = 1


def _ssm_body(x_ref, g_ref, wb_ref, wc_ref, lam_ref, d_ref, wglu_ref, h0r_ref, h0i_ref,
              o_ref, hr_ref, hi_ref, xs_scr, os_scr, *, bb, tc):
    pitch = tc + SSM_ROW_PAD
    spt = SLABS_PER_TILE
    n_cols = D_MODEL // LANES
    n_rg = bb // SUBLANES

    @pl.when(pl.program_id(1) == 0)
    def _():
        hr_ref[...] = h0r_ref[...]
        hi_ref[...] = h0i_ref[...]

    def gather_idx(rg, t):
        return pl.ds(rg * SUBLANES * pitch + t, SUBLANES, stride=pitch)

    g = g_ref[...]
    for b in range(bb):
        xn = _rms(x_ref[b], g)
        for c in range(n_cols):
            xs_scr[c, b * pitch:b * pitch + tc, :] = xn[:, c * LANES:(c + 1) * LANES]
    u = jnp.concatenate(
        [jnp.concatenate([xs_scr[c, gather_idx(rg, t), :] for c in range(n_cols)], axis=1)
         for rg in range(n_rg) for t in range(tc)], axis=0)
    ub = u.astype(BF16)

    def project_in(kt):
        return jnp.dot(ub[:, kt * MXU_DIM:(kt + 1) * MXU_DIM], wb_ref[kt], preferred_element_type=F32)

    def scan(kt, bu):
        re_cols, im_cols = [], []
        for j in range(spt):
            s = kt * spt + j
            lanes = slice(s * LANES, (s + 1) * LANES)
            lr = lam_ref[s]
            li = lam_ref[N_SLABS + s]
            re_rows, im_rows = [], []
            for rg in range(n_rg):
                seqs = slice(rg * SUBLANES, (rg + 1) * SUBLANES)
                hr = hr_ref[seqs, lanes]
                hi = hi_ref[seqs, lanes]
                for t in range(tc):
                    rows = slice((rg * tc + t) * SUBLANES, (rg * tc + t + 1) * SUBLANES)
                    nr = (lr * hr - li * hi) + bu[rows, (2 * j) * LANES:(2 * j + 1) * LANES]
                    ni = (lr * hi + li * hr) + bu[rows, (2 * j + 1) * LANES:(2 * j + 2) * LANES]
                    hr, hi = nr, ni
                    re_rows.append(nr)
                    im_rows.append(ni)
                hr_ref[seqs, lanes] = hr
                hi_ref[seqs, lanes] = hi
            re_cols.append(jnp.concatenate(re_rows, axis=0))
            im_cols.append(jnp.concatenate(im_rows, axis=0))
        return jnp.concatenate(re_cols + im_cols, axis=1).astype(BF16)

    ys = []
    bu = {kt: project_in(kt) for kt in range(min(SSM_AHEAD, N_GTILES))}
    for kt in range(N_GTILES):
        if kt + SSM_AHEAD < N_GTILES:
            bu[kt + SSM_AHEAD] = project_in(kt + SSM_AHEAD)
        hs = scan(kt, bu.pop(kt))
        ys.append(jnp.dot(hs, wc_ref[kt], preferred_element_type=F32))
    y = jnp.concatenate(ys, axis=-1) + d_ref[...] * u
    z = jax.nn.gelu(y).astype(BF16)
    gl = jnp.dot(z, wglu_ref[...], preferred_element_type=F32)
    out = gl[:, :D_MODEL] * jax.nn.sigmoid(gl[:, D_MODEL:])
    for rg in range(n_rg):
        for t in range(tc):
            rows = slice((rg * tc + t) * SUBLANES, (rg * tc + t + 1) * SUBLANES)
            for c in range(n_cols):
                os_scr[c, gather_idx(rg, t), :] = out[rows, c * LANES:(c + 1) * LANES]
    for b in range(bb):
        o_ref[b] = x_ref[b] + jnp.concatenate(
            [os_scr[c, b * pitch:b * pitch + tc, :] for c in range(n_cols)], axis=1)


def _ssm_call(x, g, wb, wc, lam, d, wglu, h0r, h0i, *, layer, bb, tc):
    bsz, seq, _ = x.shape
    mp = bb * (tc + SSM_ROW_PAD)
    body = functools.partial(_ssm_body, bb=bb, tc=tc)
    return pl.pallas_call(
        body,
        grid=(bsz // bb, seq // tc),
        in_specs=[
            pl.BlockSpec((bb, tc, D_MODEL), lambda i, j: (i, j, 0)),
            _const_spec((1, D_MODEL)),
            _layer_spec(wb, layer),
            _layer_spec(wc, layer),
            _layer_spec(lam, layer),
            _const_spec((1, D_MODEL)),
            _layer_spec(wglu, layer),
            pl.BlockSpec((bb, N_STATE), lambda i, j: (i, 0)),
            pl.BlockSpec((bb, N_STATE), lambda i, j: (i, 0)),
        ],
        out_specs=[
            pl.BlockSpec((bb, tc, D_MODEL), lambda i, j: (i, j, 0)),
            pl.BlockSpec((bb, N_STATE), lambda i, j: (i, 0)),
            pl.BlockSpec((bb, N_STATE), lambda i, j: (i, 0)),
        ],
        out_shape=[
            jax.ShapeDtypeStruct(x.shape, F32),
            jax.ShapeDtypeStruct((bsz, N_STATE), F32),
            jax.ShapeDtypeStruct((bsz, N_STATE), F32),
        ],
        scratch_shapes=[pltpu.VMEM((D_MODEL // LANES, mp, LANES), F32),
                        pltpu.VMEM((D_MODEL // LANES, mp, LANES), F32)],
        compiler_params=pltpu.CompilerParams(
            dimension_semantics=("arbitrary", "arbitrary"), vmem_limit_bytes=VMEM_LIMIT),
        name="ssm",
    )(x, g, wb, wc, lam, d, wglu, h0r, h0i)


def _qk_norm_rope(qk, ones_bd, gain, cos, sina, sinb):
    n_cols = qk.shape[1] // LANES
    sq = qk * qk
    hi = sq.astype(BF16)
    lo = (sq - hi.astype(F32)).astype(BF16)
    outs = []
    for c in range(n_cols // 2):
        sl = slice(c * MXU_DIM, (c + 1) * MXU_DIM)
        ss = (jnp.dot(hi[:, sl], ones_bd, preferred_element_type=F32)
              + jnp.dot(lo[:, sl], ones_bd, preferred_element_type=F32))
        outs.append(qk[:, sl] * lax.rsqrt(ss * (1.0 / HEAD_DIM) + NORM_EPS) * gain[:, sl])
    qn = jnp.concatenate(outs, axis=-1)
    cols = []
    for c in range(n_cols):
        xc = qn[:, c * LANES:(c + 1) * LANES]
        cols.append(xc * cos + pltpu.roll(xc, ROT_DIM // 2, 1) * sina
                    + pltpu.roll(xc, LANES - ROT_DIM // 2, 1) * sinb)
    return jnp.concatenate(cols, axis=-1)


def _lane_halves(x):
    lane = lax.broadcasted_iota(jnp.int32, x.shape, 1)
    low = lane < HEAD_DIM
    return jnp.where(low, x, 0.0), jnp.where(low, 0.0, x)


TQ = 256
ATTN_AHEAD = 2
PROJ_SLOTS = (0, 2, 3, 5, 7, 9)
ONES_ROWS = 16


def _attn_prompt_body(x_ref, g_ref, wqkv_ref, ones_ref, gain_ref, cos_ref, sina_ref, sinb_ref,
                      gaint_ref, cost_ref, sint_ref, bias_ref, sink_ref, wo_ref, o_ref, kc_ref, vc_ref,
                      qt_cur, ka_cur, kb_cur, vt_cur, ot_scr, xprev_scr, qt_nxt, ka_nxt, kb_nxt, vt_nxt,
                      ka_prev, kb_prev, vt_prev, *, blocks_per_seq):
    g = pl.program_id(0)

    @pl.when(g == 0)
    def _():
        for scr in (ka_cur, kb_cur, ka_prev, kb_prev):
            scr[...] = jnp.zeros(scr.shape, BF16)
        for scr in (vt_cur, vt_prev):
            scr[:, 0:HEAD_DIM, :] = jnp.zeros((N_KV_HEADS, HEAD_DIM, scr.shape[2]), BF16)
            scr[:, HEAD_DIM:, :] = jnp.ones((N_KV_HEADS, ONES_ROWS, scr.shape[2]), BF16)
        qt_cur[...] = jnp.zeros((D_MODEL, TQ), BF16)
        xprev_scr[...] = jnp.zeros((TQ, D_MODEL), F32)

    bias_mid = jnp.concatenate([bias_ref[...], bias_ref[...]], axis=1)
    first = (g - 1) % blocks_per_seq == 0
    key_row = lax.broadcasted_iota(jnp.int32, bias_mid.shape, 0)
    bias_first = jnp.where(jnp.logical_and(first, key_row < WINDOW), NEG_INF, bias_mid)
    lane = lax.broadcasted_iota(jnp.int32, (1, 2 * WINDOW), 1)
    pair = 2 * HEAD_DIM

    def key_window(j, h, prev, cur):
        if j == 0:
            return [prev[h], cur[h, 0:WINDOW, :]]
        return [cur[h, (j - 1) * WINDOW:(j + 1) * WINDOW, :]]

    def value_window(j, h):
        if j == 0:
            return jnp.concatenate([vt_prev[h], vt_cur[h, :, 0:WINDOW]], axis=1)
        return vt_cur[h, :, (j - 1) * WINDOW:(j + 1) * WINDOW]

    def scores(j, h):
        qpos = slice(j * WINDOW, (j + 1) * WINDOW)
        qs = jnp.concatenate([qt_cur[(2 * h) * pair:(2 * h + 1) * pair, qpos],
                              qt_cur[(2 * h + 1) * pair:(2 * h + 2) * pair, qpos]], axis=1)
        kk = jnp.concatenate(key_window(j, h, ka_prev, ka_cur) + key_window(j, h, kb_prev, kb_cur),
                             axis=0)
        return jnp.dot(kk, qs, preferred_element_type=F32)

    def softmax(j, h, st):
        bias = bias_first if j == 0 else bias_mid
        out = []
        for par in range(2):
            s = st[par * 2 * WINDOW:(par + 1) * 2 * WINDOW, :] + bias
            sink = jnp.where(lane < WINDOW, sink_ref[4 * h + par], sink_ref[4 * h + 2 + par])
            m = jnp.maximum(jnp.max(s, axis=0, keepdims=True), sink)
            out.append((jnp.exp(s - m).astype(BF16), jnp.exp(sink - m)))
        return out

    def weighted_values(j, h, probs):
        qpos = slice(j * WINDOW, (j + 1) * WINDOW)
        vth = value_window(j, h)
        for par, (p, sink_term) in enumerate(probs):
            nd = jnp.dot(vth, p, preferred_element_type=F32)
            denom = nd[HEAD_DIM:HEAD_DIM + 1, :] + sink_term
            on = nd[:HEAD_DIM, :] * (1.0 / denom)
            for rp in range(2):
                head = 4 * h + 2 * rp + par
                ot_scr[head * HEAD_DIM:(head + 1) * HEAD_DIM, qpos] = on[:, rp * WINDOW:(rp + 1) * WINDOW]

    x = x_ref[0]
    xn = _rms(x, g_ref[...]).astype(BF16)
    tabs = (cos_ref[...], sina_ref[...], sinb_ref[...])
    cos_t = cost_ref[...]
    sin_t = sint_ref[...]
    half = ROT_DIM // 2

    def project(c):
        cols = slice(c * MXU_DIM, (c + 1) * MXU_DIM)
        t = jnp.dot(xn, wqkv_ref[:, cols], preferred_element_type=F32)
        if c * MXU_DIM < D_MODEL:
            tt = t.T
            heads = []
            for hl in range(MXU_DIM // HEAD_DIM):
                blk = tt[hl * HEAD_DIM:(hl + 1) * HEAD_DIM, :]
                ms = jnp.sum(blk * blk, axis=0, keepdims=True) * (1.0 / HEAD_DIM)
                gcol = gaint_ref[c * MXU_DIM + hl * HEAD_DIM:c * MXU_DIM + (hl + 1) * HEAD_DIM, :]
                y = blk * lax.rsqrt(ms + NORM_EPS) * jnp.concatenate([gcol] * (TQ // LANES), axis=1)
                x1, x2 = y[:half], y[half:ROT_DIM]
                heads.append(jnp.concatenate(
                    [x1 * cos_t - x2 * sin_t, x2 * cos_t + x1 * sin_t, y[ROT_DIM:]], axis=0))
            qt_nxt[cols, :] = jnp.concatenate(heads, axis=0).astype(BF16)
        elif c * MXU_DIM < D_QK:
            t = _qk_norm_rope(t, ones_ref[...], gain_ref[:, cols], *tabs)
            kc_ref[0] = t[TQ - WINDOW:, :]
            for h in range(N_KV_HEADS):
                lo_part, hi_part = _lane_halves(t[:, (h // 2) * LANES:(h // 2 + 1) * LANES])
                if h % 2 == 0:
                    a, b = lo_part, pltpu.roll(lo_part, HEAD_DIM, 1)
                else:
                    a, b = pltpu.roll(hi_part, HEAD_DIM, 1), hi_part
                ka_nxt[h] = a.astype(BF16)
                kb_nxt[h] = b.astype(BF16)
        else:
            vc_ref[0] = t[TQ - WINDOW:, :]
            vt = t.T
            for h in range(N_KV_HEADS):
                vt_nxt[h] = vt[h * HEAD_DIM:(h + 1) * HEAD_DIM, :].astype(BF16)

    groups = [(j, h) for j in range(TQ // WINDOW) for h in range(N_KV_HEADS)]
    n_groups = len(groups)
    st, probs = {}, {}
    for n in range(n_groups + ATTN_AHEAD):
        if n < n_groups:
            st[n] = scores(*groups[n])
        if n in PROJ_SLOTS:
            project(PROJ_SLOTS.index(n))
        d = n - ATTN_AHEAD + 1
        if 0 <= d < n_groups:
            probs[d] = softmax(*groups[d], st.pop(d))
        d = n - ATTN_AHEAD
        if 0 <= d < n_groups:
            weighted_values(*groups[d], probs.pop(d))
    attn = ot_scr[...].T.astype(BF16)
    o_ref[0] = xprev_scr[...] + jnp.dot(attn, wo_ref[...], preferred_element_type=F32)

    xprev_scr[...] = x
    qt_cur[...] = qt_nxt[...]
    for prev, cur, nxt in ((ka_prev, ka_cur, ka_nxt), (kb_prev, kb_cur, kb_nxt)):
        prev[...] = cur[:, TQ - WINDOW:, :]
        cur[...] = nxt[...]
    vt_prev[:, 0:HEAD_DIM, :] = vt_cur[:, 0:HEAD_DIM, TQ - WINDOW:]
    vt_cur[:, 0:HEAD_DIM, :] = vt_nxt[...]


def _attn_prompt_call(x, g, wqkv, ones_bd, gain, cos, sina, sinb, bias, sinks, wo, *, layer):
    bsz, seq, _ = x.shape
    bps = seq // TQ
    n_blocks = bsz * bps
    k_blk = pltpu.VMEM((N_KV_HEADS, TQ, LANES), BF16)
    k_tail = pltpu.VMEM((N_KV_HEADS, WINDOW, LANES), BF16)
    vt_blk = pltpu.VMEM((N_KV_HEADS, HEAD_DIM + ONES_ROWS, TQ), BF16)
    vt_tail = pltpu.VMEM((N_KV_HEADS, HEAD_DIM + ONES_ROWS, WINDOW), BF16)

    def proj_block(gi):
        blk = jnp.minimum(gi, n_blocks - 1)
        return blk // bps, blk % bps

    def attn_block(gi):
        blk = jnp.maximum(gi - 1, 0)
        return blk // bps, blk % bps

    tab_spec = pl.BlockSpec((TQ, LANES), lambda gi: (proj_block(gi)[1], 0))
    cache_spec = pl.BlockSpec((1, WINDOW, D_KV), lambda gi: (proj_block(gi)[0], 0, 0))
    half = ROT_DIM // 2
    gain_t = jnp.broadcast_to(gain[0, :D_MODEL, None], (D_MODEL, LANES))
    cos_t = cos[:, :half].T
    sin_t = sina[:, half:ROT_DIM].T
    tab_t_spec = pl.BlockSpec((half, TQ), lambda gi: (0, proj_block(gi)[1]))
    body = functools.partial(_attn_prompt_body, blocks_per_seq=bps)
    return pl.pallas_call(
        body,
        grid=(n_blocks + 1,),
        in_specs=[
            pl.BlockSpec((1, TQ, D_MODEL), lambda gi: (*proj_block(gi), 0)),
            _const_spec((1, D_MODEL)),
            _layer_spec(wqkv, layer),
            _const_spec((MXU_DIM, MXU_DIM)),
            _const_spec((1, D_QK)),
            tab_spec, tab_spec, tab_spec,
            _const_spec((D_MODEL, LANES)),
            tab_t_spec, tab_t_spec,
            _const_spec((2 * WINDOW, WINDOW)),
            pl.BlockSpec(memory_space=pltpu.SMEM),
            _layer_spec(wo, layer),
        ],
        out_specs=[
            pl.BlockSpec((1, TQ, D_MODEL), lambda gi: (*attn_block(gi), 0)),
            cache_spec, cache_spec,
        ],
        out_shape=[
            jax.ShapeDtypeStruct(x.shape, F32),
            jax.ShapeDtypeStruct((bsz, WINDOW, D_KV), F32),
            jax.ShapeDtypeStruct((bsz, WINDOW, D_KV), F32),
        ],
        scratch_shapes=[pltpu.VMEM((D_MODEL, TQ), BF16), k_blk, k_blk, vt_blk,
                        pltpu.VMEM((D_MODEL, TQ), F32),
                        pltpu.VMEM((TQ, D_MODEL), F32),
                        pltpu.VMEM((D_MODEL, TQ), BF16), k_blk, k_blk,
                        pltpu.VMEM((N_KV_HEADS, HEAD_DIM, TQ), BF16),
                        k_tail, k_tail, vt_tail],
        compiler_params=pltpu.CompilerParams(
            dimension_semantics=("arbitrary",), vmem_limit_bytes=VMEM_LIMIT),
        name="attn_prompt",
    )(x, g, wqkv, ones_bd, gain, cos, sina, sinb, gain_t, cos_t, sin_t, bias, sinks, wo)


SEQ_BLOCK = 16
SAMPLE_AHEAD = 6
N_KEYS_PAD = 2 * WINDOW


def _attn_sample_body(x_ref, ck_ref, cv_ref, g_ref, wqkv_ref, ones_ref, gain_ref, cos_ref, sina_ref,
                      sinb_ref, bias_ref, sink_ref, wo_ref, o_ref, kc_ref, vc_ref,
                      kf_scr, vf_scr, oall_scr, *, s_len):
    rows = SEQ_BLOCK * s_len
    w_keep = WINDOW - s_len

    @pl.when(pl.program_id(0) == 0)
    def _():
        pad = (SEQ_BLOCK, N_KEYS_PAD - WINDOW, D_KV)
        kf_scr[:, WINDOW:, :] = jnp.zeros(pad, BF16)
        vf_scr[:, WINDOW:, :] = jnp.zeros(pad, BF16)

    x = x_ref[...].reshape(rows, D_MODEL)
    xn = _rms(x, g_ref[...]).astype(BF16)
    qkv = jnp.dot(xn, wqkv_ref[...], preferred_element_type=F32)
    qk = _qk_norm_rope(qkv[:, :D_QK], ones_ref[...], gain_ref[...],
                       cos_ref[...], sina_ref[...], sinb_ref[...])
    q = qk[:, :D_MODEL]
    k_new = qk[:, D_MODEL:].reshape(SEQ_BLOCK, s_len, D_KV)
    v_new = qkv[:, D_QK:].reshape(SEQ_BLOCK, s_len, D_KV)

    ck = ck_ref[...]
    cv = cv_ref[...]
    kc_ref[:, 0:w_keep, :] = ck[:, s_len:, :]
    vc_ref[:, 0:w_keep, :] = cv[:, s_len:, :]
    kc_ref[:, w_keep:, :] = k_new
    vc_ref[:, w_keep:, :] = v_new
    kf_scr[:, 0:WINDOW, :] = ck.astype(BF16)
    vf_scr[:, 0:WINDOW, :] = cv.astype(BF16)
    tile = 2 * SUBLANES
    zpad = jnp.zeros((SEQ_BLOCK, tile - s_len, D_KV), F32)
    kf_scr[:, WINDOW:WINDOW + tile, :] = jnp.concatenate([k_new, zpad], axis=1).astype(BF16)
    vf_scr[:, WINDOW:WINDOW + tile, :] = jnp.concatenate([v_new, zpad], axis=1).astype(BF16)

    bias = bias_ref[...]
    sink = sink_ref[...]
    nt_dims = (((1,), (1,)), ((), ()))
    zero_col = jnp.zeros((s_len, LANES), F32)
    def scores(s):
        qs = q[s * s_len:(s + 1) * s_len, :]
        tiles = []
        for h in range(N_KV_HEADS):
            for r in range(KV_REP):
                head = KV_REP * h + r
                src = qs[:, (head // 2) * LANES:(head // 2 + 1) * LANES]
                lo_part, hi_part = _lane_halves(src)
                part = lo_part if head % 2 == 0 else hi_part
                if head % 2 != h % 2:
                    part = pltpu.roll(part, HEAD_DIM, 1)
                tiles.append(jnp.concatenate([part, zero_col] if h < 2 else [zero_col, part], axis=1))
        wt = jnp.concatenate(tiles, axis=0).astype(BF16)
        return lax.dot_general(wt, kf_scr[s], nt_dims, preferred_element_type=F32)

    def softmax(sc):
        sc = sc + bias
        m = jnp.maximum(jnp.max(sc, axis=-1, keepdims=True), sink)
        p = jnp.exp(sc - m)
        denom = jnp.sum(p, axis=-1, keepdims=True) + jnp.exp(sink - m)
        return p.astype(BF16), 1.0 / denom

    def weighted_values(s, p, inv_denom):
        o = jnp.dot(p, vf_scr[s], preferred_element_type=F32) * inv_denom
        cols = []
        for c in range(N_HEADS // 2):
            pieces = []
            for head in (2 * c, 2 * c + 1):
                h, r = divmod(head, KV_REP)
                blk = o[(KV_REP * h + r) * s_len:(KV_REP * h + r + 1) * s_len,
                        (h // 2) * LANES:(h // 2 + 1) * LANES]
                lo_part, hi_part = _lane_halves(blk)
                part = lo_part if h % 2 == 0 else hi_part
                if h % 2 != head % 2:
                    part = pltpu.roll(part, HEAD_DIM, 1)
                pieces.append(part)
            cols.append(pieces[0] + pieces[1])
        oall_scr[s * s_len:(s + 1) * s_len, :] = jnp.concatenate(cols, axis=-1)

    sc, probs = {}, {}
    for n in range(SEQ_BLOCK + SAMPLE_AHEAD):
        if n < SEQ_BLOCK:
            sc[n] = scores(n)
        d = n - SAMPLE_AHEAD // 2
        if 0 <= d < SEQ_BLOCK:
            probs[d] = softmax(sc.pop(d))
        d = n - SAMPLE_AHEAD
        if 0 <= d < SEQ_BLOCK:
            weighted_values(d, *probs.pop(d))

    out = jnp.dot(oall_scr[...].astype(BF16), wo_ref[...], preferred_element_type=F32)
    o_ref[...] = (x + out).reshape(SEQ_BLOCK, s_len, D_MODEL)


def _attn_sample_call(x, ck, cv, g, wqkv, ones_bd, gain, cos, sina, sinb, bias, sink_rows, wo, *, layer):
    bsz, s_len, _ = x.shape
    rows = SEQ_BLOCK * s_len
    body = functools.partial(_attn_sample_body, s_len=s_len)
    seq_spec = lambda shape: pl.BlockSpec(shape, lambda i: (i, 0, 0))
    cache_spec = pl.BlockSpec((None, SEQ_BLOCK, WINDOW, D_KV), lambda i: (layer, i, 0, 0))
    return pl.pallas_call(
        body,
        grid=(bsz // SEQ_BLOCK,),
        in_specs=[
            seq_spec((SEQ_BLOCK, s_len, D_MODEL)),
            cache_spec,
            cache_spec,
            _const_spec((1, D_MODEL)),
            _layer_spec(wqkv, layer),
            _const_spec((MXU_DIM, MXU_DIM)),
            _const_spec((1, D_QK)),
            _const_spec((rows, LANES)),
            _const_spec((rows, LANES)),
            _const_spec((rows, LANES)),
            _const_spec((N_HEADS * s_len, N_KEYS_PAD)),
            _const_spec((N_HEADS * s_len, 1)),
            _layer_spec(wo, layer),
        ],
        out_specs=[
            seq_spec((SEQ_BLOCK, s_len, D_MODEL)),
            seq_spec((SEQ_BLOCK, WINDOW, D_KV)),
            seq_spec((SEQ_BLOCK, WINDOW, D_KV)),
        ],
        out_shape=[
            jax.ShapeDtypeStruct(x.shape, F32),
            jax.ShapeDtypeStruct(ck.shape[1:], F32),
            jax.ShapeDtypeStruct(cv.shape[1:], F32),
        ],
        scratch_shapes=[
            pltpu.VMEM((SEQ_BLOCK, N_KEYS_PAD, D_KV), BF16),
            pltpu.VMEM((SEQ_BLOCK, N_KEYS_PAD, D_KV), BF16),
            pltpu.VMEM((rows, D_MODEL), F32),
        ],
        compiler_params=pltpu.CompilerParams(
            dimension_semantics=("arbitrary",), vmem_limit_bytes=VMEM_LIMIT),
        name="attn_sample",
    )(x, ck, cv, g, wqkv, ones_bd, gain, cos, sina, sinb, bias, sink_rows, wo)


def _ssm_weights(a_re, a_im, log_dt, b_re, b_im, c_re, c_im):
    nl = a_re.shape[0]
    dt = jnp.exp(log_dt)[..., None]
    mag = jnp.exp(a_re * dt)
    lam_re = mag * jnp.cos(a_im * dt)
    lam_im = mag * jnp.sin(a_im * dt)
    den = a_re * a_re + a_im * a_im
    nr = lam_re - 1.0
    ni = lam_im
    f_re = (nr * a_re + ni * a_im) / den
    f_im = (ni * a_re - nr * a_im) / den
    bb_re = f_re[..., None] * b_re - f_im[..., None] * b_im
    bb_im = f_re[..., None] * b_im + f_im[..., None] * b_re
    gpt = MXU_DIM // SSM_GROUP
    n_st = gpt * STATE_DIM
    same_group = jnp.asarray(
        (np.arange(MXU_DIM)[:, None] // SSM_GROUP) == (np.arange(n_st)[None, :] // STATE_DIM))

    def b_tiles(bb):
        t = jnp.swapaxes(bb, 2, 3).reshape(nl, N_GTILES, MXU_DIM, STATE_DIM)
        return jnp.where(same_group, jnp.tile(t, (1, 1, 1, gpt)), 0.0)

    def c_tiles(cc):
        t = jnp.swapaxes(cc, 2, 3).reshape(nl, N_GTILES, n_st, SSM_GROUP)
        return jnp.where(same_group.T, jnp.tile(t, (1, 1, 1, gpt)), 0.0)

    wb_re = b_tiles(bb_re).astype(BF16)
    wb_im = b_tiles(bb_im).astype(BF16)
    wb = jnp.concatenate([part[..., j * LANES:(j + 1) * LANES]
                          for j in range(SLABS_PER_TILE) for part in (wb_re, wb_im)], axis=-1)
    wc = jnp.concatenate([c_tiles(c_re), -c_tiles(c_im)], axis=2).astype(BF16)
    lam = jnp.concatenate([lam_re.reshape(nl, N_SLABS, 1, LANES), lam_im.reshape(nl, N_SLABS, 1, LANES)], axis=1)
    lam = jnp.broadcast_to(lam, (nl, 2 * N_SLABS, SUBLANES, LANES))
    return wb, wc, lam


def _rope_tables(pos):
    half = ROT_DIM // 2
    inv_freq = ROPE_THETA ** (-jnp.arange(half, dtype=F32) * 2.0 / ROT_DIM)
    ang = pos[:, None] * inv_freq[None, :]
    cos, sin = jnp.cos(ang), jnp.sin(ang)
    n = pos.shape[0]
    ones = jnp.ones((n, HEAD_DIM - ROT_DIM), F32)
    zeros = jnp.zeros((n, HEAD_DIM - ROT_DIM), F32)
    zh = jnp.zeros((n, half), F32)
    cos_h = jnp.concatenate([cos, cos, ones], axis=1)
    sina_h = jnp.concatenate([zh, sin, zeros], axis=1)
    sinb_h = jnp.concatenate([-sin, zh, zeros], axis=1)
    return tuple(jnp.concatenate([t, t], axis=1) for t in (cos_h, sina_h, sinb_h))


def _ones_block_diag():
    idx = np.arange(MXU_DIM) // HEAD_DIM
    return jnp.asarray(idx[:, None] == idx[None, :], dtype=BF16)


def _prompt_bias():
    js = np.arange(2 * WINDOW)[:, None]
    iq = np.arange(WINDOW)[None, :]
    diff = WINDOW + iq - js
    return jnp.asarray(np.where((diff >= 0) & (diff < WINDOW), 0.0, NEG_INF), dtype=F32)


def _sample_bias(s_len):
    t = (np.arange(N_HEADS * s_len) % s_len)[:, None]
    key = np.arange(N_KEYS_PAD)[None, :]
    diff = np.where(key < WINDOW, WINDOW + t - key, t - (key - WINDOW))
    ok = (diff >= 0) & (diff < WINDOW) & (key < WINDOW + s_len)
    return jnp.asarray(np.where(ok, 0.0, NEG_INF), dtype=F32)


SSM_PROMPT_TC = 64
SSM_SAMPLE_BB = 32
FFN_TM = 1024


def kernel(x_prompt, x_sample, state_ssm_re, state_ssm_im, cache_swa_k, cache_swa_v, norm_mix, norm_ffn, ssm_a_re, ssm_a_im, ssm_log_dt, ssm_b_re, ssm_b_im, ssm_c_re, ssm_c_im, ssm_d, ssm_w_glu, attn_w_qkv, attn_q_norm, attn_k_norm, attn_sinks, attn_w_o, ffn_w_gate_up, ffn_w_down):
    bsz, seq, _ = x_prompt.shape
    dbsz, dseq, _ = x_sample.shape
    yp, ys = x_prompt, x_sample
    p_re, p_im, p_k, p_v, s_re, s_im, s_k, s_v = ([] for _ in range(8))

    ones_bd = _ones_block_diag()
    p_tabs = _rope_tables(jnp.arange(seq, dtype=F32))
    s_pos = PAST_LEN + jnp.arange(dseq, dtype=F32)
    s_tabs = tuple(jnp.tile(t, (SEQ_BLOCK, 1)) for t in _rope_tables(s_pos))
    p_bias = _prompt_bias()
    s_bias = _sample_bias(dseq)
    h0 = jnp.zeros((bsz, N_STATE), F32)

    wb, wc, lam = _ssm_weights(ssm_a_re, ssm_a_im, ssm_log_dt, ssm_b_re, ssm_b_im, ssm_c_re, ssm_c_im)
    wglu = ssm_w_glu.astype(BF16)
    wqkv = attn_w_qkv.astype(BF16)
    wo = attn_w_o.astype(BF16)
    n_swa = cache_swa_k.shape[0]
    ck_all = cache_swa_k.reshape(n_swa, dbsz, WINDOW, D_KV)
    cv_all = cache_swa_v.reshape(n_swa, dbsz, WINDOW, D_KV)

    for i in range(DEPTH):
        j = i // 2
        g_mix = norm_mix[i].reshape(1, D_MODEL)
        if i % 2 == 0:
            d = ssm_d[j].reshape(1, D_MODEL)
            yp, hr, hi = _ssm_call(yp, g_mix, wb, wc, lam, d, wglu, h0, h0,
                                   layer=j, bb=bsz, tc=SSM_PROMPT_TC)
            ys, sr, si = _ssm_call(ys, g_mix, wb, wc, lam, d, wglu,
                                   state_ssm_re[j].reshape(dbsz, N_STATE), state_ssm_im[j].reshape(dbsz, N_STATE),
                                   layer=j, bb=SSM_SAMPLE_BB, tc=dseq)
            shp = (N_GROUPS, STATE_DIM)
            p_re.append(hr.reshape(bsz, *shp))
            p_im.append(hi.reshape(bsz, *shp))
            s_re.append(sr.reshape(dbsz, *shp))
            s_im.append(si.reshape(dbsz, *shp))
        else:
            gain = jnp.concatenate([jnp.tile(attn_q_norm[j] * ATTN_SCALE, N_HEADS),
                                    jnp.tile(attn_k_norm[j], N_KV_HEADS)]).reshape(1, D_QK)
            sinks = attn_sinks[j]
            yp, kp, vp = _attn_prompt_call(yp, g_mix, wqkv, ones_bd, gain, *p_tabs, p_bias, sinks, wo, layer=j)
            sink_rows = jnp.repeat(sinks, dseq).reshape(N_HEADS * dseq, 1)
            ys, ks_, vs_ = _attn_sample_call(ys, ck_all, cv_all, g_mix, wqkv, ones_bd, gain, *s_tabs, s_bias,
                                             sink_rows, wo, layer=j)
            kv_shape = (WINDOW, N_KV_HEADS, HEAD_DIM)
            p_k.append(kp.reshape(bsz, *kv_shape))
            p_v.append(vp.reshape(bsz, *kv_shape))
            s_k.append(ks_.reshape(dbsz, *kv_shape))
            s_v.append(vs_.reshape(dbsz, *kv_shape))
        g_ffn = norm_ffn[i].reshape(1, D_MODEL)
        ys, wg, wu, wd = _ffn_cast_call(ys.reshape(dbsz * dseq, D_MODEL), g_ffn, ffn_w_gate_up, ffn_w_down, layer=i)
        ys = ys.reshape(dbsz, dseq, D_MODEL)
        yp = _ffn_call(yp.reshape(bsz * seq, D_MODEL), g_ffn, wg, wu, wd, tm=FFN_TM).reshape(bsz, seq, D_MODEL)

    return (yp, ys, jnp.stack(p_re), jnp.stack(p_im), jnp.stack(p_k), jnp.stack(p_v),
            jnp.stack(s_re), jnp.stack(s_im), jnp.stack(s_k), jnp.stack(s_v))
```

```python
import functools

import jax
import jax.numpy as jnp
import numpy as np
from jax import lax
from jax.experimental import pallas as pl
from jax.experimental.pallas import tpu as pltpu

F32 = jnp.float32
BF16 = jnp.bfloat16

D_MODEL = 1024
DEPTH = 4
PAST_LEN = 8192
SSM_GROUP = 16
N_GROUPS = D_MODEL // SSM_GROUP
STATE_DIM = 64
N_STATE = N_GROUPS * STATE_DIM
HEAD_DIM = 64
N_HEADS = D_MODEL // HEAD_DIM
N_KV_HEADS = 4
KV_REP = N_HEADS // N_KV_HEADS
D_KV = N_KV_HEADS * HEAD_DIM
D_QK = D_MODEL + D_KV
D_QKV = D_MODEL + 2 * D_KV
WINDOW = 128
ROT_DIM = HEAD_DIM // 4
ROPE_THETA = 500000.0
ATTN_SCALE = HEAD_DIM ** -0.5
D_FF = 2816
NORM_EPS = 1e-6
NEG_INF = -1e30

LANES = 128
SUBLANES = 8
MXU_DIM = 256
N_SLABS = N_STATE // LANES
VMEM_LIMIT = 56 * 1024 * 1024


def _rms(x, g):
    ms = jnp.mean(x * x, axis=-1, keepdims=True)
    return x * lax.rsqrt(ms + NORM_EPS) * g


def _const_spec(shape):
    nd = len(shape)
    return pl.BlockSpec(shape, lambda *_: (0,) * nd, pipeline_mode=pl.Buffered(1))


def _layer_spec(stack, layer):
    nd = stack.ndim - 1
    return pl.BlockSpec((None,) + stack.shape[1:], lambda *_: (layer,) + (0,) * nd,
                        pipeline_mode=pl.Buffered(1))


FF_CHUNK = 256


def _ffn_body(x_ref, g_ref, wg_ref, wu_ref, wd_ref, o_ref, a_scr):
    x = x_ref[...]
    xn = _rms(x, g_ref[...]).astype(BF16)
    for c in range(D_FF // FF_CHUNK):
        gate = jnp.dot(xn, wg_ref[c], preferred_element_type=F32)
        up = jnp.dot(xn, wu_ref[c], preferred_element_type=F32)
        a_scr[:, c * FF_CHUNK:(c + 1) * FF_CHUNK] = (gate * jax.nn.sigmoid(gate) * up).astype(BF16)
    o_ref[...] = x + jnp.dot(a_scr[...], wd_ref[...], preferred_element_type=F32)


def _ffn_call(x, g, wg, wu, wd, *, tm):
    m = x.shape[0]
    return pl.pallas_call(
        _ffn_body,
        grid=(m // tm,),
        in_specs=[
            pl.BlockSpec((tm, D_MODEL), lambda i: (i, 0)),
            _const_spec((1, D_MODEL)),
            _const_spec(wg.shape),
            _const_spec(wu.shape),
            _const_spec(wd.shape),
        ],
        out_specs=pl.BlockSpec((tm, D_MODEL), lambda i: (i, 0)),
        out_shape=jax.ShapeDtypeStruct((m, D_MODEL), F32),
        scratch_shapes=[pltpu.VMEM((tm, D_FF), BF16)],
        compiler_params=pltpu.CompilerParams(
            dimension_semantics=("arbitrary",), vmem_limit_bytes=VMEM_LIMIT),
        name="ffn",
    )(x, g, wg, wu, wd)


def _ffn_cast_body(x_ref, g_ref, wg_ref, wu_ref, wd_ref, o_ref, wgb_ref, wub_ref, wdb_ref, xn_scr):
    c = pl.program_id(0)

    @pl.when(c == 0)
    def _():
        x = x_ref[...]
        xn_scr[...] = _rms(x, g_ref[...]).astype(BF16)
        o_ref[...] = x

    wg = wg_ref[...].astype(BF16)
    wu = wu_ref[...].astype(BF16)
    wd = wd_ref[...].astype(BF16)
    wgb_ref[...] = wg
    wub_ref[...] = wu
    wdb_ref[...] = wd
    xn = xn_scr[...]
    gate = jnp.dot(xn, wg, preferred_element_type=F32)
    up = jnp.dot(xn, wu, preferred_element_type=F32)
    a = (gate * jax.nn.sigmoid(gate) * up).astype(BF16)
    o_ref[...] += jnp.dot(a, wd, preferred_element_type=F32)


def _ffn_cast_call(x, g, wgu, wd, *, layer):
    m = x.shape[0]
    n_chunks = D_FF // FF_CHUNK
    return pl.pallas_call(
        _ffn_cast_body,
        grid=(n_chunks,),
        in_specs=[
            pl.BlockSpec((m, D_MODEL), lambda c: (0, 0)),
            _const_spec((1, D_MODEL)),
            pl.BlockSpec((None, D_MODEL, FF_CHUNK), lambda c: (layer, 0, c)),
            pl.BlockSpec((None, D_MODEL, FF_CHUNK), lambda c: (layer, 0, n_chunks + c)),
            pl.BlockSpec((None, FF_CHUNK, D_MODEL), lambda c: (layer, c, 0)),
        ],
        out_specs=[
            pl.BlockSpec((m, D_MODEL), lambda c: (0, 0)),
            pl.BlockSpec((None, D_MODEL, FF_CHUNK), lambda c: (c, 0, 0)),
            pl.BlockSpec((None, D_MODEL, FF_CHUNK), lambda c: (c, 0, 0)),
            pl.BlockSpec((FF_CHUNK, D_MODEL), lambda c: (c, 0)),
        ],
        out_shape=[
            jax.ShapeDtypeStruct((m, D_MODEL), F32),
            jax.ShapeDtypeStruct((n_chunks, D_MODEL, FF_CHUNK), BF16),
            jax.ShapeDtypeStruct((n_chunks, D_MODEL, FF_CHUNK), BF16),
            jax.ShapeDtypeStruct((D_FF, D_MODEL), BF16),
        ],
        scratch_shapes=[pltpu.VMEM((m, D_MODEL), BF16)],
        compiler_params=pltpu.CompilerParams(
            dimension_semantics=("arbitrary",), vmem_limit_bytes=VMEM_LIMIT),
        name="ffn_cast",
    )(x, g, wgu, wgu, wd)


N_GTILES = D_MODEL // MXU_DIM
SLABS_PER_TILE = N_SLABS // N_GTILES


SSM_AHEAD = 1
SSM_ROW_PAD = 1


def _ssm_body(x_ref, g_ref, wb_ref, wc_ref, lam_ref, d_ref, wglu_ref, h0r_ref, h0i_ref,
              o_ref, hr_ref, hi_ref, xs_scr, os_scr, *, bb, tc):
    pitch = tc + SSM_ROW_PAD
    spt = SLABS_PER_TILE
    n_cols = D_MODEL // LANES
    n_rg = bb // SUBLANES

    @pl.when(pl.program_id(1) == 0)
    def _():
        hr_ref[...] = h0r_ref[...]
        hi_ref[...] = h0i_ref[...]

    def gather_idx(rg, t):
        return pl.ds(rg * SUBLANES * pitch + t, SUBLANES, stride=pitch)

    g = g_ref[...]
    for b in range(bb):
        xn = _rms(x_ref[b], g)
        for c in range(n_cols):
            xs_scr[c, b * pitch:b * pitch + tc, :] = xn[:, c * LANES:(c + 1) * LANES]
    u = jnp.concatenate(
        [jnp.concatenate([xs_scr[c, gather_idx(rg, t), :] for c in range(n_cols)], axis=1)
         for rg in range(n_rg) for t in range(tc)], axis=0)
    ub = u.astype(BF16)

    def project_in(kt):
        return jnp.dot(ub[:, kt * MXU_DIM:(kt + 1) * MXU_DIM], wb_ref[kt], preferred_element_type=F32)

    def scan(kt, bu):
        re_cols, im_cols = [], []
        for j in range(spt):
            s = kt * spt + j
            lanes = slice(s * LANES, (s + 1) * LANES)
            lr = lam_ref[s]
            li = lam_ref[N_SLABS + s]
            re_rows, im_rows = [], []
            for rg in range(n_rg):
                seqs = slice(rg * SUBLANES, (rg + 1) * SUBLANES)
                hr = hr_ref[seqs, lanes]
                hi = hi_ref[seqs, lanes]
                for t in range(tc):
                    rows = slice((rg * tc + t) * SUBLANES, (rg * tc + t + 1) * SUBLANES)
                    nr = (lr * hr - li * hi) + bu[rows, (2 * j) * LANES:(2 * j + 1) * LANES]
                    ni = (lr * hi + li * hr) + bu[rows, (2 * j + 1) * LANES:(2 * j + 2) * LANES]
                    hr, hi = nr, ni
                    re_rows.append(nr)
                    im_rows.append(ni)
                hr_ref[seqs, lanes] = hr
                hi_ref[seqs, lanes] = hi
            re_cols.append(jnp.concatenate(re_rows, axis=0))
            im_cols.append(jnp.concatenate(im_rows, axis=0))
        return jnp.concatenate(re_cols + im_cols, axis=1).astype(BF16)

    ys = []
    bu = {kt: project_in(kt) for kt in range(min(SSM_AHEAD, N_GTILES))}
    for kt in range(N_GTILES):
        if kt + SSM_AHEAD < N_GTILES:
            bu[kt + SSM_AHEAD] = project_in(kt + SSM_AHEAD)
        hs = scan(kt, bu.pop(kt))
        ys.append(jnp.dot(hs, wc_ref[kt], preferred_element_type=F32))
    y = jnp.concatenate(ys, axis=-1) + d_ref[...] * u
    z = jax.nn.gelu(y).astype(BF16)
    gl = jnp.dot(z, wglu_ref[...], preferred_element_type=F32)
    out = gl[:, :D_MODEL] * jax.nn.sigmoid(gl[:, D_MODEL:])
    for rg in range(n_rg):
        for t in range(tc):
            rows = slice((rg * tc + t) * SUBLANES, (rg * tc + t + 1) * SUBLANES)
            for c in range(n_cols):
                os_scr[c, gather_idx(rg, t), :] = out[rows, c * LANES:(c + 1) * LANES]
    for b in range(bb):
        o_ref[b] = x_ref[b] + jnp.concatenate(
            [os_scr[c, b * pitch:b * pitch + tc, :] for c in range(n_cols)], axis=1)


def _ssm_call(x, g, wb, wc, lam, d, wglu, h0r, h0i, *, layer, bb, tc):
    bsz, seq, _ = x.shape
    mp = bb * (tc + SSM_ROW_PAD)
    body = functools.partial(_ssm_body, bb=bb, tc=tc)
    return pl.pallas_call(
        body,
        grid=(bsz // bb, seq // tc),
        in_specs=[
            pl.BlockSpec((bb, tc, D_MODEL), lambda i, j: (i, j, 0)),
            _const_spec((1, D_MODEL)),
            _layer_spec(wb, layer),
            _layer_spec(wc, layer),
            _layer_spec(lam, layer),
            _const_spec((1, D_MODEL)),
            _layer_spec(wglu, layer),
            pl.BlockSpec((bb, N_STATE), lambda i, j: (i, 0)),
            pl.BlockSpec((bb, N_STATE), lambda i, j: (i, 0)),
        ],
        out_specs=[
            pl.BlockSpec((bb, tc, D_MODEL), lambda i, j: (i, j, 0)),
            pl.BlockSpec((bb, N_STATE), lambda i, j: (i, 0)),
            pl.BlockSpec((bb, N_STATE), lambda i, j: (i, 0)),
        ],
        out_shape=[
            jax.ShapeDtypeStruct(x.shape, F32),
            jax.ShapeDtypeStruct((bsz, N_STATE), F32),
            jax.ShapeDtypeStruct((bsz, N_STATE), F32),
        ],
        scratch_shapes=[pltpu.VMEM((D_MODEL // LANES, mp, LANES), F32),
                        pltpu.VMEM((D_MODEL // LANES, mp, LANES), F32)],
        compiler_params=pltpu.CompilerParams(
            dimension_semantics=("arbitrary", "arbitrary"), vmem_limit_bytes=VMEM_LIMIT),
        name="ssm",
    )(x, g, wb, wc, lam, d, wglu, h0r, h0i)


def _qk_norm_rope(qk, ones_bd, gain, cos, sina, sinb):
    n_cols = qk.shape[1] // LANES
    sq = qk * qk
    hi = sq.astype(BF16)
    lo = (sq - hi.astype(F32)).astype(BF16)
    outs = []
    for c in range(n_cols // 2):
        sl = slice(c * MXU_DIM, (c + 1) * MXU_DIM)
        ss = (jnp.dot(hi[:, sl], ones_bd, preferred_element_type=F32)
              + jnp.dot(lo[:, sl], ones_bd, preferred_element_type=F32))
        outs.append(qk[:, sl] * lax.rsqrt(ss * (1.0 / HEAD_DIM) + NORM_EPS) * gain[:, sl])
    qn = jnp.concatenate(outs, axis=-1)
    cols = []
    for c in range(n_cols):
        xc = qn[:, c * LANES:(c + 1) * LANES]
        cols.append(xc * cos + pltpu.roll(xc, ROT_DIM // 2, 1) * sina
                    + pltpu.roll(xc, LANES - ROT_DIM // 2, 1) * sinb)
    return jnp.concatenate(cols, axis=-1)


def _lane_halves(x):
    lane = lax.broadcasted_iota(jnp.int32, x.shape, 1)
    low = lane < HEAD_DIM
    return jnp.where(low, x, 0.0), jnp.where(low, 0.0, x)


TQ = 256
ATTN_AHEAD = 2
PROJ_SLOTS = (0, 2, 3, 5, 7, 9)
ONES_ROWS = 16


def _attn_prompt_body(x_ref, g_ref, wqkv_ref, ones_ref, gain_ref, cos_ref, sina_ref, sinb_ref,
                      gaint_ref, cost_ref, sint_ref, bias_ref, sink_ref, wo_ref, o_ref, kc_ref, vc_ref,
                      qt_cur, ka_cur, kb_cur, vt_cur, ot_scr, xprev_scr, qt_nxt, ka_nxt, kb_nxt, vt_nxt,
                      ka_prev, kb_prev, vt_prev, *, blocks_per_seq):
    g = pl.program_id(0)

    @pl.when(g == 0)
    def _():
        for scr in (ka_cur, kb_cur, ka_prev, kb_prev):
            scr[...] = jnp.zeros(scr.shape, BF16)
        for scr in (vt_cur, vt_prev):
            scr[:, 0:HEAD_DIM, :] = jnp.zeros((N_KV_HEADS, HEAD_DIM, scr.shape[2]), BF16)
            scr[:, HEAD_DIM:, :] = jnp.ones((N_KV_HEADS, ONES_ROWS, scr.shape[2]), BF16)
        qt_cur[...] = jnp.zeros((D_MODEL, TQ), BF16)
        xprev_scr[...] = jnp.zeros((TQ, D_MODEL), F32)

    bias_mid = jnp.concatenate([bias_ref[...], bias_ref[...]], axis=1)
    first = (g - 1) % blocks_per_seq == 0
    key_row = lax.broadcasted_iota(jnp.int32, bias_mid.shape, 0)
    bias_first = jnp.where(jnp.logical_and(first, key_row < WINDOW), NEG_INF, bias_mid)
    lane = lax.broadcasted_iota(jnp.int32, (1, 2 * WINDOW), 1)
    pair = 2 * HEAD_DIM

    def key_window(j, h, prev, cur):
        if j == 0:
            return [prev[h], cur[h, 0:WINDOW, :]]
        return [cur[h, (j - 1) * WINDOW:(j + 1) * WINDOW, :]]

    def value_window(j, h):
        if j == 0:
            return jnp.concatenate([vt_prev[h], vt_cur[h, :, 0:WINDOW]], axis=1)
        return vt_cur[h, :, (j - 1) * WINDOW:(j + 1) * WINDOW]

    def scores(j, h):
        qpos = slice(j * WINDOW, (j + 1) * WINDOW)
        qs = jnp.concatenate([qt_cur[(2 * h) * pair:(2 * h + 1) * pair, qpos],
                              qt_cur[(2 * h + 1) * pair:(2 * h + 2) * pair, qpos]], axis=1)
        kk = jnp.concatenate(key_window(j, h, ka_prev, ka_cur) + key_window(j, h, kb_prev, kb_cur),
                             axis=0)
        return jnp.dot(kk, qs, preferred_element_type=F32)

    def softmax(j, h, st):
        bias = bias_first if j == 0 else bias_mid
        out = []
        for par in range(2):
            s = st[par * 2 * WINDOW:(par + 1) * 2 * WINDOW, :] + bias
            sink = jnp.where(lane < WINDOW, sink_ref[4 * h + par], sink_ref[4 * h + 2 + par])
            m = jnp.maximum(jnp.max(s, axis=0, keepdims=True), sink)
            out.append((jnp.exp(s - m).astype(BF16), jnp.exp(sink - m)))
        return out

    def weighted_values(j, h, probs):
        qpos = slice(j * WINDOW, (j + 1) * WINDOW)
        vth = value_window(j, h)
        for par, (p, sink_term) in enumerate(probs):
            nd = jnp.dot(vth, p, preferred_element_type=F32)
            denom = nd[HEAD_DIM:HEAD_DIM + 1, :] + sink_term
            on = nd[:HEAD_DIM, :] * (1.0 / denom)
            for rp in range(2):
                head = 4 * h + 2 * rp + par
                ot_scr[head * HEAD_DIM:(head + 1) * HEAD_DIM, qpos] = on[:, rp * WINDOW:(rp + 1) * WINDOW]

    x = x_ref[0]
    xn = _rms(x, g_ref[...]).astype(BF16)
    tabs = (cos_ref[...], sina_ref[...], sinb_ref[...])
    cos_t = cost_ref[...]
    sin_t = sint_ref[...]
    half = ROT_DIM // 2

    def project(c):
        cols = slice(c * MXU_DIM, (c + 1) * MXU_DIM)
        t = jnp.dot(xn, wqkv_ref[:, cols], preferred_element_type=F32)
        if c * MXU_DIM < D_MODEL:
            tt = t.T
            heads = []
            for hl in range(MXU_DIM // HEAD_DIM):
                blk = tt[hl * HEAD_DIM:(hl + 1) * HEAD_DIM, :]
                ms = jnp.sum(blk * blk, axis=0, keepdims=True) * (1.0 / HEAD_DIM)
                gcol = gaint_ref[c * MXU_DIM + hl * HEAD_DIM:c * MXU_DIM + (hl + 1) * HEAD_DIM, :]
                y = blk * lax.rsqrt(ms + NORM_EPS) * jnp.concatenate([gcol] * (TQ // LANES), axis=1)
                x1, x2 = y[:half], y[half:ROT_DIM]
                heads.append(jnp.concatenate(
                    [x1 * cos_t - x2 * sin_t, x2 * cos_t + x1 * sin_t, y[ROT_DIM:]], axis=0))
            qt_nxt[cols, :] = jnp.concatenate(heads, axis=0).astype(BF16)
        elif c * MXU_DIM < D_QK:
            t = _qk_norm_rope(t, ones_ref[...], gain_ref[:, cols], *tabs)
            kc_ref[0] = t[TQ - WINDOW:, :]
            for h in range(N_KV_HEADS):
                lo_part, hi_part = _lane_halves(t[:, (h // 2) * LANES:(h // 2 + 1) * LANES])
                if h % 2 == 0:
                    a, b = lo_part, pltpu.roll(lo_part, HEAD_DIM, 1)
                else:
                    a, b = pltpu.roll(hi_part, HEAD_DIM, 1), hi_part
                ka_nxt[h] = a.astype(BF16)
                kb_nxt[h] = b.astype(BF16)
        else:
            vc_ref[0] = t[TQ - WINDOW:, :]
            vt = t.T
            for h in range(N_KV_HEADS):
                vt_nxt[h] = vt[h * HEAD_DIM:(h + 1) * HEAD_DIM, :].astype(BF16)

    groups = [(j, h) for j in range(TQ // WINDOW) for h in range(N_KV_HEADS)]
    n_groups = len(groups)
    st, probs = {}, {}
    for n in range(n_groups + ATTN_AHEAD):
        if n < n_groups:
            st[n] = scores(*groups[n])
        if n in PROJ_SLOTS:
            project(PROJ_SLOTS.index(n))
        d = n - ATTN_AHEAD + 1
        if 0 <= d < n_groups:
            probs[d] = softmax(*groups[d], st.pop(d))
        d = n - ATTN_AHEAD
        if 0 <= d < n_groups:
            weighted_values(*groups[d], probs.pop(d))
    attn = ot_scr[...].T.astype(BF16)
    o_ref[0] = xprev_scr[...] + jnp.dot(attn, wo_ref[...], preferred_element_type=F32)

    xprev_scr[...] = x
    qt_cur[...] = qt_nxt[...]
    for prev, cur, nxt in ((ka_prev, ka_cur, ka_nxt), (kb_prev, kb_cur, kb_nxt)):
        prev[...] = cur[:, TQ - WINDOW:, :]
        cur[...] = nxt[...]
    vt_prev[:, 0:HEAD_DIM, :] = vt_cur[:, 0:HEAD_DIM, TQ - WINDOW:]
    vt_cur[:, 0:HEAD_DIM, :] = vt_nxt[...]


def _attn_prompt_call(x, g, wqkv, ones_bd, gain, cos, sina, sinb, bias, sinks, wo):
    bsz, seq, _ = x.shape
    bps = seq // TQ
    n_blocks = bsz * bps
    k_blk = pltpu.VMEM((N_KV_HEADS, TQ, LANES), BF16)
    k_tail = pltpu.VMEM((N_KV_HEADS, WINDOW, LANES), BF16)
    vt_blk = pltpu.VMEM((N_KV_HEADS, HEAD_DIM + ONES_ROWS, TQ), BF16)
    vt_tail = pltpu.VMEM((N_KV_HEADS, HEAD_DIM + ONES_ROWS, WINDOW), BF16)

    def proj_block(gi):
        blk = jnp.minimum(gi, n_blocks - 1)
        return blk // bps, blk % bps

    def attn_block(gi):
        blk = jnp.maximum(gi - 1, 0)
        return blk // bps, blk % bps

    tab_spec = pl.BlockSpec((TQ, LANES), lambda gi: (proj_block(gi)[1], 0))
    cache_spec = pl.BlockSpec((1, WINDOW, D_KV), lambda gi: (proj_block(gi)[0], 0, 0))
    half = ROT_DIM // 2
    gain_t = jnp.broadcast_to(gain[0, :D_MODEL, None], (D_MODEL, LANES))
    cos_t = cos[:, :half].T
    sin_t = sina[:, half:ROT_DIM].T
    tab_t_spec = pl.BlockSpec((half, TQ), lambda gi: (0, proj_block(gi)[1]))
    body = functools.partial(_attn_prompt_body, blocks_per_seq=bps)
    return pl.pallas_call(
        body,
        grid=(n_blocks + 1,),
        in_specs=[
            pl.BlockSpec((1, TQ, D_MODEL), lambda gi: (*proj_block(gi), 0)),
            _const_spec((1, D_MODEL)),
            _const_spec(wqkv.shape),
            _const_spec((MXU_DIM, MXU_DIM)),
            _const_spec((1, D_QK)),
            tab_spec, tab_spec, tab_spec,
            _const_spec((D_MODEL, LANES)),
            tab_t_spec, tab_t_spec,
            _const_spec((2 * WINDOW, WINDOW)),
            pl.BlockSpec(memory_space=pltpu.SMEM),
            _const_spec(wo.shape),
        ],
        out_specs=[
            pl.BlockSpec((1, TQ, D_MODEL), lambda gi: (*attn_block(gi), 0)),
            cache_spec, cache_spec,
        ],
        out_shape=[
            jax.ShapeDtypeStruct(x.shape, F32),
            jax.ShapeDtypeStruct((bsz, WINDOW, D_KV), F32),
            jax.ShapeDtypeStruct((bsz, WINDOW, D_KV), F32),
        ],
        scratch_shapes=[pltpu.VMEM((D_MODEL, TQ), BF16), k_blk, k_blk, vt_blk,
                        pltpu.VMEM((D_MODEL, TQ), F32),
                        pltpu.VMEM((TQ, D_MODEL), F32),
                        pltpu.VMEM((D_MODEL, TQ), BF16), k_blk, k_blk,
                        pltpu.VMEM((N_KV_HEADS, HEAD_DIM, TQ), BF16),
                        k_tail, k_tail, vt_tail],
        compiler_params=pltpu.CompilerParams(
            dimension_semantics=("arbitrary",), vmem_limit_bytes=VMEM_LIMIT),
        name="attn_prompt",
    )(x, g, wqkv, ones_bd, gain, cos, sina, sinb, gain_t, cos_t, sin_t, bias, sinks, wo)


SEQ_BLOCK = 16
SAMPLE_AHEAD = 6
N_KEYS_PAD = 2 * WINDOW


def _attn_sample_body(x_ref, ck_ref, cv_ref, g_ref, wqkv_ref, ones_ref, gain_ref, cos_ref, sina_ref,
                      sinb_ref, bias_ref, sink_ref, wo_ref, o_ref, kc_ref, vc_ref, wqkvb_ref, wob_ref,
                      kf_scr, vf_scr, oall_scr, *, s_len):
    rows = SEQ_BLOCK * s_len
    w_keep = WINDOW - s_len

    @pl.when(pl.program_id(0) == 0)
    def _():
        pad = (SEQ_BLOCK, N_KEYS_PAD - WINDOW, D_KV)
        kf_scr[:, WINDOW:, :] = jnp.zeros(pad, BF16)
        vf_scr[:, WINDOW:, :] = jnp.zeros(pad, BF16)
        wqkvb_ref[...] = wqkv_ref[...].astype(BF16)
        wob_ref[...] = wo_ref[...].astype(BF16)

    x = x_ref[...].reshape(rows, D_MODEL)
    xn = _rms(x, g_ref[...]).astype(BF16)
    qkv = jnp.dot(xn, wqkvb_ref[...], preferred_element_type=F32)
    qk = _qk_norm_rope(qkv[:, :D_QK], ones_ref[...], gain_ref[...],
                       cos_ref[...], sina_ref[...], sinb_ref[...])
    q = qk[:, :D_MODEL]
    k_new = qk[:, D_MODEL:].reshape(SEQ_BLOCK, s_len, D_KV)
    v_new = qkv[:, D_QK:].reshape(SEQ_BLOCK, s_len, D_KV)

    ck = ck_ref[...]
    cv = cv_ref[...]
    kc_ref[:, 0:w_keep, :] = ck[:, s_len:, :]
    vc_ref[:, 0:w_keep, :] = cv[:, s_len:, :]
    kc_ref[:, w_keep:, :] = k_new
    vc_ref[:, w_keep:, :] = v_new
    kf_scr[:, 0:WINDOW, :] = ck.astype(BF16)
    vf_scr[:, 0:WINDOW, :] = cv.astype(BF16)
    tile = 2 * SUBLANES
    zpad = jnp.zeros((SEQ_BLOCK, tile - s_len, D_KV), F32)
    kf_scr[:, WINDOW:WINDOW + tile, :] = jnp.concatenate([k_new, zpad], axis=1).astype(BF16)
    vf_scr[:, WINDOW:WINDOW + tile, :] = jnp.concatenate([v_new, zpad], axis=1).astype(BF16)

    bias = bias_ref[...]
    sink = sink_ref[...]
    nt_dims = (((1,), (1,)), ((), ()))
    zero_col = jnp.zeros((s_len, LANES), F32)
    def scores(s):
        qs = q[s * s_len:(s + 1) * s_len, :]
        tiles = []
        for h in range(N_KV_HEADS):
            for r in range(KV_REP):
                head = KV_REP * h + r
                src = qs[:, (head // 2) * LANES:(head // 2 + 1) * LANES]
                lo_part, hi_part = _lane_halves(src)
                part = lo_part if head % 2 == 0 else hi_part
                if head % 2 != h % 2:
                    part = pltpu.roll(part, HEAD_DIM, 1)
                tiles.append(jnp.concatenate([part, zero_col] if h < 2 else [zero_col, part], axis=1))
        wt = jnp.concatenate(tiles, axis=0).astype(BF16)
        return lax.dot_general(wt, kf_scr[s], nt_dims, preferred_element_type=F32)

    def softmax(sc):
        sc = sc + bias
        m = jnp.maximum(jnp.max(sc, axis=-1, keepdims=True), sink)
        p = jnp.exp(sc - m)
        denom = jnp.sum(p, axis=-1, keepdims=True) + jnp.exp(sink - m)
        return p.astype(BF16), 1.0 / denom

    def weighted_values(s, p, inv_denom):
        o = jnp.dot(p, vf_scr[s], preferred_element_type=F32) * inv_denom
        cols = []
        for c in range(N_HEADS // 2):
            pieces = []
            for head in (2 * c, 2 * c + 1):
                h, r = divmod(head, KV_REP)
                blk = o[(KV_REP * h + r) * s_len:(KV_REP * h + r + 1) * s_len,
                        (h // 2) * LANES:(h // 2 + 1) * LANES]
                lo_part, hi_part = _lane_halves(blk)
                part = lo_part if h % 2 == 0 else hi_part
                if h % 2 != head % 2:
                    part = pltpu.roll(part, HEAD_DIM, 1)
                pieces.append(part)
            cols.append(pieces[0] + pieces[1])
        oall_scr[s * s_len:(s + 1) * s_len, :] = jnp.concatenate(cols, axis=-1)

    sc, probs = {}, {}
    for n in range(SEQ_BLOCK + SAMPLE_AHEAD):
        if n < SEQ_BLOCK:
            sc[n] = scores(n)
        d = n - SAMPLE_AHEAD // 2
        if 0 <= d < SEQ_BLOCK:
            probs[d] = softmax(sc.pop(d))
        d = n - SAMPLE_AHEAD
        if 0 <= d < SEQ_BLOCK:
            weighted_values(d, *probs.pop(d))

    out = jnp.dot(oall_scr[...].astype(BF16), wob_ref[...], preferred_element_type=F32)
    o_ref[...] = (x + out).reshape(SEQ_BLOCK, s_len, D_MODEL)


def _attn_sample_call(x, ck, cv, g, wqkv, ones_bd, gain, cos, sina, sinb, bias, sink_rows, wo, *, layer):
    bsz, s_len, _ = x.shape
    rows = SEQ_BLOCK * s_len
    body = functools.partial(_attn_sample_body, s_len=s_len)
    seq_spec = lambda shape: pl.BlockSpec(shape, lambda i: (i, 0, 0))
    cache_spec = pl.BlockSpec((None, SEQ_BLOCK, WINDOW, D_KV), lambda i: (layer, i, 0, 0))
    return pl.pallas_call(
        body,
        grid=(bsz // SEQ_BLOCK,),
        in_specs=[
            seq_spec((SEQ_BLOCK, s_len, D_MODEL)),
            cache_spec,
            cache_spec,
            _const_spec((1, D_MODEL)),
            _layer_spec(wqkv, layer),
            _const_spec((MXU_DIM, MXU_DIM)),
            _const_spec((1, D_QK)),
            _const_spec((rows, LANES)),
            _const_spec((rows, LANES)),
            _const_spec((rows, LANES)),
            _const_spec((N_HEADS * s_len, N_KEYS_PAD)),
            _const_spec((N_HEADS * s_len, 1)),
            _layer_spec(wo, layer),
        ],
        out_specs=[
            seq_spec((SEQ_BLOCK, s_len, D_MODEL)),
            seq_spec((SEQ_BLOCK, WINDOW, D_KV)),
            seq_spec((SEQ_BLOCK, WINDOW, D_KV)),
            pl.BlockSpec(wqkv.shape[1:], lambda i: (0, 0)),
            pl.BlockSpec(wo.shape[1:], lambda i: (0, 0)),
        ],
        out_shape=[
            jax.ShapeDtypeStruct(x.shape, F32),
            jax.ShapeDtypeStruct(ck.shape[1:], F32),
            jax.ShapeDtypeStruct(cv.shape[1:], F32),
            jax.ShapeDtypeStruct(wqkv.shape[1:], BF16),
            jax.ShapeDtypeStruct(wo.shape[1:], BF16),
        ],
        scratch_shapes=[
            pltpu.VMEM((SEQ_BLOCK, N_KEYS_PAD, D_KV), BF16),
            pltpu.VMEM((SEQ_BLOCK, N_KEYS_PAD, D_KV), BF16),
            pltpu.VMEM((rows, D_MODEL), F32),
        ],
        compiler_params=pltpu.CompilerParams(
            dimension_semantics=("arbitrary",), vmem_limit_bytes=VMEM_LIMIT),
        name="attn_sample",
    )(x, ck, cv, g, wqkv, ones_bd, gain, cos, sina, sinb, bias, sink_rows, wo)


def _ssm_weights(a_re, a_im, log_dt, b_re, b_im, c_re, c_im):
    nl = a_re.shape[0]
    dt = jnp.exp(log_dt)[..., None]
    mag = jnp.exp(a_re * dt)
    lam_re = mag * jnp.cos(a_im * dt)
    lam_im = mag * jnp.sin(a_im * dt)
    den = a_re * a_re + a_im * a_im
    nr = lam_re - 1.0
    ni = lam_im
    f_re = (nr * a_re + ni * a_im) / den
    f_im = (ni * a_re - nr * a_im) / den
    bb_re = f_re[..., None] * b_re - f_im[..., None] * b_im
    bb_im = f_re[..., None] * b_im + f_im[..., None] * b_re
    gpt = MXU_DIM // SSM_GROUP
    n_st = gpt * STATE_DIM
    same_group = jnp.asarray(
        (np.arange(MXU_DIM)[:, None] // SSM_GROUP) == (np.arange(n_st)[None, :] // STATE_DIM))

    def b_tiles(bb):
        t = jnp.swapaxes(bb, 2, 3).reshape(nl, N_GTILES, MXU_DIM, STATE_DIM)
        return jnp.where(same_group, jnp.tile(t, (1, 1, 1, gpt)), 0.0)

    def c_tiles(cc):
        t = jnp.swapaxes(cc, 2, 3).reshape(nl, N_GTILES, n_st, SSM_GROUP)
        return jnp.where(same_group.T, jnp.tile(t, (1, 1, 1, gpt)), 0.0)

    wb_re = b_tiles(bb_re).astype(BF16)
    wb_im = b_tiles(bb_im).astype(BF16)
    wb = jnp.concatenate([part[..., j * LANES:(j + 1) * LANES]
                          for j in range(SLABS_PER_TILE) for part in (wb_re, wb_im)], axis=-1)
    wc = jnp.concatenate([c_tiles(c_re), -c_tiles(c_im)], axis=2).astype(BF16)
    lam = jnp.concatenate([lam_re.reshape(nl, N_SLABS, 1, LANES), lam_im.reshape(nl, N_SLABS, 1, LANES)], axis=1)
    lam = jnp.broadcast_to(lam, (nl, 2 * N_SLABS, SUBLANES, LANES))
    return wb, wc, lam


def _rope_tables(pos):
    half = ROT_DIM // 2
    inv_freq = ROPE_THETA ** (-jnp.arange(half, dtype=F32) * 2.0 / ROT_DIM)
    ang = pos[:, None] * inv_freq[None, :]
    cos, sin = jnp.cos(ang), jnp.sin(ang)
    n = pos.shape[0]
    ones = jnp.ones((n, HEAD_DIM - ROT_DIM), F32)
    zeros = jnp.zeros((n, HEAD_DIM - ROT_DIM), F32)
    zh = jnp.zeros((n, half), F32)
    cos_h = jnp.concatenate([cos, cos, ones], axis=1)
    sina_h = jnp.concatenate([zh, sin, zeros], axis=1)
    sinb_h = jnp.concatenate([-sin, zh, zeros], axis=1)
    return tuple(jnp.concatenate([t, t], axis=1) for t in (cos_h, sina_h, sinb_h))


def _ones_block_diag():
    idx = np.arange(MXU_DIM) // HEAD_DIM
    return jnp.asarray(idx[:, None] == idx[None, :], dtype=BF16)


def _prompt_bias():
    js = np.arange(2 * WINDOW)[:, None]
    iq = np.arange(WINDOW)[None, :]
    diff = WINDOW + iq - js
    return jnp.asarray(np.where((diff >= 0) & (diff < WINDOW), 0.0, NEG_INF), dtype=F32)


def _sample_bias(s_len):
    t = (np.arange(N_HEADS * s_len) % s_len)[:, None]
    key = np.arange(N_KEYS_PAD)[None, :]
    diff = np.where(key < WINDOW, WINDOW + t - key, t - (key - WINDOW))
    ok = (diff >= 0) & (diff < WINDOW) & (key < WINDOW + s_len)
    return jnp.asarray(np.where(ok, 0.0, NEG_INF), dtype=F32)


SSM_PROMPT_TC = 64
SSM_SAMPLE_BB = 32
FFN_TM = 1024


def kernel(x_prompt, x_sample, state_ssm_re, state_ssm_im, cache_swa_k, cache_swa_v, norm_mix, norm_ffn, ssm_a_re, ssm_a_im, ssm_log_dt, ssm_b_re, ssm_b_im, ssm_c_re, ssm_c_im, ssm_d, ssm_w_glu, attn_w_qkv, attn_q_norm, attn_k_norm, attn_sinks, attn_w_o, ffn_w_gate_up, ffn_w_down):
    bsz, seq, _ = x_prompt.shape
    dbsz, dseq, _ = x_sample.shape
    yp, ys = x_prompt, x_sample
    p_re, p_im, p_k, p_v, s_re, s_im, s_k, s_v = ([] for _ in range(8))

    ones_bd = _ones_block_diag()
    p_tabs = _rope_tables(jnp.arange(seq, dtype=F32))
    s_pos = PAST_LEN + jnp.arange(dseq, dtype=F32)
    s_tabs = tuple(jnp.tile(t, (SEQ_BLOCK, 1)) for t in _rope_tables(s_pos))
    p_bias = _prompt_bias()
    s_bias = _sample_bias(dseq)
    h0 = jnp.zeros((bsz, N_STATE), F32)

    wb, wc, lam = _ssm_weights(ssm_a_re, ssm_a_im, ssm_log_dt, ssm_b_re, ssm_b_im, ssm_c_re, ssm_c_im)
    wglu = ssm_w_glu.astype(BF16)
    n_swa = cache_swa_k.shape[0]
    ck_all = cache_swa_k.reshape(n_swa, dbsz, WINDOW, D_KV)
    cv_all = cache_swa_v.reshape(n_swa, dbsz, WINDOW, D_KV)

    for i in range(DEPTH):
        j = i // 2
        g_mix = norm_mix[i].reshape(1, D_MODEL)
        if i % 2 == 0:
            d = ssm_d[j].reshape(1, D_MODEL)
            yp, hr, hi = _ssm_call(yp, g_mix, wb, wc, lam, d, wglu, h0, h0,
                                   layer=j, bb=bsz, tc=SSM_PROMPT_TC)
            ys, sr, si = _ssm_call(ys, g_mix, wb, wc, lam, d, wglu,
                                   state_ssm_re[j].reshape(dbsz, N_STATE), state_ssm_im[j].reshape(dbsz, N_STATE),
                                   layer=j, bb=SSM_SAMPLE_BB, tc=dseq)
            shp = (N_GROUPS, STATE_DIM)
            p_re.append(hr.reshape(bsz, *shp))
            p_im.append(hi.reshape(bsz, *shp))
            s_re.append(sr.reshape(dbsz, *shp))
            s_im.append(si.reshape(dbsz, *shp))
        else:
            gain = jnp.concatenate([jnp.tile(attn_q_norm[j] * ATTN_SCALE, N_HEADS),
                                    jnp.tile(attn_k_norm[j], N_KV_HEADS)]).reshape(1, D_QK)
            sinks = attn_sinks[j]
            sink_rows = jnp.repeat(sinks, dseq).reshape(N_HEADS * dseq, 1)
            ys, ks_, vs_, wqkv, wo = _attn_sample_call(ys, ck_all, cv_all, g_mix, attn_w_qkv, ones_bd, gain, *s_tabs,
                                                       s_bias, sink_rows, attn_w_o, layer=j)
            yp, kp, vp = _attn_prompt_call(yp, g_mix, wqkv, ones_bd, gain, *p_tabs, p_bias, sinks, wo)
            kv_shape = (WINDOW, N_KV_HEADS, HEAD_DIM)
            p_k.append(kp.reshape(bsz, *kv_shape))
            p_v.append(vp.reshape(bsz, *kv_shape))
            s_k.append(ks_.reshape(dbsz, *kv_shape))
            s_v.append(vs_.reshape(dbsz, *kv_shape))
        g_ffn = norm_ffn[i].reshape(1, D_MODEL)
        ys, wg, wu, wd = _ffn_cast_call(ys.reshape(dbsz * dseq, D_MODEL), g_ffn, ffn_w_gate_up, ffn_w_down, layer=i)
        ys = ys.reshape(dbsz, dseq, D_MODEL)
        yp = _ffn_call(yp.reshape(bsz * seq, D_MODEL), g_ffn, wg, wu, wd, tm=FFN_TM).reshape(bsz, seq, D_MODEL)

    return (yp, ys, jnp.stack(p_re), jnp.stack(p_im), jnp.stack(p_k), jnp.stack(p_v),
            jnp.stack(s_re), jnp.stack(s_im), jnp.stack(s_k), jnp.stack(s_v))
```

```python
import functools

import jax
import jax.numpy as jnp
import numpy as np
from jax import lax
from jax.experimental import pallas as pl
from jax.experimental.pallas import tpu as pltpu

F32 = jnp.float32
BF16 = jnp.bfloat16

D_MODEL = 1024
DEPTH = 4
PAST_LEN = 8192
SSM_GROUP = 16
N_GROUPS = D_MODEL // SSM_GROUP
STATE_DIM = 64
N_STATE = N_GROUPS * STATE_DIM
HEAD_DIM = 64
N_HEADS = D_MODEL // HEAD_DIM
N_KV_HEADS = 4
KV_REP = N_HEADS // N_KV_HEADS
D_KV = N_KV_HEADS * HEAD_DIM
D_QK = D_MODEL + D_KV
D_QKV = D_MODEL + 2 * D_KV
WINDOW = 128
ROT_DIM = HEAD_DIM // 4
ROPE_THETA = 500000.0
ATTN_SCALE = HEAD_DIM ** -0.5
D_FF = 2816
NORM_EPS = 1e-6
NEG_INF = -1e30

LANES = 128
SUBLANES = 8
MXU_DIM = 256
N_SLABS = N_STATE // LANES
VMEM_LIMIT = 56 * 1024 * 1024


def _rms(x, g):
    ms = jnp.mean(x * x, axis=-1, keepdims=True)
    return x * lax.rsqrt(ms + NORM_EPS) * g


def _const_spec(shape):
    nd = len(shape)
    return pl.BlockSpec(shape, lambda *_: (0,) * nd, pipeline_mode=pl.Buffered(1))


def _layer_spec(stack, layer):
    nd = stack.ndim - 1
    return pl.BlockSpec((None,) + stack.shape[1:], lambda *_: (layer,) + (0,) * nd,
                        pipeline_mode=pl.Buffered(1))


FF_CHUNK = 256


def _ffn_body(x_ref, g_ref, wg_ref, wu_ref, wd_ref, o_ref, a_scr):
    x = x_ref[...]
    xn = _rms(x, g_ref[...]).astype(BF16)
    for c in range(D_FF // FF_CHUNK):
        gate = jnp.dot(xn, wg_ref[c], preferred_element_type=F32)
        up = jnp.dot(xn, wu_ref[c], preferred_element_type=F32)
        a_scr[:, c * FF_CHUNK:(c + 1) * FF_CHUNK] = (gate * jax.nn.sigmoid(gate) * up).astype(BF16)
    o_ref[...] = x + jnp.dot(a_scr[...], wd_ref[...], preferred_element_type=F32)


def _ffn_call(x, g, wg, wu, wd, *, tm):
    m = x.shape[0]
    return pl.pallas_call(
        _ffn_body,
        grid=(m // tm,),
        in_specs=[
            pl.BlockSpec((tm, D_MODEL), lambda i: (i, 0)),
            _const_spec((1, D_MODEL)),
            _const_spec(wg.shape),
            _const_spec(wu.shape),
            _const_spec(wd.shape),
        ],
        out_specs=pl.BlockSpec((tm, D_MODEL), lambda i: (i, 0)),
        out_shape=jax.ShapeDtypeStruct((m, D_MODEL), F32),
        scratch_shapes=[pltpu.VMEM((tm, D_FF), BF16)],
        compiler_params=pltpu.CompilerParams(
            dimension_semantics=("arbitrary",), vmem_limit_bytes=VMEM_LIMIT),
        name="ffn",
    )(x, g, wg, wu, wd)


def _ffn_cast_body(x_ref, g_ref, wg_ref, wu_ref, wd_ref, o_ref, wgb_ref, wub_ref, wdb_ref, xn_scr):
    c = pl.program_id(0)

    @pl.when(c == 0)
    def _():
        x = x_ref[...]
        xn_scr[...] = _rms(x, g_ref[...]).astype(BF16)
        o_ref[...] = x

    wg = wg_ref[...].astype(BF16)
    wu = wu_ref[...].astype(BF16)
    wd = wd_ref[...].astype(BF16)
    wgb_ref[...] = wg
    wub_ref[...] = wu
    wdb_ref[...] = wd
    xn = xn_scr[...]
    gate = jnp.dot(xn, wg, preferred_element_type=F32)
    up = jnp.dot(xn, wu, preferred_element_type=F32)
    a = (gate * jax.nn.sigmoid(gate) * up).astype(BF16)
    o_ref[...] += jnp.dot(a, wd, preferred_element_type=F32)


def _ffn_cast_call(x, g, wgu, wd, *, layer):
    m = x.shape[0]
    n_chunks = D_FF // FF_CHUNK
    return pl.pallas_call(
        _ffn_cast_body,
        grid=(n_chunks,),
        in_specs=[
            pl.BlockSpec((m, D_MODEL), lambda c: (0, 0)),
            _const_spec((1, D_MODEL)),
            pl.BlockSpec((None, D_MODEL, FF_CHUNK), lambda c: (layer, 0, c)),
            pl.BlockSpec((None, D_MODEL, FF_CHUNK), lambda c: (layer, 0, n_chunks + c)),
            pl.BlockSpec((None, FF_CHUNK, D_MODEL), lambda c: (layer, c, 0)),
        ],
        out_specs=[
            pl.BlockSpec((m, D_MODEL), lambda c: (0, 0)),
            pl.BlockSpec((None, D_MODEL, FF_CHUNK), lambda c: (c, 0, 0)),
            pl.BlockSpec((None, D_MODEL, FF_CHUNK), lambda c: (c, 0, 0)),
            pl.BlockSpec((FF_CHUNK, D_MODEL), lambda c: (c, 0)),
        ],
        out_shape=[
            jax.ShapeDtypeStruct((m, D_MODEL), F32),
            jax.ShapeDtypeStruct((n_chunks, D_MODEL, FF_CHUNK), BF16),
            jax.ShapeDtypeStruct((n_chunks, D_MODEL, FF_CHUNK), BF16),
            jax.ShapeDtypeStruct((D_FF, D_MODEL), BF16),
        ],
        scratch_shapes=[pltpu.VMEM((m, D_MODEL), BF16)],
        compiler_params=pltpu.CompilerParams(
            dimension_semantics=("arbitrary",), vmem_limit_bytes=VMEM_LIMIT),
        name="ffn_cast",
    )(x, g, wgu, wgu, wd)


N_GTILES = D_MODEL // MXU_DIM
SLABS_PER_TILE = N_SLABS // N_GTILES


SSM_AHEAD = 1
SSM_ROW_PAD = 1


def _ssm_body(x_ref, g_ref, wb_ref, wc_ref, lam_ref, d_ref, wglu_ref, h0r_ref, h0i_ref,
              o_ref, hr_ref, hi_ref, *rest, bb, tc, cast_glu):
    pitch = tc + SSM_ROW_PAD
    spt = SLABS_PER_TILE
    n_cols = D_MODEL // LANES
    n_rg = bb // SUBLANES

    @pl.when(pl.program_id(1) == 0)
    def _():
        hr_ref[...] = h0r_ref[...]
        hi_ref[...] = h0i_ref[...]

    if cast_glu:
        wglub_ref, xs_scr, os_scr = rest

        @pl.when(jnp.logical_and(pl.program_id(0) == 0, pl.program_id(1) == 0))
        def _():
            wglub_ref[...] = wglu_ref[...].astype(BF16)

        wglu_ref = wglub_ref
    else:
        xs_scr, os_scr = rest

    def gather_idx(rg, t):
        return pl.ds(rg * SUBLANES * pitch + t, SUBLANES, stride=pitch)

    g = g_ref[...]
    for b in range(bb):
        xn = _rms(x_ref[b], g)
        for c in range(n_cols):
            xs_scr[c, b * pitch:b * pitch + tc, :] = xn[:, c * LANES:(c + 1) * LANES]
    u = jnp.concatenate(
        [jnp.concatenate([xs_scr[c, gather_idx(rg, t), :] for c in range(n_cols)], axis=1)
         for rg in range(n_rg) for t in range(tc)], axis=0)
    ub = u.astype(BF16)

    def project_in(kt):
        return jnp.dot(ub[:, kt * MXU_DIM:(kt + 1) * MXU_DIM], wb_ref[kt], preferred_element_type=F32)

    def scan(kt, bu):
        re_cols, im_cols = [], []
        for j in range(spt):
            s = kt * spt + j
            lanes = slice(s * LANES, (s + 1) * LANES)
            lr = lam_ref[s]
            li = lam_ref[N_SLABS + s]
            re_rows, im_rows = [], []
            for rg in range(n_rg):
                seqs = slice(rg * SUBLANES, (rg + 1) * SUBLANES)
                hr = hr_ref[seqs, lanes]
                hi = hi_ref[seqs, lanes]
                for t in range(tc):
                    rows = slice((rg * tc + t) * SUBLANES, (rg * tc + t + 1) * SUBLANES)
                    nr = (lr * hr - li * hi) + bu[rows, (2 * j) * LANES:(2 * j + 1) * LANES]
                    ni = (lr * hi + li * hr) + bu[rows, (2 * j + 1) * LANES:(2 * j + 2) * LANES]
                    hr, hi = nr, ni
                    re_rows.append(nr)
                    im_rows.append(ni)
                hr_ref[seqs, lanes] = hr
                hi_ref[seqs, lanes] = hi
            re_cols.append(jnp.concatenate(re_rows, axis=0))
            im_cols.append(jnp.concatenate(im_rows, axis=0))
        return jnp.concatenate(re_cols + im_cols, axis=1).astype(BF16)

    ys = []
    bu = {kt: project_in(kt) for kt in range(min(SSM_AHEAD, N_GTILES))}
    for kt in range(N_GTILES):
        if kt + SSM_AHEAD < N_GTILES:
            bu[kt + SSM_AHEAD] = project_in(kt + SSM_AHEAD)
        hs = scan(kt, bu.pop(kt))
        ys.append(jnp.dot(hs, wc_ref[kt], preferred_element_type=F32))
    y = jnp.concatenate(ys, axis=-1) + d_ref[...] * u
    z = jax.nn.gelu(y).astype(BF16)
    gl = jnp.dot(z, wglu_ref[...], preferred_element_type=F32)
    out = gl[:, :D_MODEL] * jax.nn.sigmoid(gl[:, D_MODEL:])
    for rg in range(n_rg):
        for t in range(tc):
            rows = slice((rg * tc + t) * SUBLANES, (rg * tc + t + 1) * SUBLANES)
            for c in range(n_cols):
                os_scr[c, gather_idx(rg, t), :] = out[rows, c * LANES:(c + 1) * LANES]
    for b in range(bb):
        o_ref[b] = x_ref[b] + jnp.concatenate(
            [os_scr[c, b * pitch:b * pitch + tc, :] for c in range(n_cols)], axis=1)


def _ssm_call(x, g, wb, wc, lam, d, wglu, h0r, h0i, *, layer, bb, tc):
    bsz, seq, _ = x.shape
    mp = bb * (tc + SSM_ROW_PAD)
    cast_glu = wglu.dtype == F32
    body = functools.partial(_ssm_body, bb=bb, tc=tc, cast_glu=cast_glu)
    glu_shape = (D_MODEL, 2 * D_MODEL)
    extra_specs = [pl.BlockSpec(glu_shape, lambda i, j: (0, 0))] if cast_glu else []
    extra_shapes = [jax.ShapeDtypeStruct(glu_shape, BF16)] if cast_glu else []
    return pl.pallas_call(
        body,
        grid=(bsz // bb, seq // tc),
        in_specs=[
            pl.BlockSpec((bb, tc, D_MODEL), lambda i, j: (i, j, 0)),
            _const_spec((1, D_MODEL)),
            _layer_spec(wb, layer),
            _layer_spec(wc, layer),
            _layer_spec(lam, layer),
            _const_spec((1, D_MODEL)),
            _layer_spec(wglu, layer) if cast_glu else _const_spec(glu_shape),
            pl.BlockSpec((bb, N_STATE), lambda i, j: (i, 0)),
            pl.BlockSpec((bb, N_STATE), lambda i, j: (i, 0)),
        ],
        out_specs=[
            pl.BlockSpec((bb, tc, D_MODEL), lambda i, j: (i, j, 0)),
            pl.BlockSpec((bb, N_STATE), lambda i, j: (i, 0)),
            pl.BlockSpec((bb, N_STATE), lambda i, j: (i, 0)),
        ] + extra_specs,
        out_shape=[
            jax.ShapeDtypeStruct(x.shape, F32),
            jax.ShapeDtypeStruct((bsz, N_STATE), F32),
            jax.ShapeDtypeStruct((bsz, N_STATE), F32),
        ] + extra_shapes,
        scratch_shapes=[pltpu.VMEM((D_MODEL // LANES, mp, LANES), F32),
                        pltpu.VMEM((D_MODEL // LANES, mp, LANES), F32)],
        compiler_params=pltpu.CompilerParams(
            dimension_semantics=("arbitrary", "arbitrary"), vmem_limit_bytes=VMEM_LIMIT),
        name="ssm",
    )(x, g, wb, wc, lam, d, wglu, h0r, h0i)


def _qk_norm_rope(qk, ones_bd, gain, cos, sina, sinb):
    n_cols = qk.shape[1] // LANES
    sq = qk * qk
    hi = sq.astype(BF16)
    lo = (sq - hi.astype(F32)).astype(BF16)
    outs = []
    for c in range(n_cols // 2):
        sl = slice(c * MXU_DIM, (c + 1) * MXU_DIM)
        ss = (jnp.dot(hi[:, sl], ones_bd, preferred_element_type=F32)
              + jnp.dot(lo[:, sl], ones_bd, preferred_element_type=F32))
        outs.append(qk[:, sl] * lax.rsqrt(ss * (1.0 / HEAD_DIM) + NORM_EPS) * gain[:, sl])
    qn = jnp.concatenate(outs, axis=-1)
    cols = []
    for c in range(n_cols):
        xc = qn[:, c * LANES:(c + 1) * LANES]
        cols.append(xc * cos + pltpu.roll(xc, ROT_DIM // 2, 1) * sina
                    + pltpu.roll(xc, LANES - ROT_DIM // 2, 1) * sinb)
    return jnp.concatenate(cols, axis=-1)


def _lane_halves(x):
    lane = lax.broadcasted_iota(jnp.int32, x.shape, 1)
    low = lane < HEAD_DIM
    return jnp.where(low, x, 0.0), jnp.where(low, 0.0, x)


TQ = 256
ATTN_AHEAD = 2
PROJ_SLOTS = (0, 2, 3, 5, 7, 9)
ONES_ROWS = 16


def _attn_prompt_body(x_ref, g_ref, wqkv_ref, ones_ref, gain_ref, cos_ref, sina_ref, sinb_ref,
                      gaint_ref, cost_ref, sint_ref, bias_ref, sink_ref, wo_ref, o_ref, kc_ref, vc_ref,
                      qt_cur, ka_cur, kb_cur, vt_cur, ot_scr, xprev_scr, qt_nxt, ka_nxt, kb_nxt, vt_nxt,
                      ka_prev, kb_prev, vt_prev, *, blocks_per_seq):
    g = pl.program_id(0)

    @pl.when(g == 0)
    def _():
        for scr in (ka_cur, kb_cur, ka_prev, kb_prev):
            scr[...] = jnp.zeros(scr.shape, BF16)
        for scr in (vt_cur, vt_prev):
            scr[:, 0:HEAD_DIM, :] = jnp.zeros((N_KV_HEADS, HEAD_DIM, scr.shape[2]), BF16)
            scr[:, HEAD_DIM:, :] = jnp.ones((N_KV_HEADS, ONES_ROWS, scr.shape[2]), BF16)
        qt_cur[...] = jnp.zeros((D_MODEL, TQ), BF16)
        xprev_scr[...] = jnp.zeros((TQ, D_MODEL), F32)

    bias_mid = jnp.concatenate([bias_ref[...], bias_ref[...]], axis=1)
    first = (g - 1) % blocks_per_seq == 0
    key_row = lax.broadcasted_iota(jnp.int32, bias_mid.shape, 0)
    bias_first = jnp.where(jnp.logical_and(first, key_row < WINDOW), NEG_INF, bias_mid)
    lane = lax.broadcasted_iota(jnp.int32, (1, 2 * WINDOW), 1)
    pair = 2 * HEAD_DIM

    def key_window(j, h, prev, cur):
        if j == 0:
            return [prev[h], cur[h, 0:WINDOW, :]]
        return [cur[h, (j - 1) * WINDOW:(j + 1) * WINDOW, :]]

    def value_window(j, h):
        if j == 0:
            return jnp.concatenate([vt_prev[h], vt_cur[h, :, 0:WINDOW]], axis=1)
        return vt_cur[h, :, (j - 1) * WINDOW:(j + 1) * WINDOW]

    def scores(j, h):
        qpos = slice(j * WINDOW, (j + 1) * WINDOW)
        qs = jnp.concatenate([qt_cur[(2 * h) * pair:(2 * h + 1) * pair, qpos],
                              qt_cur[(2 * h + 1) * pair:(2 * h + 2) * pair, qpos]], axis=1)
        kk = jnp.concatenate(key_window(j, h, ka_prev, ka_cur) + key_window(j, h, kb_prev, kb_cur),
                             axis=0)
        return jnp.dot(kk, qs, preferred_element_type=F32)

    def softmax(j, h, st):
        bias = bias_first if j == 0 else bias_mid
        out = []
        for par in range(2):
            s = st[par * 2 * WINDOW:(par + 1) * 2 * WINDOW, :] + bias
            sink = jnp.where(lane < WINDOW, sink_ref[4 * h + par], sink_ref[4 * h + 2 + par])
            m = jnp.maximum(jnp.max(s, axis=0, keepdims=True), sink)
            out.append((jnp.exp(s - m).astype(BF16), jnp.exp(sink - m)))
        return out

    def weighted_values(j, h, probs):
        qpos = slice(j * WINDOW, (j + 1) * WINDOW)
        vth = value_window(j, h)
        for par, (p, sink_term) in enumerate(probs):
            nd = jnp.dot(vth, p, preferred_element_type=F32)
            denom = nd[HEAD_DIM:HEAD_DIM + 1, :] + sink_term
            on = nd[:HEAD_DIM, :] * (1.0 / denom)
            for rp in range(2):
                head = 4 * h + 2 * rp + par
                ot_scr[head * HEAD_DIM:(head + 1) * HEAD_DIM, qpos] = on[:, rp * WINDOW:(rp + 1) * WINDOW]

    x = x_ref[0]
    xn = _rms(x, g_ref[...]).astype(BF16)
    tabs = (cos_ref[...], sina_ref[...], sinb_ref[...])
    cos_t = cost_ref[...]
    sin_t = sint_ref[...]
    half = ROT_DIM // 2

    def project(c):
        cols = slice(c * MXU_DIM, (c + 1) * MXU_DIM)
        t = jnp.dot(xn, wqkv_ref[:, cols], preferred_element_type=F32)
        if c * MXU_DIM < D_MODEL:
            tt = t.T
            heads = []
            for hl in range(MXU_DIM // HEAD_DIM):
                blk = tt[hl * HEAD_DIM:(hl + 1) * HEAD_DIM, :]
                ms = jnp.sum(blk * blk, axis=0, keepdims=True) * (1.0 / HEAD_DIM)
                gcol = gaint_ref[c * MXU_DIM + hl * HEAD_DIM:c * MXU_DIM + (hl + 1) * HEAD_DIM, :]
                y = blk * lax.rsqrt(ms + NORM_EPS) * jnp.concatenate([gcol] * (TQ // LANES), axis=1)
                x1, x2 = y[:half], y[half:ROT_DIM]
                heads.append(jnp.concatenate(
                    [x1 * cos_t - x2 * sin_t, x2 * cos_t + x1 * sin_t, y[ROT_DIM:]], axis=0))
            qt_nxt[cols, :] = jnp.concatenate(heads, axis=0).astype(BF16)
        elif c * MXU_DIM < D_QK:
            t = _qk_norm_rope(t, ones_ref[...], gain_ref[:, cols], *tabs)
            kc_ref[0] = t[TQ - WINDOW:, :]
            for h in range(N_KV_HEADS):
                lo_part, hi_part = _lane_halves(t[:, (h // 2) * LANES:(h // 2 + 1) * LANES])
                if h % 2 == 0:
                    a, b = lo_part, pltpu.roll(lo_part, HEAD_DIM, 1)
                else:
                    a, b = pltpu.roll(hi_part, HEAD_DIM, 1), hi_part
                ka_nxt[h] = a.astype(BF16)
                kb_nxt[h] = b.astype(BF16)
        else:
            vc_ref[0] = t[TQ - WINDOW:, :]
            vt = t.T
            for h in range(N_KV_HEADS):
                vt_nxt[h] = vt[h * HEAD_DIM:(h + 1) * HEAD_DIM, :].astype(BF16)

    groups = [(j, h) for j in range(TQ // WINDOW) for h in range(N_KV_HEADS)]
    n_groups = len(groups)
    st, probs = {}, {}
    for n in range(n_groups + ATTN_AHEAD):
        if n < n_groups:
            st[n] = scores(*groups[n])
        if n in PROJ_SLOTS:
            project(PROJ_SLOTS.index(n))
        d = n - ATTN_AHEAD + 1
        if 0 <= d < n_groups:
            probs[d] = softmax(*groups[d], st.pop(d))
        d = n - ATTN_AHEAD
        if 0 <= d < n_groups:
            weighted_values(*groups[d], probs.pop(d))
    attn = ot_scr[...].T.astype(BF16)
    o_ref[0] = xprev_scr[...] + jnp.dot(attn, wo_ref[...], preferred_element_type=F32)

    xprev_scr[...] = x
    qt_cur[...] = qt_nxt[...]
    for prev, cur, nxt in ((ka_prev, ka_cur, ka_nxt), (kb_prev, kb_cur, kb_nxt)):
        prev[...] = cur[:, TQ - WINDOW:, :]
        cur[...] = nxt[...]
    vt_prev[:, 0:HEAD_DIM, :] = vt_cur[:, 0:HEAD_DIM, TQ - WINDOW:]
    vt_cur[:, 0:HEAD_DIM, :] = vt_nxt[...]


def _attn_prompt_call(x, g, wqkv, ones_bd, gain, cos, sina, sinb, bias, sinks, wo):
    bsz, seq, _ = x.shape
    bps = seq // TQ
    n_blocks = bsz * bps
    k_blk = pltpu.VMEM((N_KV_HEADS, TQ, LANES), BF16)
    k_tail = pltpu.VMEM((N_KV_HEADS, WINDOW, LANES), BF16)
    vt_blk = pltpu.VMEM((N_KV_HEADS, HEAD_DIM + ONES_ROWS, TQ), BF16)
    vt_tail = pltpu.VMEM((N_KV_HEADS, HEAD_DIM + ONES_ROWS, WINDOW), BF16)

    def proj_block(gi):
        blk = jnp.minimum(gi, n_blocks - 1)
        return blk // bps, blk % bps

    def attn_block(gi):
        blk = jnp.maximum(gi - 1, 0)
        return blk // bps, blk % bps

    tab_spec = pl.BlockSpec((TQ, LANES), lambda gi: (proj_block(gi)[1], 0))
    cache_spec = pl.BlockSpec((1, WINDOW, D_KV), lambda gi: (proj_block(gi)[0], 0, 0))
    half = ROT_DIM // 2
    gain_t = jnp.broadcast_to(gain[0, :D_MODEL, None], (D_MODEL, LANES))
    cos_t = cos[:, :half].T
    sin_t = sina[:, half:ROT_DIM].T
    tab_t_spec = pl.BlockSpec((half, TQ), lambda gi: (0, proj_block(gi)[1]))
    body = functools.partial(_attn_prompt_body, blocks_per_seq=bps)
    return pl.pallas_call(
        body,
        grid=(n_blocks + 1,),
        in_specs=[
            pl.BlockSpec((1, TQ, D_MODEL), lambda gi: (*proj_block(gi), 0)),
            _const_spec((1, D_MODEL)),
            _const_spec(wqkv.shape),
            _const_spec((MXU_DIM, MXU_DIM)),
            _const_spec((1, D_QK)),
            tab_spec, tab_spec, tab_spec,
            _const_spec((D_MODEL, LANES)),
            tab_t_spec, tab_t_spec,
            _const_spec((2 * WINDOW, WINDOW)),
            pl.BlockSpec(memory_space=pltpu.SMEM),
            _const_spec(wo.shape),
        ],
        out_specs=[
            pl.BlockSpec((1, TQ, D_MODEL), lambda gi: (*attn_block(gi), 0)),
            cache_spec, cache_spec,
        ],
        out_shape=[
            jax.ShapeDtypeStruct(x.shape, F32),
            jax.ShapeDtypeStruct((bsz, WINDOW, D_KV), F32),
            jax.ShapeDtypeStruct((bsz, WINDOW, D_KV), F32),
        ],
        scratch_shapes=[pltpu.VMEM((D_MODEL, TQ), BF16), k_blk, k_blk, vt_blk,
                        pltpu.VMEM((D_MODEL, TQ), F32),
                        pltpu.VMEM((TQ, D_MODEL), F32),
                        pltpu.VMEM((D_MODEL, TQ), BF16), k_blk, k_blk,
                        pltpu.VMEM((N_KV_HEADS, HEAD_DIM, TQ), BF16),
                        k_tail, k_tail, vt_tail],
        compiler_params=pltpu.CompilerParams(
            dimension_semantics=("arbitrary",), vmem_limit_bytes=VMEM_LIMIT),
        name="attn_prompt",
    )(x, g, wqkv, ones_bd, gain, cos, sina, sinb, gain_t, cos_t, sin_t, bias, sinks, wo)


SEQ_BLOCK = 16
SAMPLE_AHEAD = 6
N_KEYS_PAD = 2 * WINDOW


def _attn_sample_body(x_ref, ck_ref, cv_ref, g_ref, wqkv_ref, ones_ref, gain_ref, cos_ref, sina_ref,
                      sinb_ref, bias_ref, sink_ref, wo_ref, o_ref, kc_ref, vc_ref, wqkvb_ref, wob_ref,
                      kf_scr, vf_scr, oall_scr, *, s_len):
    rows = SEQ_BLOCK * s_len
    w_keep = WINDOW - s_len

    @pl.when(pl.program_id(0) == 0)
    def _():
        pad = (SEQ_BLOCK, N_KEYS_PAD - WINDOW, D_KV)
        kf_scr[:, WINDOW:, :] = jnp.zeros(pad, BF16)
        vf_scr[:, WINDOW:, :] = jnp.zeros(pad, BF16)
        wqkvb_ref[...] = wqkv_ref[...].astype(BF16)
        wob_ref[...] = wo_ref[...].astype(BF16)

    x = x_ref[...].reshape(rows, D_MODEL)
    xn = _rms(x, g_ref[...]).astype(BF16)
    qkv = jnp.dot(xn, wqkvb_ref[...], preferred_element_type=F32)
    qk = _qk_norm_rope(qkv[:, :D_QK], ones_ref[...], gain_ref[...],
                       cos_ref[...], sina_ref[...], sinb_ref[...])
    q = qk[:, :D_MODEL]
    k_new = qk[:, D_MODEL:].reshape(SEQ_BLOCK, s_len, D_KV)
    v_new = qkv[:, D_QK:].reshape(SEQ_BLOCK, s_len, D_KV)

    ck = ck_ref[...]
    cv = cv_ref[...]
    kc_ref[:, 0:w_keep, :] = ck[:, s_len:, :]
    vc_ref[:, 0:w_keep, :] = cv[:, s_len:, :]
    kc_ref[:, w_keep:, :] = k_new
    vc_ref[:, w_keep:, :] = v_new
    kf_scr[:, 0:WINDOW, :] = ck.astype(BF16)
    vf_scr[:, 0:WINDOW, :] = cv.astype(BF16)
    tile = 2 * SUBLANES
    zpad = jnp.zeros((SEQ_BLOCK, tile - s_len, D_KV), F32)
    kf_scr[:, WINDOW:WINDOW + tile, :] = jnp.concatenate([k_new, zpad], axis=1).astype(BF16)
    vf_scr[:, WINDOW:WINDOW + tile, :] = jnp.concatenate([v_new, zpad], axis=1).astype(BF16)

    bias = bias_ref[...]
    sink = sink_ref[...]
    nt_dims = (((1,), (1,)), ((), ()))
    zero_col = jnp.zeros((s_len, LANES), F32)
    def scores(s):
        qs = q[s * s_len:(s + 1) * s_len, :]
        tiles = []
        for h in range(N_KV_HEADS):
            for r in range(KV_REP):
                head = KV_REP * h + r
                src = qs[:, (head // 2) * LANES:(head // 2 + 1) * LANES]
                lo_part, hi_part = _lane_halves(src)
                part = lo_part if head % 2 == 0 else hi_part
                if head % 2 != h % 2:
                    part = pltpu.roll(part, HEAD_DIM, 1)
                tiles.append(jnp.concatenate([part, zero_col] if h < 2 else [zero_col, part], axis=1))
        wt = jnp.concatenate(tiles, axis=0).astype(BF16)
        return lax.dot_general(wt, kf_scr[s], nt_dims, preferred_element_type=F32)

    def softmax(sc):
        sc = sc + bias
        m = jnp.maximum(jnp.max(sc, axis=-1, keepdims=True), sink)
        p = jnp.exp(sc - m)
        denom = jnp.sum(p, axis=-1, keepdims=True) + jnp.exp(sink - m)
        return p.astype(BF16), 1.0 / denom

    def weighted_values(s, p, inv_denom):
        o = jnp.dot(p, vf_scr[s], preferred_element_type=F32) * inv_denom
        cols = []
        for c in range(N_HEADS // 2):
            pieces = []
            for head in (2 * c, 2 * c + 1):
                h, r = divmod(head, KV_REP)
                blk = o[(KV_REP * h + r) * s_len:(KV_REP * h + r + 1) * s_len,
                        (h // 2) * LANES:(h // 2 + 1) * LANES]
                lo_part, hi_part = _lane_halves(blk)
                part = lo_part if h % 2 == 0 else hi_part
                if h % 2 != head % 2:
                    part = pltpu.roll(part, HEAD_DIM, 1)
                pieces.append(part)
            cols.append(pieces[0] + pieces[1])
        oall_scr[s * s_len:(s + 1) * s_len, :] = jnp.concatenate(cols, axis=-1)

    sc, probs = {}, {}
    for n in range(SEQ_BLOCK + SAMPLE_AHEAD):
        if n < SEQ_BLOCK:
            sc[n] = scores(n)
        d = n - SAMPLE_AHEAD // 2
        if 0 <= d < SEQ_BLOCK:
            probs[d] = softmax(sc.pop(d))
        d = n - SAMPLE_AHEAD
        if 0 <= d < SEQ_BLOCK:
            weighted_values(d, *probs.pop(d))

    out = jnp.dot(oall_scr[...].astype(BF16), wob_ref[...], preferred_element_type=F32)
    o_ref[...] = (x + out).reshape(SEQ_BLOCK, s_len, D_MODEL)


def _attn_sample_call(x, ck, cv, g, wqkv, ones_bd, gain, cos, sina, sinb, bias, sink_rows, wo, *, layer):
    bsz, s_len, _ = x.shape
    rows = SEQ_BLOCK * s_len
    body = functools.partial(_attn_sample_body, s_len=s_len)
    seq_spec = lambda shape: pl.BlockSpec(shape, lambda i: (i, 0, 0))
    cache_spec = pl.BlockSpec((None, SEQ_BLOCK, WINDOW, D_KV), lambda i: (layer, i, 0, 0))
    return pl.pallas_call(
        body,
        grid=(bsz // SEQ_BLOCK,),
        in_specs=[
            seq_spec((SEQ_BLOCK, s_len, D_MODEL)),
            cache_spec,
            cache_spec,
            _const_spec((1, D_MODEL)),
            _layer_spec(wqkv, layer),
            _const_spec((MXU_DIM, MXU_DIM)),
            _const_spec((1, D_QK)),
            _const_spec((rows, LANES)),
            _const_spec((rows, LANES)),
            _const_spec((rows, LANES)),
            _const_spec((N_HEADS * s_len, N_KEYS_PAD)),
            _const_spec((N_HEADS * s_len, 1)),
            _layer_spec(wo, layer),
        ],
        out_specs=[
            seq_spec((SEQ_BLOCK, s_len, D_MODEL)),
            seq_spec((SEQ_BLOCK, WINDOW, D_KV)),
            seq_spec((SEQ_BLOCK, WINDOW, D_KV)),
            pl.BlockSpec(wqkv.shape[1:], lambda i: (0, 0)),
            pl.BlockSpec(wo.shape[1:], lambda i: (0, 0)),
        ],
        out_shape=[
            jax.ShapeDtypeStruct(x.shape, F32),
            jax.ShapeDtypeStruct(ck.shape[1:], F32),
            jax.ShapeDtypeStruct(cv.shape[1:], F32),
            jax.ShapeDtypeStruct(wqkv.shape[1:], BF16),
            jax.ShapeDtypeStruct(wo.shape[1:], BF16),
        ],
        scratch_shapes=[
            pltpu.VMEM((SEQ_BLOCK, N_KEYS_PAD, D_KV), BF16),
            pltpu.VMEM((SEQ_BLOCK, N_KEYS_PAD, D_KV), BF16),
            pltpu.VMEM((rows, D_MODEL), F32),
        ],
        compiler_params=pltpu.CompilerParams(
            dimension_semantics=("arbitrary",), vmem_limit_bytes=VMEM_LIMIT),
        name="attn_sample",
    )(x, ck, cv, g, wqkv, ones_bd, gain, cos, sina, sinb, bias, sink_rows, wo)


def _ssm_weights(a_re, a_im, log_dt, b_re, b_im, c_re, c_im):
    nl = a_re.shape[0]
    dt = jnp.exp(log_dt)[..., None]
    mag = jnp.exp(a_re * dt)
    lam_re = mag * jnp.cos(a_im * dt)
    lam_im = mag * jnp.sin(a_im * dt)
    den = a_re * a_re + a_im * a_im
    nr = lam_re - 1.0
    ni = lam_im
    f_re = (nr * a_re + ni * a_im) / den
    f_im = (ni * a_re - nr * a_im) / den
    bb_re = f_re[..., None] * b_re - f_im[..., None] * b_im
    bb_im = f_re[..., None] * b_im + f_im[..., None] * b_re
    gpt = MXU_DIM // SSM_GROUP
    n_st = gpt * STATE_DIM
    same_group = jnp.asarray(
        (np.arange(MXU_DIM)[:, None] // SSM_GROUP) == (np.arange(n_st)[None, :] // STATE_DIM))

    def b_tiles(bb):
        t = jnp.swapaxes(bb, 2, 3).reshape(nl, N_GTILES, MXU_DIM, STATE_DIM)
        return jnp.where(same_group, jnp.tile(t, (1, 1, 1, gpt)), 0.0)

    def c_tiles(cc):
        t = jnp.swapaxes(cc, 2, 3).reshape(nl, N_GTILES, n_st, SSM_GROUP)
        return jnp.where(same_group.T, jnp.tile(t, (1, 1, 1, gpt)), 0.0)

    wb_re = b_tiles(bb_re).astype(BF16)
    wb_im = b_tiles(bb_im).astype(BF16)
    wb = jnp.concatenate([part[..., j * LANES:(j + 1) * LANES]
                          for j in range(SLABS_PER_TILE) for part in (wb_re, wb_im)], axis=-1)
    wc = jnp.concatenate([c_tiles(c_re), -c_tiles(c_im)], axis=2).astype(BF16)
    lam = jnp.concatenate([lam_re.reshape(nl, N_SLABS, 1, LANES), lam_im.reshape(nl, N_SLABS, 1, LANES)], axis=1)
    lam = jnp.broadcast_to(lam, (nl, 2 * N_SLABS, SUBLANES, LANES))
    return wb, wc, lam


def _rope_tables(pos):
    half = ROT_DIM // 2
    inv_freq = ROPE_THETA ** (-jnp.arange(half, dtype=F32) * 2.0 / ROT_DIM)
    ang = pos[:, None] * inv_freq[None, :]
    cos, sin = jnp.cos(ang), jnp.sin(ang)
    n = pos.shape[0]
    ones = jnp.ones((n, HEAD_DIM - ROT_DIM), F32)
    zeros = jnp.zeros((n, HEAD_DIM - ROT_DIM), F32)
    zh = jnp.zeros((n, half), F32)
    cos_h = jnp.concatenate([cos, cos, ones], axis=1)
    sina_h = jnp.concatenate([zh, sin, zeros], axis=1)
    sinb_h = jnp.concatenate([-sin, zh, zeros], axis=1)
    return tuple(jnp.concatenate([t, t], axis=1) for t in (cos_h, sina_h, sinb_h))


def _ones_block_diag():
    idx = np.arange(MXU_DIM) // HEAD_DIM
    return jnp.asarray(idx[:, None] == idx[None, :], dtype=BF16)


def _prompt_bias():
    js = np.arange(2 * WINDOW)[:, None]
    iq = np.arange(WINDOW)[None, :]
    diff = WINDOW + iq - js
    return jnp.asarray(np.where((diff >= 0) & (diff < WINDOW), 0.0, NEG_INF), dtype=F32)


def _sample_bias(s_len):
    t = (np.arange(N_HEADS * s_len) % s_len)[:, None]
    key = np.arange(N_KEYS_PAD)[None, :]
    diff = np.where(key < WINDOW, WINDOW + t - key, t - (key - WINDOW))
    ok = (diff >= 0) & (diff < WINDOW) & (key < WINDOW + s_len)
    return jnp.asarray(np.where(ok, 0.0, NEG_INF), dtype=F32)


SSM_PROMPT_TC = 64
SSM_SAMPLE_BB = 32
FFN_TM = 1024


def kernel(x_prompt, x_sample, state_ssm_re, state_ssm_im, cache_swa_k, cache_swa_v, norm_mix, norm_ffn, ssm_a_re, ssm_a_im, ssm_log_dt, ssm_b_re, ssm_b_im, ssm_c_re, ssm_c_im, ssm_d, ssm_w_glu, attn_w_qkv, attn_q_norm, attn_k_norm, attn_sinks, attn_w_o, ffn_w_gate_up, ffn_w_down):
    bsz, seq, _ = x_prompt.shape
    dbsz, dseq, _ = x_sample.shape
    yp, ys = x_prompt, x_sample
    p_re, p_im, p_k, p_v, s_re, s_im, s_k, s_v = ([] for _ in range(8))

    ones_bd = _ones_block_diag()
    p_tabs = _rope_tables(jnp.arange(seq, dtype=F32))
    s_pos = PAST_LEN + jnp.arange(dseq, dtype=F32)
    s_tabs = tuple(jnp.tile(t, (SEQ_BLOCK, 1)) for t in _rope_tables(s_pos))
    p_bias = _prompt_bias()
    s_bias = _sample_bias(dseq)
    h0 = jnp.zeros((bsz, N_STATE), F32)

    wb, wc, lam = _ssm_weights(ssm_a_re, ssm_a_im, ssm_log_dt, ssm_b_re, ssm_b_im, ssm_c_re, ssm_c_im)
    n_swa = cache_swa_k.shape[0]
    ck_all = cache_swa_k.reshape(n_swa, dbsz, WINDOW, D_KV)
    cv_all = cache_swa_v.reshape(n_swa, dbsz, WINDOW, D_KV)

    for i in range(DEPTH):
        j = i // 2
        g_mix = norm_mix[i].reshape(1, D_MODEL)
        if i % 2 == 0:
            d = ssm_d[j].reshape(1, D_MODEL)
            ys, sr, si, wglu = _ssm_call(ys, g_mix, wb, wc, lam, d, ssm_w_glu,
                                         state_ssm_re[j].reshape(dbsz, N_STATE),
                                         state_ssm_im[j].reshape(dbsz, N_STATE),
                                         layer=j, bb=SSM_SAMPLE_BB, tc=dseq)
            yp, hr, hi = _ssm_call(yp, g_mix, wb, wc, lam, d, wglu, h0, h0,
                                   layer=j, bb=bsz, tc=SSM_PROMPT_TC)
            shp = (N_GROUPS, STATE_DIM)
            p_re.append(hr.reshape(bsz, *shp))
            p_im.append(hi.reshape(bsz, *shp))
            s_re.append(sr.reshape(dbsz, *shp))
            s_im.append(si.reshape(dbsz, *shp))
        else:
            gain = jnp.concatenate([jnp.tile(attn_q_norm[j] * ATTN_SCALE, N_HEADS),
                                    jnp.tile(attn_k_norm[j], N_KV_HEADS)]).reshape(1, D_QK)
            sinks = attn_sinks[j]
            sink_rows = jnp.repeat(sinks, dseq).reshape(N_HEADS * dseq, 1)
            ys, ks_, vs_, wqkv, wo = _attn_sample_call(ys, ck_all, cv_all, g_mix, attn_w_qkv, ones_bd, gain, *s_tabs,
                                                       s_bias, sink_rows, attn_w_o, layer=j)
            yp, kp, vp = _attn_prompt_call(yp, g_mix, wqkv, ones_bd, gain, *p_tabs, p_bias, sinks, wo)
            kv_shape = (WINDOW, N_KV_HEADS, HEAD_DIM)
            p_k.append(kp.reshape(bsz, *kv_shape))
            p_v.append(vp.reshape(bsz, *kv_shape))
            s_k.append(ks_.reshape(dbsz, *kv_shape))
            s_v.append(vs_.reshape(dbsz, *kv_shape))
        g_ffn = norm_ffn[i].reshape(1, D_MODEL)
        ys, wg, wu, wd = _ffn_cast_call(ys.reshape(dbsz * dseq, D_MODEL), g_ffn, ffn_w_gate_up, ffn_w_down, layer=i)
        ys = ys.reshape(dbsz, dseq, D_MODEL)
        yp = _ffn_call(yp.reshape(bsz * seq, D_MODEL), g_ffn, wg, wu, wd, tm=FFN_TM).reshape(bsz, seq, D_MODEL)

    return (yp, ys, jnp.stack(p_re), jnp.stack(p_im), jnp.stack(p_k), jnp.stack(p_v),
            jnp.stack(s_re), jnp.stack(s_im), jnp.stack(s_k), jnp.stack(s_v))
```
